```python
import math
import jax
import jax.numpy as jnp
from jax import lax
import numpy as np

D_MODEL = 2048
BATCH = 8
SEQ = 2048
DEPTH = 4
DEC_BATCH = 32
DEC_SEQ = 64
PAST_LEN = 2048

CHUNK = 64
N_MIXERS = 3
N_GLA = len(range(0, DEPTH, N_MIXERS))
N_RWKV = len(range(1, DEPTH, N_MIXERS))
N_RET = len(range(2, DEPTH, N_MIXERS))
N_DENSE = len(range(0, DEPTH, 2))
N_MOE = len(range(1, DEPTH, 2))
DEEPNORM_ALPHA = (2.0 * DEPTH) ** 0.25
DEEPNORM_BETA = (8.0 * DEPTH) ** -0.25
LN_EPS = 1e-5

GLA_HEADS = 4
GLA_DK = D_MODEL // 2 // GLA_HEADS
GLA_DV = D_MODEL // GLA_HEADS
GLA_GATE_RANK = 16
GLA_TAU = 16.0
GLA_NORM_EPS = 1e-5

RWKV_HEAD = 64
RWKV_HEADS = D_MODEL // RWKV_HEAD
RWKV_DECAY_LORA = max(32, int(round(1.8 * D_MODEL ** 0.5 / 32)) * 32)
RWKV_AAA_LORA = max(32, int(round(1.8 * D_MODEL ** 0.5 / 32)) * 32)
RWKV_GATE_LORA = max(32, int(round(0.6 * D_MODEL ** 0.8 / 32)) * 32)
RWKV_GN_EPS = 64e-5

RET_HEADS = 8
RET_DK = D_MODEL // RET_HEADS
RET_DV = 2 * D_MODEL // RET_HEADS
RET_ROPE_BASE = 10000.0
RET_GN_EPS = 1e-5

FFN_DIM = 7 * D_MODEL // 2
N_EXPERTS = 8
MOE_TOP_K = 2
MOE_BLOCK = 256

F32 = jnp.float32

kernel_name = 'hybrid_gla_rwkv7_retnet_streaming_step'


def layer_norm(x, g, b):
    xf = x.astype(F32)
    mu = jnp.mean(xf, -1, keepdims=True)
    var = jnp.mean(jnp.square(xf - mu), -1, keepdims=True)
    return ((xf - mu) * lax.rsqrt(var + LN_EPS) * g.astype(F32) + b.astype(F32)).astype(x.dtype)


def rms_norm_last(y, eps):
    y = y.astype(F32)
    return y * lax.rsqrt(jnp.mean(jnp.square(y), -1, keepdims=True) + eps)


def group_norm_last(y, eps):
    y = y.astype(F32)
    mu = jnp.mean(y, -1, keepdims=True)
    var = jnp.mean(jnp.square(y - mu), -1, keepdims=True)
    return (y - mu) * lax.rsqrt(var + eps)


def split_heads(a, n_heads):
    b, t, hd = a.shape
    return a.reshape(b, t, n_heads, hd // n_heads).transpose(0, 2, 1, 3)


def merge_heads(a):
    b, n, t, d = a.shape
    return a.transpose(0, 2, 1, 3).reshape(b, t, n * d)


def decay_linear_attention(q, k, v, log_decay, s0):
    b, h, t, _ = q.shape
    dv = v.shape[-1]
    n_blk = -(-t // CHUNK)
    pad = n_blk * CHUNK - t

    def to_blocks(a):
        a = jnp.pad(a.astype(F32), ((0, 0), (0, 0), (0, pad), (0, 0)))
        return jnp.moveaxis(a.reshape(b, h, n_blk, CHUNK, a.shape[-1]), 2, 0)

    causal = jnp.tril(jnp.ones((CHUNK, CHUNK), dtype=bool))

    def step(s, blk):
        qi, ki, vi, gi = blk
        cum = jnp.cumsum(gi, axis=2)
        total = cum[:, :, -1:, :]
        q_dec = qi * jnp.exp(cum)
        k_inv = ki * jnp.exp(-cum)
        k_tail = ki * jnp.exp(total - cum)
        scores = jnp.where(causal, jnp.einsum('bhtd,bhsd->bhts', q_dec, k_inv), 0.0)
        o = jnp.einsum('bhtd,bhdv->bhtv', q_dec, s) + jnp.einsum('bhts,bhsv->bhtv', scores, vi)
        s_new = jnp.exp(total[:, :, 0, :])[..., None] * s + jnp.einsum('bhsd,bhsv->bhdv', k_tail, vi)
        return s_new, o

    s_fin, o = lax.scan(step, s0.astype(F32), (to_blocks(q), to_blocks(k), to_blocks(v), to_blocks(log_decay)))
    o = jnp.moveaxis(o, 0, 2).reshape(b, h, n_blk * CHUNK, dv)[:, :, :t]
    return o, s_fin


def rotate_pairs(a, pos):
    half = a.shape[-1] // 2
    inv_freq = 1.0 / (RET_ROPE_BASE ** jnp.linspace(0.0, 1.0, half, dtype=F32))
    ang = pos.astype(F32)[:, None] * inv_freq[None, :]
    cos, sin = jnp.cos(ang), jnp.sin(ang)
    a = a.astype(F32)
    a_even, a_odd = a[..., 0::2], a[..., 1::2]
    return jnp.stack([a_even * cos - a_odd * sin, a_even * sin + a_odd * cos], -1).reshape(a.shape)


def gla_mixer(x, s0, w_q, w_k, w_v, w_r, w_a1, w_a2, b_a, norm_g, w_o):
    q = split_heads(x @ w_q, GLA_HEADS) * (GLA_DK ** -0.5)
    k = split_heads(x @ w_k, GLA_HEADS)
    v = split_heads(x @ w_v, GLA_HEADS)
    log_alpha = jax.nn.log_sigmoid(((x @ w_a1) @ w_a2 + b_a).astype(F32)) / GLA_TAU
    o, s = decay_linear_attention(q, k, v, split_heads(log_alpha, GLA_HEADS), s0)
    o = rms_norm_last(o, GLA_NORM_EPS) * norm_g.astype(F32)
    out = (merge_heads(o).astype(x.dtype) * jax.nn.silu(x @ w_r)) @ w_o
    return out, s


def rwkv7_recurrence(r, log_w, k, v, a, bvec, s0):
    def step(s, inp):
        r_t, lw_t, k_t, v_t, a_t, b_t = inp
        sa = jnp.einsum('bhij,bhj->bhi', s, a_t)
        s = (s * jnp.exp(lw_t)[:, :, None, :] + sa[..., None] * b_t[:, :, None, :]
             + v_t[..., None] * k_t[:, :, None, :])
        return s, jnp.einsum('bhij,bhj->bhi', s, r_t)

    xs = tuple(jnp.swapaxes(u, 0, 1) for u in (r, log_w, k, v, a, bvec))
    s, y = lax.scan(step, s0.astype(F32), xs)
    return jnp.swapaxes(y, 0, 1), s


def rwkv7_mixer(x, shift0, s0, mu, w_r, w_k, w_v, w_o, w0, w1, w2, a0, a1, a2, g1, g2,
                k_k, k_a, r_k, gn_g, gn_b):
    b, t, d = x.shape
    x_prev = jnp.concatenate([shift0[:, None, :].astype(x.dtype), x[:, :-1]], axis=1)
    xx = x_prev - x
    xr, xw, xk, xv, xa, xg = (x + xx * mu[j] for j in range(6))
    r = xr @ w_r
    k = xk @ w_k
    v = xv @ w_v
    w_log = -jax.nn.softplus(-(w0 + jnp.tanh(xw @ w1) @ w2).astype(F32)) - 0.5
    log_decay = -jnp.exp(w_log)
    a = jax.nn.sigmoid((a0 + (xa @ a1) @ a2).astype(F32))
    g = jax.nn.sigmoid(xg @ g1) @ g2

    def heads(u):
        return u.astype(F32).reshape(b, t, RWKV_HEADS, RWKV_HEAD)

    kk = heads(k.astype(F32) * k_k.astype(F32))
    kk = kk / jnp.maximum(jnp.sqrt(jnp.sum(jnp.square(kk), -1, keepdims=True)), 1e-12)
    k_h = heads(k.astype(F32) * (1.0 + (a - 1.0) * k_a.astype(F32)))
    a_h = heads(a)
    r_h, v_h = heads(r), heads(v)
    y, s = rwkv7_recurrence(r_h, heads(log_decay), k_h, v_h, -kk, kk * a_h, s0)
    y = group_norm_last(y, RWKV_GN_EPS).reshape(b, t, d) * gn_g.astype(F32) + gn_b.astype(F32)
    bonus = jnp.sum(r_h * k_h * r_k.astype(F32), -1, keepdims=True) * v_h
    y = y + bonus.reshape(b, t, d)
    out = (y.astype(x.dtype) * g) @ w_o
    return out, x[:, -1], s


def retention_mixer(x, pos, s0, w_q, w_k, w_v, w_g, gn_g, w_o):
    b, t, _ = x.shape
    q = rotate_pairs(split_heads(x @ w_q, RET_HEADS), pos)
    k = rotate_pairs(split_heads(x @ w_k, RET_HEADS), pos) * (RET_DK ** -0.5)
    v = split_heads(x @ w_v, RET_HEADS)
    log_gamma = jnp.log1p(-jnp.exp2(-5.0 - jnp.arange(RET_HEADS, dtype=F32)))
    log_decay = jnp.broadcast_to(log_gamma[None, :, None, None], (b, RET_HEADS, t, RET_DK))
    o, s = decay_linear_attention(q, k, v, log_decay, s0)
    o = merge_heads(group_norm_last(o, RET_GN_EPS)) * gn_g.astype(F32)
    out = (jax.nn.silu(x @ w_g) * o.astype(x.dtype)) @ w_o
    return out, s


def swiglu(x, w1, w3, w2):
    return (jax.nn.silu(x @ w1) * (x @ w3)) @ w2


def moe_swiglu(x, w_router, w1, w3, w2):
    b, t, d = x.shape
    xt = x.reshape(-1, d)
    n = xt.shape[0]
    logits = (xt @ w_router).astype(F32)
    top_val, top_idx = lax.top_k(logits, MOE_TOP_K)
    gates = jax.nn.softmax(top_val, axis=-1)
    slots = n * MOE_TOP_K
    flat_e = top_idx.reshape(-1)
    flat_tok = jnp.repeat(jnp.arange(n, dtype=jnp.int32), MOE_TOP_K)
    flat_gate = gates.reshape(-1)
    order = jnp.argsort(flat_e)
    se, stok, sgate = flat_e[order], flat_tok[order], flat_gate[order]
    counts = jnp.bincount(flat_e, length=N_EXPERTS)
    start = jnp.cumsum(counts) - counts
    padded = (counts + MOE_BLOCK - 1) // MOE_BLOCK * MOE_BLOCK
    pend = jnp.cumsum(padded)
    pstart = pend - padded
    n_blocks = (slots + N_EXPERTS * (MOE_BLOCK - 1) + MOE_BLOCK - 1) // MOE_BLOCK
    dest = pstart[se] + jnp.arange(slots, dtype=jnp.int32) - start[se]
    buf = jnp.zeros((n_blocks * MOE_BLOCK, d), x.dtype).at[dest].set(xt[stok])
    block_start = jnp.arange(n_blocks, dtype=jnp.int32) * MOE_BLOCK
    block_expert = jnp.clip(jnp.searchsorted(pend, block_start, side='right'), 0, N_EXPERTS - 1)

    def expert_block(args):
        xb, e = args
        return swiglu(xb, w1[e], w3[e], w2[e])

    yb = lax.map(expert_block, (buf.reshape(n_blocks, MOE_BLOCK, d), block_expert))
    y_sorted = yb.reshape(-1, d)[dest] * sgate[:, None].astype(x.dtype)
    out = jnp.zeros((n, d), x.dtype).at[stok].add(y_sorted)
    return out.reshape(b, t, d)


def trunk(x, pos, st_gla, st_rwkv, st_shift, st_ret, p):
    new_gla, new_rwkv, new_shift, new_ret = [], [], [], []
    for i in range(DEPTH):
        kind, slot = i % N_MIXERS, i // N_MIXERS
        if kind == 0:
            h, s = gla_mixer(x, st_gla[slot], p['gla_wq'][slot], p['gla_wk'][slot], p['gla_wv'][slot],
                             p['gla_wr'][slot], p['gla_wa1'][slot], p['gla_wa2'][slot], p['gla_ba'][slot],
                             p['gla_norm_g'][slot], p['gla_wo'][slot])
            new_gla.append(s)
        elif kind == 1:
            h, sh, s = rwkv7_mixer(x, st_shift[slot], st_rwkv[slot], p['rwkv_mu'][slot], p['rwkv_wr'][slot],
                                   p['rwkv_wk'][slot], p['rwkv_wv'][slot], p['rwkv_wo'][slot], p['rwkv_w0'][slot],
                                   p['rwkv_w1'][slot], p['rwkv_w2'][slot], p['rwkv_a0'][slot], p['rwkv_a1'][slot],
                                   p['rwkv_a2'][slot], p['rwkv_g1'][slot], p['rwkv_g2'][slot], p['rwkv_k_k'][slot],
                                   p['rwkv_k_a'][slot], p['rwkv_r_k'][slot], p['rwkv_gn_g'][slot], p['rwkv_gn_b'][slot])
            new_shift.append(sh)
            new_rwkv.append(s)
        else:
            h, s = retention_mixer(x, pos, st_ret[slot], p['ret_wq'][slot], p['ret_wk'][slot], p['ret_wv'][slot],
                                   p['ret_wg'][slot], p['ret_gn_g'][slot], p['ret_wo'][slot])
            new_ret.append(s)
        x = layer_norm(DEEPNORM_ALPHA * x + h, p['ln_g'][i, 0], p['ln_b'][i, 0])
        fslot = i // 2
        if i % 2 == 0:
            h = swiglu(x, p['ffn_w1'][fslot], p['ffn_w3'][fslot], p['ffn_w2'][fslot])
        else:
            h = moe_swiglu(x, p['moe_router'][fslot], p['moe_w1'][fslot], p['moe_w3'][fslot], p['moe_w2'][fslot])
        x = layer_norm(DEEPNORM_ALPHA * x + h, p['ln_g'][i, 1], p['ln_b'][i, 1])
    return x, jnp.stack(new_gla), jnp.stack(new_rwkv), jnp.stack(new_shift), jnp.stack(new_ret)


def setup_inputs(seed: int = 0) -> dict:
    key = jax.random.key(seed)
    keys = iter(jax.random.split(key, 64))

    def nrm(shape, scale=1.0):
        return jax.random.normal(next(keys), shape, F32) * scale

    def uni(shape, lo, hi):
        return jax.random.uniform(next(keys), shape, F32, lo, hi)

    D = D_MODEL
    dk_a, dv_a = GLA_HEADS * GLA_DK, GLA_HEADS * GLA_DV
    dk_r, dv_r = RET_HEADS * RET_DK, RET_HEADS * RET_DV
    beta = DEEPNORM_BETA
    return {
        'x_prompt': nrm((BATCH, SEQ, D)),
        'x_sample': nrm((DEC_BATCH, DEC_SEQ, D)),
        'state_gla': nrm((N_GLA, DEC_BATCH, GLA_HEADS, GLA_DK, GLA_DV)),
        'state_rwkv': nrm((N_RWKV, DEC_BATCH, RWKV_HEADS, RWKV_HEAD, RWKV_HEAD)),
        'state_shift': nrm((N_RWKV, DEC_BATCH, D)),
        'state_ret': nrm((N_RET, DEC_BATCH, RET_HEADS, RET_DK, RET_DV)),
        'ln_g': 1.0 + nrm((DEPTH, 2, D), 0.02),
        'ln_b': nrm((DEPTH, 2, D), 0.02),
        'gla_wq': nrm((N_GLA, D, dk_a), D ** -0.5),
        'gla_wk': nrm((N_GLA, D, dk_a), D ** -0.5),
        'gla_wv': nrm((N_GLA, D, dv_a), D ** -0.5),
        'gla_wr': nrm((N_GLA, D, dv_a), D ** -0.5),
        'gla_wa1': nrm((N_GLA, D, GLA_GATE_RANK), D ** -0.5),
        'gla_wa2': nrm((N_GLA, GLA_GATE_RANK, dk_a), GLA_GATE_RANK ** -0.5),
        'gla_ba': nrm((N_GLA, dk_a), 0.1),
        'gla_norm_g': 1.0 + nrm((N_GLA, GLA_DV), 0.02),
        'gla_wo': nrm((N_GLA, dv_a, D), beta * dv_a ** -0.5),
        'rwkv_mu': uni((N_RWKV, 6, D), 0.0, 1.0),
        'rwkv_wr': nrm((N_RWKV, D, D), D ** -0.5),
        'rwkv_wk': nrm((N_RWKV, D, D), D ** -0.5),
        'rwkv_wv': nrm((N_RWKV, D, D), D ** -0.5),
        'rwkv_wo': nrm((N_RWKV, D, D), beta * D ** -0.5),
        'rwkv_w0': uni((N_RWKV, D), -6.0, 1.0),
        'rwkv_w1': nrm((N_RWKV, D, RWKV_DECAY_LORA), D ** -0.5),
        'rwkv_w2': nrm((N_RWKV, RWKV_DECAY_LORA, D), 0.1 * RWKV_DECAY_LORA ** -0.5),
        'rwkv_a0': nrm((N_RWKV, D), 0.1),
        'rwkv_a1': nrm((N_RWKV, D, RWKV_AAA_LORA), D ** -0.5),
        'rwkv_a2': nrm((N_RWKV, RWKV_AAA_LORA, D), 0.1 * RWKV_AAA_LORA ** -0.5),
        'rwkv_g1': nrm((N_RWKV, D, RWKV_GATE_LORA), D ** -0.5),
        'rwkv_g2': nrm((N_RWKV, RWKV_GATE_LORA, D), RWKV_GATE_LORA ** -0.5),
        'rwkv_k_k': 0.85 + nrm((N_RWKV, D), 0.02),
        'rwkv_k_a': 1.0 + nrm((N_RWKV, D), 0.02),
        'rwkv_r_k': nrm((N_RWKV, RWKV_HEADS, RWKV_HEAD), 0.1),
        'rwkv_gn_g': 1.0 + nrm((N_RWKV, D), 0.02),
        'rwkv_gn_b': nrm((N_RWKV, D), 0.02),
        'ret_wq': nrm((N_RET, D, dk_r), D ** -0.5),
        'ret_wk': nrm((N_RET, D, dk_r), D ** -0.5),
        'ret_wv': nrm((N_RET, D, dv_r), D ** -0.5),
        'ret_wg': nrm((N_RET, D, dv_r), D ** -0.5),
        'ret_gn_g': 1.0 + nrm((N_RET, dv_r), 0.02),
        'ret_wo': nrm((N_RET, dv_r, D), beta * dv_r ** -0.5),
        'ffn_w1': nrm((N_DENSE, D, FFN_DIM), D ** -0.5),
        'ffn_w3': nrm((N_DENSE, D, FFN_DIM), D ** -0.5),
        'ffn_w2': nrm((N_DENSE, FFN_DIM, D), beta * FFN_DIM ** -0.5),
        'moe_router': nrm((N_MOE, D, N_EXPERTS), D ** -0.5),
        'moe_w1': nrm((N_MOE, N_EXPERTS, D, FFN_DIM), D ** -0.5),
        'moe_w3': nrm((N_MOE, N_EXPERTS, D, FFN_DIM), D ** -0.5),
        'moe_w2': nrm((N_MOE, N_EXPERTS, FFN_DIM, D), beta * FFN_DIM ** -0.5),
    }


def reference(x_prompt, x_sample, state_gla, state_rwkv, state_shift, state_ret, ln_g, ln_b,
              gla_wq, gla_wk, gla_wv, gla_wr, gla_wa1, gla_wa2, gla_ba, gla_norm_g, gla_wo,
              rwkv_mu, rwkv_wr, rwkv_wk, rwkv_wv, rwkv_wo, rwkv_w0, rwkv_w1, rwkv_w2, rwkv_a0, rwkv_a1,
              rwkv_a2, rwkv_g1, rwkv_g2, rwkv_k_k, rwkv_k_a, rwkv_r_k, rwkv_gn_g, rwkv_gn_b,
              ret_wq, ret_wk, ret_wv, ret_wg, ret_gn_g, ret_wo,
              ffn_w1, ffn_w3, ffn_w2, moe_router, moe_w1, moe_w3, moe_w2):
    p = dict(ln_g=ln_g, ln_b=ln_b,
             gla_wq=gla_wq, gla_wk=gla_wk, gla_wv=gla_wv, gla_wr=gla_wr, gla_wa1=gla_wa1, gla_wa2=gla_wa2,
             gla_ba=gla_ba, gla_norm_g=gla_norm_g, gla_wo=gla_wo,
             rwkv_mu=rwkv_mu, rwkv_wr=rwkv_wr, rwkv_wk=rwkv_wk, rwkv_wv=rwkv_wv, rwkv_wo=rwkv_wo,
             rwkv_w0=rwkv_w0, rwkv_w1=rwkv_w1, rwkv_w2=rwkv_w2, rwkv_a0=rwkv_a0, rwkv_a1=rwkv_a1,
             rwkv_a2=rwkv_a2, rwkv_g1=rwkv_g1, rwkv_g2=rwkv_g2, rwkv_k_k=rwkv_k_k, rwkv_k_a=rwkv_k_a,
             rwkv_r_k=rwkv_r_k, rwkv_gn_g=rwkv_gn_g, rwkv_gn_b=rwkv_gn_b,
             ret_wq=ret_wq, ret_wk=ret_wk, ret_wv=ret_wv, ret_wg=ret_wg, ret_gn_g=ret_gn_g, ret_wo=ret_wo,
             ffn_w1=ffn_w1, ffn_w3=ffn_w3, ffn_w2=ffn_w2,
             moe_router=moe_router, moe_w1=moe_w1, moe_w3=moe_w3, moe_w2=moe_w2)
    bp, tp, _ = x_prompt.shape
    zero_gla = jnp.zeros((N_GLA, bp, GLA_HEADS, GLA_DK, GLA_DV), F32)
    zero_rwkv = jnp.zeros((N_RWKV, bp, RWKV_HEADS, RWKV_HEAD, RWKV_HEAD), F32)
    zero_shift = jnp.zeros((N_RWKV, bp, D_MODEL), x_prompt.dtype)
    zero_ret = jnp.zeros((N_RET, bp, RET_HEADS, RET_DK, RET_DV), F32)
    pos_prompt = jnp.arange(tp, dtype=jnp.int32)
    pos_sample = PAST_LEN + jnp.arange(x_sample.shape[1], dtype=jnp.int32)
    y_prompt, gla_p, rwkv_p, shift_p, ret_p = trunk(x_prompt, pos_prompt, zero_gla, zero_rwkv, zero_shift, zero_ret, p)
    y_sample, gla_s, rwkv_s, shift_s, ret_s = trunk(x_sample, pos_sample, state_gla, state_rwkv, state_shift, state_ret, p)
    return (y_prompt, y_sample, gla_p, rwkv_p, shift_p, ret_p, gla_s, rwkv_s, shift_s, ret_s)
```

```python
import functools
import math

import jax
import jax.numpy as jnp
from jax import lax
from jax.experimental import pallas as pl
from jax.experimental.pallas import tpu as pltpu

F32 = jnp.float32
BF16 = jnp.bfloat16

CHUNK = 64
LANES = 128
VMEM_LIMIT_BYTES = 56 * 1024 * 1024

LN_EPS = 1e-5
GLA_TAU = 16.0
GLA_NORM_EPS = 1e-5
RWKV_HEAD = 64
RWKV_GN_EPS = 64e-5
RET_ROPE_BASE = 10000.0
RET_GN_EPS = 1e-5
MOE_TOP_K = 2
MOE_BLOCK = 512

_HI = lax.Precision.HIGHEST


def _cparams(sem):
    return pltpu.CompilerParams(dimension_semantics=sem, vmem_limit_bytes=VMEM_LIMIT_BYTES)


def _bdot(a, b):
    return jnp.dot(a.astype(BF16), b.astype(BF16), preferred_element_type=F32)


def _bdot_nt(a, b):
    return lax.dot_general(a.astype(BF16), b.astype(BF16), (((1,), (1,)), ((), ())),
                           preferred_element_type=F32)


def _log_sigmoid(z):
    return -(jnp.maximum(-z, 0.0) + jnp.log1p(jnp.exp(-jnp.abs(z))))


def _act(name, z):
    if name == "none":
        return z
    if name == "silu":
        return z * jax.nn.sigmoid(z)
    if name == "sigmoid":
        return jax.nn.sigmoid(z)
    if name == "tanh":
        return jnp.tanh(z)
    if name == "gla_gate":
        return _log_sigmoid(z) / GLA_TAU
    if name == "rwkv_decay":
        return -jnp.exp(_log_sigmoid(z) - 0.5)
    raise ValueError(name)


def _mm_body(*refs, act, has_bias, nk, scale):
    if has_bias:
        x_ref, w_ref, b_ref, o_ref = refs[:4]
        scratch = refs[4:]
    else:
        x_ref, w_ref, o_ref = refs[:3]
        b_ref = None
        scratch = refs[3:]
    part = jnp.dot(x_ref[...], w_ref[...].astype(BF16), preferred_element_type=F32)

    def finish(acc):
        if scale != 1.0:
            acc = acc * scale
        if has_bias:
            acc = acc + b_ref[...]
        o_ref[...] = _act(act, acc).astype(o_ref.dtype)

    if nk == 1:
        finish(part)
    else:
        acc_ref = scratch[0]
        k = pl.program_id(2)

        @pl.when(k == 0)
        def _():
            acc_ref[...] = part

        @pl.when(k > 0)
        def _():
            acc_ref[...] += part

        @pl.when(k == nk - 1)
        def _():
            finish(acc_ref[...])


def _pick(n, pref):
    for t in pref:
        if n % t == 0:
            return t
    return n


def matmul(x, w, slot=0, *, act="none", bias=None, scale=1.0, out_dtype=F32):
    m, kdim = x.shape
    _, kw, n = w.shape
    assert kw == kdim, (w.shape, x.shape)
    tm = _pick(m, (1024, 512, 256, 128, 64, 32, 16, 8))
    tn = _pick(n, (512, 256, 128))
    tk = kdim if kdim <= 2048 else _pick(kdim, (2048, 1792, 1024, 512))
    nk = kdim // tk
    in_specs = [pl.BlockSpec((tm, tk), lambda i, j, k: (i, k)),
                pl.BlockSpec((None, tk, tn), lambda i, j, k: (slot, k, j))]
    args = [x, w]
    if bias is not None:
        in_specs.append(pl.BlockSpec((1, tn), lambda i, j, k: (0, j)))
        args.append(bias.reshape(1, n).astype(F32))
    scratch = [pltpu.VMEM((tm, tn), F32)] if nk > 1 else []
    return pl.pallas_call(
        functools.partial(_mm_body, act=act, has_bias=bias is not None, nk=nk, scale=scale),
        grid=(m // tm, n // tn, nk),
        in_specs=in_specs,
        out_specs=pl.BlockSpec((tm, tn), lambda i, j, k: (i, j)),
        out_shape=jax.ShapeDtypeStruct((m, n), out_dtype),
        scratch_shapes=scratch,
        compiler_params=_cparams(("parallel", "parallel", "arbitrary")),
        name="matmul_" + act,
    )(*args)


def _ln_body(x_ref, h_ref, g_ref, b_ref, of_ref, ob_ref, *, alpha):
    z = alpha * x_ref[...] + h_ref[...]
    mu = jnp.mean(z, -1, keepdims=True)
    zc = z - mu
    var = jnp.mean(zc * zc, -1, keepdims=True)
    y = zc * lax.rsqrt(var + LN_EPS) * g_ref[...] + b_ref[...]
    of_ref[...] = y
    ob_ref[...] = y.astype(BF16)


def residual_layer_norm(x, h, g, b, alpha):
    m, d = x.shape
    tm = _pick(m, (512, 256, 128, 64, 32, 16, 8))
    row = pl.BlockSpec((tm, d), lambda i: (i, 0))
    vec = pl.BlockSpec((1, d), lambda i: (0, 0))
    return pl.pallas_call(
        functools.partial(_ln_body, alpha=alpha),
        grid=(m // tm,),
        in_specs=[row, row, vec, vec],
        out_specs=[row, row],
        out_shape=[jax.ShapeDtypeStruct((m, d), F32), jax.ShapeDtypeStruct((m, d), BF16)],
        compiler_params=_cparams(("parallel",)),
        name="residual_layer_norm",
    )(x, h, g.reshape(1, d), b.reshape(1, d))


def _shift_mix_body(x_ref, xp_ref, mu_ref, o_ref):
    x = x_ref[...]
    xx = xp_ref[...] - x
    for j in range(o_ref.shape[0]):
        o_ref[j] = (x + xx * mu_ref[j:j + 1, :]).astype(BF16)


def shift_mix(x, x_prev, mu):
    m, d = x.shape
    nmix = mu.shape[0]
    tm = _pick(m, (512, 256, 128, 64, 32, 16, 8))
    row = pl.BlockSpec((tm, d), lambda i: (i, 0))
    return pl.pallas_call(
        _shift_mix_body,
        grid=(m // tm,),
        in_specs=[row, row, pl.BlockSpec((nmix, d), lambda i: (0, 0))],
        out_specs=pl.BlockSpec((nmix, tm, d), lambda i: (0, i, 0)),
        out_shape=jax.ShapeDtypeStruct((nmix, m, d), BF16),
        compiler_params=_cparams(("parallel",)),
        name="rwkv_shift_mix",
    )(x, x_prev, mu)


def _chunk_state_init(c, ncp, cps, state_ref, s0_ref):
    @pl.when(jnp.logical_and(c < ncp, c % cps == 0))
    def _():
        state_ref[...] = jnp.zeros(state_ref.shape, state_ref.dtype)

    @pl.when(c >= ncp)
    def _():
        state_ref[...] = s0_ref[0]


def _dla_body(*refs, heads, dk, dv, mode, ncp, cps):
    if mode == "gla":
        q_ref, k_ref, v_ref, g_ref, gate_ref, ng_ref, s0_ref, o_ref, sp_ref, ss_ref, st_ref = refs
    else:
        (q_ref, k_ref, v_ref, cos_ref, sin_ref, gate_ref, ng_ref, s0_ref,
         o_ref, sp_ref, ss_ref, st_ref) = refs
    c = pl.program_id(0)
    _chunk_state_init(c, ncp, cps, st_ref, s0_ref)

    t_row = lax.broadcasted_iota(jnp.int32, (CHUNK, CHUNK), 0)
    t_col = lax.broadcasted_iota(jnp.int32, (CHUNK, CHUNK), 1)
    causal = t_row >= t_col
    if mode == "gla":
        lower_ones = causal.astype(F32)
        ones_cols = jnp.ones((CHUNK, LANES), F32)
    else:
        width = heads * dk
        even = (lax.broadcasted_iota(jnp.int32, (CHUNK, width), 1) % 2) == 0
        q_all = q_ref[...]
        k_all = k_ref[...]
        q_sw = jnp.where(even, pltpu.roll(q_all, width - 1, 1), pltpu.roll(q_all, 1, 1))
        k_sw = jnp.where(even, pltpu.roll(k_all, width - 1, 1), pltpu.roll(k_all, 1, 1))
        cos = cos_ref[...]
        sin = sin_ref[...]
        frame = (lax.broadcasted_iota(jnp.int32, (CHUNK, 1), 0) + 1).astype(F32)

    for h in range(heads):
        ks = slice(h * dk, (h + 1) * dk)
        vs = slice(h * dv, (h + 1) * dv)
        v = v_ref[:, vs]
        s_prev = st_ref[h]
        if mode == "gla":
            q = q_ref[:, ks]
            k = k_ref[:, ks]
            g = g_ref[:, ks]
            cum = jnp.dot(lower_ones, g, precision=_HI, preferred_element_type=F32)
            total = cum[CHUNK - 1:CHUNK, :]
            total_col = lax.dot_general(g, ones_cols, (((0,), (0,)), ((), ())), precision=_HI,
                                        preferred_element_type=F32)[:, :1]
            state_decay = jnp.exp(total_col)
            q_dec = q * jnp.exp(cum)
            k_inv = k * jnp.exp(-cum)
            k_tail = k * jnp.exp(total - cum)
        else:
            q = q_all[:, ks] * cos + q_sw[:, ks] * sin
            k = (k_all[:, ks] * cos + k_sw[:, ks] * sin) * (dk ** -0.5)
            log_gamma = math.log1p(-(2.0 ** (-5.0 - h)))
            cum = frame * log_gamma
            total = CHUNK * log_gamma
            state_decay = math.exp(total)
            q_dec = q * jnp.exp(cum)
            k_inv = k * jnp.exp(-cum)
            k_tail = k * jnp.exp(total - cum)
        scores = jnp.where(causal, _bdot_nt(q_dec, k_inv), 0.0)
        o = _bdot(q_dec, s_prev) + _bdot(scores, v)
        k_tail_t = jnp.transpose(k_tail)
        st_ref[h] = state_decay * s_prev + _bdot(k_tail_t, v)
        if mode == "gla":
            o = o * lax.rsqrt(jnp.mean(o * o, -1, keepdims=True) + GLA_NORM_EPS) * ng_ref[...]
        else:
            mu = jnp.mean(o, -1, keepdims=True)
            oc = o - mu
            var = jnp.mean(oc * oc, -1, keepdims=True)
            o = oc * lax.rsqrt(var + RET_GN_EPS) * ng_ref[:, vs]
        if mode == "gla":
            o_ref[:, vs] = (o * gate_ref[:, vs]).astype(BF16)
        else:
            o_ref[:, vs] = (gate_ref[:, vs] * o).astype(BF16)

    @pl.when(c < ncp)
    def _():
        sp_ref[0] = st_ref[...]

    @pl.when(c >= ncp)
    def _():
        ss_ref[0] = st_ref[...]


def _seq_state_specs(state_shape, ncp, cps, n_prompt):
    blk = (1,) + tuple(state_shape)
    zeros = (0,) * len(state_shape)
    s0_spec = pl.BlockSpec(blk, lambda c: (jnp.maximum(c - ncp, 0),) + zeros)
    sp_spec = pl.BlockSpec(blk, lambda c: (jnp.minimum(c // cps, n_prompt - 1),) + zeros)
    ss_spec = pl.BlockSpec(blk, lambda c: (jnp.maximum(c - ncp, 0),) + zeros)
    return s0_spec, sp_spec, ss_spec


def decay_attention(q, k, v, extra, gate, norm_g, s0, *, mode, heads, n_prompt, cps):
    nt = q.shape[0]
    dk = q.shape[1] // heads
    dv = v.shape[1] // heads
    n_sample = s0.shape[0]
    ncp = n_prompt * cps
    nchunks = nt // CHUNK
    assert nchunks == ncp + n_sample
    rowk = pl.BlockSpec((CHUNK, heads * dk), lambda c: (c, 0))
    rowv = pl.BlockSpec((CHUNK, heads * dv), lambda c: (c, 0))
    s0_spec, sp_spec, ss_spec = _seq_state_specs((heads, dk, dv), ncp, cps, n_prompt)
    if mode == "gla":
        extra_specs = [rowk]
        ng_spec = pl.BlockSpec((1, dv), lambda c: (0, 0))
        norm_g = norm_g.reshape(1, dv)
    else:
        pos_spec = pl.BlockSpec((CHUNK, dk), lambda c: (jnp.where(c < ncp, c % cps, cps), 0))
        extra_specs = [pos_spec, pos_spec]
        ng_spec = pl.BlockSpec((1, heads * dv), lambda c: (0, 0))
        norm_g = norm_g.reshape(1, heads * dv)
    return pl.pallas_call(
        functools.partial(_dla_body, heads=heads, dk=dk, dv=dv, mode=mode, ncp=ncp, cps=cps),
        grid=(nchunks,),
        in_specs=[rowk, rowk, rowv] + extra_specs + [rowv, ng_spec, s0_spec],
        out_specs=[rowv, sp_spec, ss_spec],
        out_shape=[jax.ShapeDtypeStruct((nt, heads * dv), BF16),
                   jax.ShapeDtypeStruct((n_prompt, heads, dk, dv), F32),
                   jax.ShapeDtypeStruct((n_sample, heads, dk, dv), F32)],
        scratch_shapes=[pltpu.VMEM((heads, dk, dv), F32)],
        compiler_params=_cparams(("arbitrary",)),
        name="decay_attention_" + mode,
    )(q, k, v, *extra, gate, norm_g, s0)


def _rwkv_body(r_ref, k_ref, v_ref, lw_ref, a_ref, g_ref, kk_ref, ka_ref, rk_ref, gng_ref, gnb_ref,
               s0_ref, o_ref, sp_ref, ss_ref, st_ref, *, pairs, ncp, cps):
    c = pl.program_id(0)
    n = RWKV_HEAD
    w2 = 2 * n

    @pl.when(jnp.logical_and(c < ncp, c % cps == 0))
    def _():
        st_ref[...] = jnp.zeros(st_ref.shape, F32)

    @pl.when(c >= ncp)
    def _():
        zero = jnp.zeros((n, n), F32)
        for p in range(pairs):
            top = jnp.concatenate([s0_ref[0, 2 * p], zero], axis=1)
            bot = jnp.concatenate([zero, s0_ref[0, 2 * p + 1]], axis=1)
            st_ref[p] = jnp.concatenate([top, bot], axis=0)

    def paired(ref):
        x = ref[...]
        return jnp.stack([x[:, p * w2:(p + 1) * w2] for p in range(pairs)])

    lane = lax.broadcasted_iota(jnp.int32, (1, 1, w2), 2)
    first = lane < n

    def head_sum(x):
        s_a = jnp.sum(jnp.where(first, x, 0.0), -1, keepdims=True)
        s_b = jnp.sum(jnp.where(first, 0.0, x), -1, keepdims=True)
        return jnp.where(first, s_a, s_b)

    def stacked(x):
        return jnp.concatenate([jnp.where(first, x, 0.0), jnp.where(first, 0.0, x)], axis=1)

    def bmm(a, b):
        return lax.dot_general(a.astype(BF16), b.astype(BF16), (((2,), (1,)), ((0,), (0,))),
                               preferred_element_type=F32)

    def bmm_nt(a, b):
        return lax.dot_general(a.astype(BF16), b.astype(BF16), (((2,), (2,)), ((0,), (0,))),
                               preferred_element_type=F32)

    r = paired(r_ref)
    k_raw = paired(k_ref)
    v = paired(v_ref)
    lw = paired(lw_ref)
    a = paired(a_ref)
    k_k = paired(kk_ref)
    k_a = paired(ka_ref)
    r_k = paired(rk_ref)

    kk = k_raw * k_k
    kk = kk / jnp.maximum(jnp.sqrt(head_sum(kk * kk)), 1e-12)
    k_h = k_raw * (1.0 + (a - 1.0) * k_a)
    a_vec = -kk
    b_vec = kk * a

    t_row = lax.broadcasted_iota(jnp.int32, (CHUNK, CHUNK), 0)
    t_col = lax.broadcasted_iota(jnp.int32, (CHUNK, CHUNK), 1)
    lower_ones = (t_row >= t_col).astype(F32)
    cum = paired_value(jnp.dot(lower_ones, lw_ref[...], precision=_HI, preferred_element_type=F32),
                       pairs, w2)
    total = cum[:, CHUNK - 1:CHUNK, :]
    p_now = jnp.exp(cum)
    p_prev = jnp.exp(cum - lw)
    p_inv = jnp.exp(-cum)
    p_tail = jnp.exp(total - cum)

    lhs = jnp.concatenate([stacked(a_vec * p_prev), stacked(r * p_now)], axis=1)
    rhs = jnp.concatenate([stacked(b_vec * p_inv), stacked(k_h * p_inv)], axis=1)
    sc = bmm_nt(lhs, rhs)
    s_prev = st_ref[...]
    sr = bmm_nt(lhs, s_prev)

    i_row = lax.broadcasted_iota(jnp.int32, (1, w2, w2), 1)
    i_col = lax.broadcasted_iota(jnp.int32, (1, w2, w2), 2)
    same = (i_row // n) == (i_col // n)
    strict = jnp.logical_and(same, (i_col % n) < (i_row % n))
    incl = jnp.logical_and(same, (i_col % n) <= (i_row % n))
    a_ab = jnp.where(strict, sc[:, :w2, :w2], 0.0)
    a_ak = jnp.where(strict, sc[:, :w2, w2:], 0.0)
    a_rb = jnp.where(incl, sc[:, w2:, :w2], 0.0)
    a_rk = jnp.where(incl, sc[:, w2:, w2:], 0.0)
    u0 = sr[:, :w2]
    y0 = sr[:, w2:]
    v_st = stacked(v)

    eye = (i_row == i_col).astype(F32)
    t_inv = eye + a_ab
    power = a_ab
    span = 1
    while 2 * span < CHUNK:
        power = bmm(power, power)
        t_inv = t_inv + bmm(t_inv, power)
        span *= 2

    u_st = bmm(t_inv, u0 + bmm(a_ak, v_st))
    uv = jnp.concatenate([u_st, v_st], axis=1)
    y_st = y0 + bmm(jnp.concatenate([a_rb, a_rk], axis=2), uv)
    y = y_st[:, :n] + y_st[:, n:]

    tails = jnp.concatenate([stacked(b_vec * p_tail), stacked(k_h * p_tail)], axis=1)
    uv_t = jnp.swapaxes(uv, 1, 2)
    s_new = s_prev * jnp.exp(total) + bmm(uv_t, tails)
    st_ref[...] = s_new

    mu = head_sum(y) * (1.0 / n)
    yc = y - mu
    var = head_sum(yc * yc) * (1.0 / n)
    yn = yc * lax.rsqrt(var + RWKV_GN_EPS) * paired(gng_ref) + paired(gnb_ref)
    out = yn + head_sum(r * k_h * r_k) * v
    gate = paired(g_ref)
    for p in range(pairs):
        o_ref[:, p * w2:(p + 1) * w2] = (out[p] * gate[p]).astype(BF16)

    def store_state(dst_ref):
        for p in range(pairs):
            dst_ref[0, 2 * p] = s_new[p, :n, :n]
            dst_ref[0, 2 * p + 1] = s_new[p, n:, n:]

    @pl.when(c < ncp)
    def _():
        store_state(sp_ref)

    @pl.when(c >= ncp)
    def _():
        store_state(ss_ref)


def paired_value(x, pairs, w2):
    return jnp.stack([x[:, p * w2:(p + 1) * w2] for p in range(pairs)])


def rwkv7_attention(r, k, v, lw, a, g, k_k, k_a, r_k, gn_g, gn_b, s0, *, n_prompt, cps):
    nt, d = r.shape
    heads = d // RWKV_HEAD
    pairs = heads // 2
    n_sample = s0.shape[0]
    ncp = n_prompt * cps
    assert nt // CHUNK == ncp + n_sample
    row = pl.BlockSpec((CHUNK, d), lambda c: (c, 0))
    vec = pl.BlockSpec((1, d), lambda c: (0, 0))
    s0_spec, sp_spec, ss_spec = _seq_state_specs((heads, RWKV_HEAD, RWKV_HEAD), ncp, cps, n_prompt)
    vecs = [u.reshape(1, d) for u in (k_k, k_a, r_k, gn_g, gn_b)]
    return pl.pallas_call(
        functools.partial(_rwkv_body, pairs=pairs, ncp=ncp, cps=cps),
        grid=(nt // CHUNK,),
        in_specs=[row] * 6 + [vec] * 5 + [s0_spec],
        out_specs=[row, sp_spec, ss_spec],
        out_shape=[jax.ShapeDtypeStruct((nt, d), BF16),
                   jax.ShapeDtypeStruct((n_prompt, heads, RWKV_HEAD, RWKV_HEAD), F32),
                   jax.ShapeDtypeStruct((n_sample, heads, RWKV_HEAD, RWKV_HEAD), F32)],
        scratch_shapes=[pltpu.VMEM((pairs, 2 * RWKV_HEAD, 2 * RWKV_HEAD), F32)],
        compiler_params=_cparams(("arbitrary",)),
        name="rwkv7_attention",
    )(r, k, v, lw, a, g, *vecs, s0)


def _gate_up_body(be_ref, chg_ref, nu_ref, x_ref, w1_ref, w3_ref, h_ref, w1c_ref, w3c_ref):
    b = pl.program_id(1)

    @pl.when(chg_ref[b] == 1)
    def _():
        w1c_ref[...] = w1_ref[...].astype(BF16)
        w3c_ref[...] = w3_ref[...].astype(BF16)

    @pl.when(b < nu_ref[0])
    def _():
        x = x_ref[...]
        gate = jnp.dot(x, w1c_ref[...], preferred_element_type=F32)
        up = jnp.dot(x, w3c_ref[...], preferred_element_type=F32)
        h_ref[...] = (gate * jax.nn.sigmoid(gate) * up).astype(BF16)

    @pl.when(b >= nu_ref[0])
    def _():
        h_ref[...] = jnp.zeros(h_ref.shape, BF16)


def _down_body(be_ref, chg_ref, nu_ref, h_ref, w2_ref, y_ref, w2c_ref):
    b = pl.program_id(1)

    @pl.when(chg_ref[b] == 1)
    def _():
        w2c_ref[...] = w2_ref[...].astype(BF16)

    @pl.when(b < nu_ref[0])
    def _():
        y_ref[...] = jnp.dot(h_ref[...], w2c_ref[...], preferred_element_type=F32)

    @pl.when(b >= nu_ref[0])
    def _():
        y_ref[...] = jnp.zeros(y_ref.shape, F32)


def grouped_swiglu(xs, block_expert, n_used, w1, w3, w2, base):
    rows, d = xs.shape
    f = w1.shape[-1]
    nb = rows // MOE_BLOCK
    be = block_expert.astype(jnp.int32) + base
    changed = jnp.concatenate([jnp.ones((1,), jnp.int32),
                               (be[1:] != be[:-1]).astype(jnp.int32)])
    nu = jnp.reshape(n_used, (1,)).astype(jnp.int32)
    tf = _pick(f, (512, 256, 128))
    h = pl.pallas_call(
        _gate_up_body,
        grid_spec=pltpu.PrefetchScalarGridSpec(
            num_scalar_prefetch=3,
            grid=(f // tf, nb),
            in_specs=[pl.BlockSpec((MOE_BLOCK, d), lambda j, b, be, ch, nu: (b, 0)),
                      pl.BlockSpec((None, d, tf), lambda j, b, be, ch, nu: (be[b], 0, j)),
                      pl.BlockSpec((None, d, tf), lambda j, b, be, ch, nu: (be[b], 0, j))],
            out_specs=pl.BlockSpec((MOE_BLOCK, tf), lambda j, b, be, ch, nu: (b, j)),
            scratch_shapes=[pltpu.VMEM((d, tf), BF16), pltpu.VMEM((d, tf), BF16)]),
        out_shape=jax.ShapeDtypeStruct((rows, f), BF16),
        compiler_params=_cparams(("arbitrary", "arbitrary")),
        name="swiglu_gate_up",
    )(be, changed, nu, xs, w1, w3)
    tn = _pick(d, (256, 128))
    return pl.pallas_call(
        _down_body,
        grid_spec=pltpu.PrefetchScalarGridSpec(
            num_scalar_prefetch=3,
            grid=(d // tn, nb),
            in_specs=[pl.BlockSpec((MOE_BLOCK, f), lambda j, b, be, ch, nu: (b, 0)),
                      pl.BlockSpec((None, f, tn), lambda j, b, be, ch, nu: (be[b], 0, j))],
            out_specs=pl.BlockSpec((MOE_BLOCK, tn), lambda j, b, be, ch, nu: (b, j)),
            scratch_shapes=[pltpu.VMEM((f, tn), BF16)]),
        out_shape=jax.ShapeDtypeStruct((rows, d), F32),
        compiler_params=_cparams(("arbitrary", "arbitrary")),
        name="swiglu_down",
    )(be, changed, nu, h, w2)


def _router_body(x_ref, w_ref, o_ref):
    o_ref[...] = jnp.dot(x_ref[...], w_ref[...], precision=_HI, preferred_element_type=F32)


def router_logits(x, w_pad):
    m, d = x.shape
    n = w_pad.shape[1]
    tm = _pick(m, (512, 256, 128, 64, 32, 16, 8))
    return pl.pallas_call(
        _router_body,
        grid=(m // tm,),
        in_specs=[pl.BlockSpec((tm, d), lambda i: (i, 0)), pl.BlockSpec((d, n), lambda i: (0, 0))],
        out_specs=pl.BlockSpec((tm, n), lambda i: (i, 0)),
        out_shape=jax.ShapeDtypeStruct((m, n), F32),
        compiler_params=_cparams(("parallel",)),
        name="moe_router",
    )(x, w_pad)


def _pad_cols(w, mult=LANES):
    pad = (-w.shape[-1]) % mult
    return jnp.pad(w, [(0, 0)] * (w.ndim - 1) + [(0, pad)]) if pad else w


def _pad_rows(w, mult=LANES):
    pad = (-w.shape[-2]) % mult
    return jnp.pad(w, [(0, 0)] * (w.ndim - 2) + [(0, pad), (0, 0)]) if pad else w


def dense_swiglu(xb, w1, w3, w2, slot):
    nb = xb.shape[0] // MOE_BLOCK
    return grouped_swiglu(xb, jnp.zeros((nb,), jnp.int32), jnp.int32(nb), w1, w3, w2, slot)


def moe_swiglu(xf, xb, w_router, w1, w3, w2, slot):
    n, d = xf.shape
    n_exp = w_router.shape[-1]
    logits = router_logits(xf, _pad_cols(w_router[slot]))[:, :n_exp]
    top_val, top_idx = lax.top_k(logits, MOE_TOP_K)
    gates = jax.nn.softmax(top_val, axis=-1)
    slots = n * MOE_TOP_K
    flat_e = top_idx.reshape(-1)
    onehot = (flat_e[:, None] == jnp.arange(n_exp, dtype=flat_e.dtype)[None, :]).astype(jnp.int32)
    rank = jnp.sum((jnp.cumsum(onehot, axis=0) - onehot) * onehot, axis=1)
    counts = jnp.sum(onehot, axis=0)
    padded = (counts + MOE_BLOCK - 1) // MOE_BLOCK * MOE_BLOCK
    pend = jnp.cumsum(padded)
    pstart = pend - padded
    nb = (slots + n_exp * (MOE_BLOCK - 1) + MOE_BLOCK - 1) // MOE_BLOCK
    dest = (pstart[flat_e] + rank).astype(jnp.int32)
    flat_tok = jnp.arange(slots, dtype=jnp.int32) // MOE_TOP_K
    src_tok = jnp.zeros((nb * MOE_BLOCK,), jnp.int32).at[dest].set(flat_tok)
    block_start = jnp.arange(nb, dtype=jnp.int32) * MOE_BLOCK
    block_expert = jnp.clip(jnp.searchsorted(pend, block_start, side="right"), 0, n_exp - 1)
    n_used = pend[-1] // MOE_BLOCK
    xs = jnp.take(xb, src_tok, axis=0)
    e1 = w1.reshape((-1,) + w1.shape[2:])
    e3 = w3.reshape((-1,) + w3.shape[2:])
    e2 = w2.reshape((-1,) + w2.shape[2:])
    ys = grouped_swiglu(xs, block_expert, n_used, e1, e3, e2, slot * n_exp)
    picked = jnp.take(ys, dest, axis=0).reshape(n, MOE_TOP_K, d)
    return jnp.sum(picked * gates[:, :, None], axis=1)


def _rope_tables(dk, cps, past_len):
    half = dk // 2
    inv_freq = 1.0 / (RET_ROPE_BASE ** jnp.linspace(0.0, 1.0, half, dtype=F32))
    pos = jnp.concatenate([jnp.arange(cps * CHUNK, dtype=jnp.int32),
                           past_len + jnp.arange(CHUNK, dtype=jnp.int32)])
    ang = pos.astype(F32)[:, None] * inv_freq[None, :]
    cos = jnp.repeat(jnp.cos(ang), 2, axis=1)
    sin = jnp.stack([-jnp.sin(ang), jnp.sin(ang)], axis=-1).reshape(pos.shape[0], dk)
    return cos, sin


def kernel(x_prompt, x_sample, state_gla, state_rwkv, state_shift, state_ret, ln_g, ln_b,
           gla_wq, gla_wk, gla_wv, gla_wr, gla_wa1, gla_wa2, gla_ba, gla_norm_g, gla_wo,
           rwkv_mu, rwkv_wr, rwkv_wk, rwkv_wv, rwkv_wo, rwkv_w0, rwkv_w1, rwkv_w2, rwkv_a0, rwkv_a1,
           rwkv_a2, rwkv_g1, rwkv_g2, rwkv_k_k, rwkv_k_a, rwkv_r_k, rwkv_gn_g, rwkv_gn_b,
           ret_wq, ret_wk, ret_wv, ret_wg, ret_gn_g, ret_wo,
           ffn_w1, ffn_w3, ffn_w2, moe_router, moe_w1, moe_w3, moe_w2):
    bp, tp, d = x_prompt.shape
    bs, ts, _ = x_sample.shape
    assert ts == CHUNK and tp % CHUNK == 0
    depth = ln_g.shape[0]
    alpha = (2.0 * depth) ** 0.25
    cps = tp // CHUNK
    past_len = tp
    n_prompt_rows = bp * tp
    seq = dict(n_prompt=bp, cps=cps)

    gla_heads = state_gla.shape[2]
    gla_dk = state_gla.shape[3]
    ret_heads = state_ret.shape[2]
    ret_dk = state_ret.shape[3]

    xf = jnp.concatenate([x_prompt.reshape(-1, d), x_sample.reshape(-1, d)], axis=0)
    xb = xf.astype(BF16)

    new_gla_p, new_gla_s, new_rwkv_p, new_rwkv_s = [], [], [], []
    new_shift_p, new_shift_s, new_ret_p, new_ret_s = [], [], [], []
    for i in range(depth):
        kind, slot = i % 3, i // 3
        if kind == 0:
            q = matmul(xb, gla_wq, slot, scale=gla_dk ** -0.5)
            k = matmul(xb, gla_wk, slot)
            v = matmul(xb, gla_wv, slot)
            gate = matmul(xb, gla_wr, slot, act="silu")
            low = matmul(xb, _pad_cols(gla_wa1), slot, out_dtype=BF16)
            log_alpha = matmul(low, _pad_rows(gla_wa2), slot, act="gla_gate", bias=gla_ba[slot])
            o, sp, ss = decay_attention(q, k, v, (log_alpha,), gate, gla_norm_g[slot], state_gla[slot],
                                        mode="gla", heads=gla_heads, **seq)
            h = matmul(o, gla_wo, slot)
            new_gla_p.append(sp)
            new_gla_s.append(ss)
        elif kind == 1:
            xp3 = xf[:n_prompt_rows].reshape(bp, tp, d)
            xs3 = xf[n_prompt_rows:].reshape(bs, ts, d)
            prev_p = jnp.concatenate([jnp.zeros((bp, 1, d), F32), xp3[:, :-1]], axis=1)
            prev_s = jnp.concatenate([state_shift[slot][:, None, :], xs3[:, :-1]], axis=1)
            x_prev = jnp.concatenate([prev_p.reshape(-1, d), prev_s.reshape(-1, d)], axis=0)
            mixes = shift_mix(xf, x_prev, rwkv_mu[slot])
            xr, xw, xk, xv, xa, xg = (mixes[j] for j in range(6))
            r = matmul(xr, rwkv_wr, slot)
            k = matmul(xk, rwkv_wk, slot)
            v = matmul(xv, rwkv_wv, slot)
            w_mid = matmul(xw, _pad_cols(rwkv_w1), slot, act="tanh", out_dtype=BF16)
            log_decay = matmul(w_mid, _pad_rows(rwkv_w2), slot, act="rwkv_decay", bias=rwkv_w0[slot])
            a_mid = matmul(xa, _pad_cols(rwkv_a1), slot, out_dtype=BF16)
            a = matmul(a_mid, _pad_rows(rwkv_a2), slot, act="sigmoid", bias=rwkv_a0[slot])
            g_mid = matmul(xg, _pad_cols(rwkv_g1), slot, act="sigmoid", out_dtype=BF16)
            g = matmul(g_mid, _pad_rows(rwkv_g2), slot)
            o, sp, ss = rwkv7_attention(r, k, v, log_decay, a, g, rwkv_k_k[slot], rwkv_k_a[slot],
                                        rwkv_r_k[slot].reshape(-1), rwkv_gn_g[slot], rwkv_gn_b[slot],
                                        state_rwkv[slot], **seq)
            h = matmul(o, rwkv_wo, slot)
            new_rwkv_p.append(sp)
            new_rwkv_s.append(ss)
            new_shift_p.append(xp3[:, -1])
            new_shift_s.append(xs3[:, -1])
        else:
            q = matmul(xb, ret_wq, slot)
            k = matmul(xb, ret_wk, slot)
            v = matmul(xb, ret_wv, slot)
            gate = matmul(xb, ret_wg, slot, act="silu")
            cos, sin = _rope_tables(ret_dk, cps, past_len)
            o, sp, ss = decay_attention(q, k, v, (cos, sin), gate, ret_gn_g[slot], state_ret[slot],
                                        mode="ret", heads=ret_heads, **seq)
            h = matmul(o, ret_wo, slot)
            new_ret_p.append(sp)
            new_ret_s.append(ss)
        xf, xb = residual_layer_norm(xf, h, ln_g[i, 0], ln_b[i, 0], alpha)
        fslot = i // 2
        if i % 2 == 0:
            h = dense_swiglu(xb, ffn_w1, ffn_w3, ffn_w2, fslot)
        else:
            h = moe_swiglu(xf, xb, moe_router, moe_w1, moe_w3, moe_w2, fslot)
        xf, xb = residual_layer_norm(xf, h, ln_g[i, 1], ln_b[i, 1], alpha)

    y_prompt = xf[:n_prompt_rows].reshape(bp, tp, d)
    y_sample = xf[n_prompt_rows:].reshape(bs, ts, d)
    return (y_prompt, y_sample,
            jnp.stack(new_gla_p), jnp.stack(new_rwkv_p), jnp.stack(new_shift_p), jnp.stack(new_ret_p),
            jnp.stack(new_gla_s), jnp.stack(new_rwkv_s), jnp.stack(new_shift_s), jnp.stack(new_ret_s))
```

```python
import functools
import math

import jax
import jax.numpy as jnp
from jax import lax
from jax.experimental import pallas as pl
from jax.experimental.pallas import tpu as pltpu

F32 = jnp.float32
BF16 = jnp.bfloat16

CHUNK = 64
LANES = 128
VMEM_LIMIT_BYTES = 56 * 1024 * 1024

LN_EPS = 1e-5
GLA_TAU = 16.0
GLA_NORM_EPS = 1e-5
RWKV_HEAD = 64
RWKV_GN_EPS = 64e-5
RET_ROPE_BASE = 10000.0
RET_GN_EPS = 1e-5
MOE_TOP_K = 2
MOE_BLOCK = 512

_HI = lax.Precision.HIGHEST


def _cparams(sem):
    return pltpu.CompilerParams(dimension_semantics=sem, vmem_limit_bytes=VMEM_LIMIT_BYTES)


def _bdot(a, b):
    return jnp.dot(a.astype(BF16), b.astype(BF16), preferred_element_type=F32)


def _bdot_nt(a, b):
    return lax.dot_general(a.astype(BF16), b.astype(BF16), (((1,), (1,)), ((), ())),
                           preferred_element_type=F32)


def _log_sigmoid(z):
    return -(jnp.maximum(-z, 0.0) + jnp.log1p(jnp.exp(-jnp.abs(z))))


def _act(name, z):
    if name == "none":
        return z
    if name == "silu":
        return z * jax.nn.sigmoid(z)
    if name == "sigmoid":
        return jax.nn.sigmoid(z)
    if name == "tanh":
        return jnp.tanh(z)
    if name == "gla_gate":
        return _log_sigmoid(z) / GLA_TAU
    if name == "rwkv_decay":
        return -jnp.exp(_log_sigmoid(z) - 0.5)
    raise ValueError(name)


def _mm_body(*refs, act, has_bias, scale):
    if has_bias:
        x_ref, w_ref, b_ref, o_ref, wc_ref = refs
    else:
        x_ref, w_ref, o_ref, wc_ref = refs
        b_ref = None

    @pl.when(pl.program_id(1) == 0)
    def _():
        wc_ref[...] = w_ref[...].astype(BF16)

    acc = jnp.dot(x_ref[...], wc_ref[...], preferred_element_type=F32)
    if scale != 1.0:
        acc = acc * scale
    if has_bias:
        acc = acc + b_ref[...]
    o_ref[...] = _act(act, acc).astype(o_ref.dtype)


def _pick(n, pref):
    for t in pref:
        if n % t == 0:
            return t
    return n


MM_WEIGHT_TILE_ELEMS = 2 * 1024 * 1024


def matmul(x, w, slot=0, *, x_slot=None, act="none", bias=None, scale=1.0, out_dtype=F32):
    m, kdim = x.shape[-2:]
    _, kw, n = w.shape
    assert kw == kdim, (w.shape, x.shape)
    tm = _pick(m, (1024, 512, 256, 128, 64, 32, 16, 8))
    tn = _pick(n, tuple(t for t in (1024, 512, 256, 128) if t * kdim <= MM_WEIGHT_TILE_ELEMS))
    if x_slot is None:
        x_spec = pl.BlockSpec((tm, kdim), lambda j, i: (i, 0))
    else:
        x_spec = pl.BlockSpec((None, tm, kdim), lambda j, i: (x_slot, i, 0))
    in_specs = [x_spec, pl.BlockSpec((None, kdim, tn), lambda j, i: (slot, 0, j))]
    args = [x, w]
    if bias is not None:
        in_specs.append(pl.BlockSpec((1, tn), lambda j, i: (0, j)))
        args.append(bias.reshape(1, n).astype(F32))
    return pl.pallas_call(
        functools.partial(_mm_body, act=act, has_bias=bias is not None, scale=scale),
        grid=(n // tn, m // tm),
        in_specs=in_specs,
        out_specs=pl.BlockSpec((tm, tn), lambda j, i: (i, j)),
        out_shape=jax.ShapeDtypeStruct((m, n), out_dtype),
        scratch_shapes=[pltpu.VMEM((kdim, tn), BF16)],
        compiler_params=_cparams(("arbitrary", "arbitrary")),
        name="matmul_" + act,
    )(*args)


def _ln_body(*refs, alpha, gated, head_tiles):
    if gated:
        x_ref, h0_ref, h1_ref, gate_ref, g_ref, b_ref, oa_ref, ob_ref = refs
        h = h0_ref[...] * gate_ref[:, 0:1] + h1_ref[...] * gate_ref[:, 1:2]
    else:
        x_ref, h_ref, g_ref, b_ref, oa_ref, ob_ref = refs
        h = h_ref[...]
    z = alpha * x_ref[...] + h
    mu = jnp.mean(z, -1, keepdims=True)
    zc = z - mu
    var = jnp.mean(zc * zc, -1, keepdims=True)
    y = zc * lax.rsqrt(var + LN_EPS) * g_ref[...] + b_ref[...]
    if head_tiles is None:
        oa_ref[...] = y
        ob_ref[...] = y.astype(BF16)
    else:
        i = pl.program_id(0)

        @pl.when(i < head_tiles)
        def _():
            oa_ref[...] = y

        @pl.when(i >= head_tiles)
        def _():
            ob_ref[...] = y


def residual_layer_norm(x, h, g, b, alpha, gates=None, split_rows=None):
    m, d = x.shape
    tm = _pick(m, (512, 256, 128, 64, 32, 16, 8))
    row = pl.BlockSpec((tm, d), lambda i: (i, 0))
    vec = pl.BlockSpec((1, d), lambda i: (0, 0))
    if gates is None:
        h_specs, h_args = [row], [h]
    else:
        h_specs = [row, row, pl.BlockSpec((tm, gates.shape[1]), lambda i: (i, 0))]
        h_args = [h[0], h[1], gates]
    if split_rows is None:
        head_tiles = None
        out_specs = [row, row]
        out_shape = [jax.ShapeDtypeStruct((m, d), F32), jax.ShapeDtypeStruct((m, d), BF16)]
    else:
        assert split_rows % tm == 0 and 0 < split_rows < m
        head_tiles = split_rows // tm
        out_specs = [pl.BlockSpec((tm, d), lambda i: (jnp.minimum(i, head_tiles - 1), 0)),
                     pl.BlockSpec((tm, d), lambda i: (jnp.maximum(i - head_tiles, 0), 0))]
        out_shape = [jax.ShapeDtypeStruct((split_rows, d), F32),
                     jax.ShapeDtypeStruct((m - split_rows, d), F32)]
    return pl.pallas_call(
        functools.partial(_ln_body, alpha=alpha, gated=gates is not None, head_tiles=head_tiles),
        grid=(m // tm,),
        in_specs=[row] + h_specs + [vec, vec],
        out_specs=out_specs,
        out_shape=out_shape,
        compiler_params=_cparams(("parallel",) if split_rows is None else ("arbitrary",)),
        name="residual_layer_norm",
    )(x, *h_args, g.reshape(1, d), b.reshape(1, d))


SUBLANES = 8


def _shift_mix_body(x_ref, halo_ref, shift_ref, mu_ref, o_ref, *, tm, tp, ts, prompt_tiles):
    i = pl.program_id(0)
    x = x_ref[...]
    row = lax.broadcasted_iota(jnp.int32, (tm, 1), 0)
    prev = jnp.where(row == 0, halo_ref[SUBLANES - 1:SUBLANES, :], pltpu.roll(x, 1, 0))
    prompt_start = jnp.logical_and(row == 0, (i * tm) % tp == 0)
    prev_prompt = jnp.where(prompt_start, 0.0, prev)
    nseq = tm // ts
    carried = jnp.broadcast_to(shift_ref[...][:, None, :], (nseq, ts, x.shape[1])).reshape(x.shape)
    prev_sample = jnp.where(row % ts == 0, carried, prev)
    xx = jnp.where(i < prompt_tiles, prev_prompt, prev_sample) - x
    for j in range(o_ref.shape[0]):
        o_ref[j] = (x + xx * mu_ref[j:j + 1, :]).astype(BF16)


def shift_mix(x, shift0, mu, *, n_prompt_rows, tp, ts):
    m, d = x.shape
    nmix = mu.shape[0]
    tm = 512
    assert tp % tm == 0 and tm % ts == 0 and n_prompt_rows % tm == 0 and m % tm == 0
    prompt_tiles = n_prompt_rows // tm
    nseq = tm // ts
    assert shift0.shape[0] % nseq == 0
    halo_blocks = tm // SUBLANES
    return pl.pallas_call(
        functools.partial(_shift_mix_body, tm=tm, tp=tp, ts=ts, prompt_tiles=prompt_tiles),
        grid=(m // tm,),
        in_specs=[pl.BlockSpec((tm, d), lambda i: (i, 0)),
                  pl.BlockSpec((SUBLANES, d), lambda i: (jnp.maximum(i * halo_blocks - 1, 0), 0)),
                  pl.BlockSpec((nseq, d), lambda i: (jnp.maximum(i - prompt_tiles, 0), 0)),
                  pl.BlockSpec((nmix, d), lambda i: (0, 0))],
        out_specs=pl.BlockSpec((nmix, tm, d), lambda i: (0, i, 0)),
        out_shape=jax.ShapeDtypeStruct((nmix, m, d), BF16),
        compiler_params=_cparams(("parallel",)),
        name="rwkv_shift_mix",
    )(x, x, shift0, mu)


def _chunk_state_init(c, ncp, cps, state_ref, s0_ref):
    @pl.when(jnp.logical_and(c < ncp, c % cps == 0))
    def _():
        state_ref[...] = jnp.zeros(state_ref.shape, state_ref.dtype)

    @pl.when(c >= ncp)
    def _():
        state_ref[...] = s0_ref[0]


def _dla_body(*refs, heads, dk, dv, mode, ncp, cps):
    if mode == "gla":
        q_ref, k_ref, v_ref, g_ref, gate_ref, ng_ref, s0_ref, o_ref, sp_ref, ss_ref, st_ref = refs
    else:
        (q_ref, k_ref, v_ref, cos_ref, sin_ref, gate_ref, ng_ref, s0_ref,
         o_ref, sp_ref, ss_ref, st_ref) = refs
    c = pl.program_id(0)
    _chunk_state_init(c, ncp, cps, st_ref, s0_ref)

    t_row = lax.broadcasted_iota(jnp.int32, (CHUNK, CHUNK), 0)
    t_col = lax.broadcasted_iota(jnp.int32, (CHUNK, CHUNK), 1)
    causal = t_row >= t_col
    if mode == "gla":
        lower_ones = causal.astype(F32)
        ones_cols = jnp.ones((CHUNK, LANES), F32)
    else:
        width = heads * dk
        even = (lax.broadcasted_iota(jnp.int32, (CHUNK, width), 1) % 2) == 0
        q_all = q_ref[...]
        k_all = k_ref[...]
        q_sw = jnp.where(even, pltpu.roll(q_all, width - 1, 1), pltpu.roll(q_all, 1, 1))
        k_sw = jnp.where(even, pltpu.roll(k_all, width - 1, 1), pltpu.roll(k_all, 1, 1))
        cos = cos_ref[...]
        sin = sin_ref[...]
        frame = (lax.broadcasted_iota(jnp.int32, (CHUNK, 1), 0) + 1).astype(F32)

    for h in range(heads):
        ks = slice(h * dk, (h + 1) * dk)
        vs = slice(h * dv, (h + 1) * dv)
        v = v_ref[:, vs]
        s_prev = st_ref[h]
        if mode == "gla":
            q = q_ref[:, ks]
            k = k_ref[:, ks]
            g = g_ref[:, ks]
            cum = jnp.dot(lower_ones, g, precision=_HI, preferred_element_type=F32)
            total = cum[CHUNK - 1:CHUNK, :]
            total_col = lax.dot_general(g, ones_cols, (((0,), (0,)), ((), ())), precision=_HI,
                                        preferred_element_type=F32)[:, :1]
            state_decay = jnp.exp(total_col)
            q_dec = q * jnp.exp(cum)
            k_inv = k * jnp.exp(-cum)
            k_tail = k * jnp.exp(total - cum)
        else:
            q = q_all[:, ks] * cos + q_sw[:, ks] * sin
            k = (k_all[:, ks] * cos + k_sw[:, ks] * sin) * (dk ** -0.5)
            log_gamma = math.log1p(-(2.0 ** (-5.0 - h)))
            cum = frame * log_gamma
            total = CHUNK * log_gamma
            state_decay = math.exp(total)
            q_dec = q * jnp.exp(cum)
            k_inv = k * jnp.exp(-cum)
            k_tail = k * jnp.exp(total - cum)
        scores = jnp.where(causal, _bdot_nt(q_dec, k_inv), 0.0)
        o = _bdot(q_dec, s_prev) + _bdot(scores, v)
        k_tail_t = jnp.transpose(k_tail)
        st_ref[h] = state_decay * s_prev + _bdot(k_tail_t, v)
        if mode == "gla":
            o = o * lax.rsqrt(jnp.mean(o * o, -1, keepdims=True) + GLA_NORM_EPS) * ng_ref[...]
        else:
            mu = jnp.mean(o, -1, keepdims=True)
            oc = o - mu
            var = jnp.mean(oc * oc, -1, keepdims=True)
            o = oc * lax.rsqrt(var + RET_GN_EPS) * ng_ref[:, vs]
        if mode == "gla":
            o_ref[:, vs] = (o * gate_ref[:, vs]).astype(BF16)
        else:
            o_ref[:, vs] = (gate_ref[:, vs] * o).astype(BF16)

    @pl.when(jnp.logical_and(c < ncp, c % cps == cps - 1))
    def _():
        sp_ref[0] = st_ref[...]

    @pl.when(c >= ncp)
    def _():
        ss_ref[0] = st_ref[...]


def _seq_state_specs(state_shape, ncp, cps, n_prompt, slot):
    blk = (1,) + tuple(state_shape)
    zeros = (0,) * len(state_shape)
    s0_spec = pl.BlockSpec((None,) + blk, lambda c: (slot, jnp.maximum(c - ncp, 0)) + zeros)
    sp_spec = pl.BlockSpec(blk, lambda c: (jnp.minimum(c // cps, n_prompt - 1),) + zeros)
    ss_spec = pl.BlockSpec(blk, lambda c: (jnp.maximum(c - ncp, 0),) + zeros)
    return s0_spec, sp_spec, ss_spec


def decay_attention(q, k, v, extra, gate, norm_g, s0, slot, *, mode, heads, n_prompt, cps):
    nt = q.shape[0]
    dk = q.shape[1] // heads
    dv = v.shape[1] // heads
    n_sample = s0.shape[1]
    ncp = n_prompt * cps
    nchunks = nt // CHUNK
    assert nchunks == ncp + n_sample
    rowk = pl.BlockSpec((CHUNK, heads * dk), lambda c: (c, 0))
    rowv = pl.BlockSpec((CHUNK, heads * dv), lambda c: (c, 0))
    s0_spec, sp_spec, ss_spec = _seq_state_specs((heads, dk, dv), ncp, cps, n_prompt, slot)
    if mode == "gla":
        extra_specs = [rowk]
        ng_spec = pl.BlockSpec((1, dv), lambda c: (0, 0))
        norm_g = norm_g.reshape(1, dv)
    else:
        pos_spec = pl.BlockSpec((CHUNK, dk), lambda c: (jnp.where(c < ncp, c % cps, cps), 0))
        extra_specs = [pos_spec, pos_spec]
        ng_spec = pl.BlockSpec((1, heads * dv), lambda c: (0, 0))
        norm_g = norm_g.reshape(1, heads * dv)
    return pl.pallas_call(
        functools.partial(_dla_body, heads=heads, dk=dk, dv=dv, mode=mode, ncp=ncp, cps=cps),
        grid=(nchunks,),
        in_specs=[rowk, rowk, rowv] + extra_specs + [rowv, ng_spec, s0_spec],
        out_specs=[rowv, sp_spec, ss_spec],
        out_shape=[jax.ShapeDtypeStruct((nt, heads * dv), BF16),
                   jax.ShapeDtypeStruct((n_prompt, heads, dk, dv), F32),
                   jax.ShapeDtypeStruct((n_sample, heads, dk, dv), F32)],
        scratch_shapes=[pltpu.VMEM((heads, dk, dv), F32)],
        compiler_params=_cparams(("arbitrary",)),
        name="decay_attention_" + mode,
    )(q, k, v, *extra, gate, norm_g, s0)


def _rwkv_body(r_ref, k_ref, v_ref, lw_ref, a_ref, g_ref, kk_ref, ka_ref, rk_ref, gng_ref, gnb_ref,
               s0_ref, o_ref, sp_ref, ss_ref, st_ref, *, pairs, ncp, cps):
    c = pl.program_id(0)
    n = RWKV_HEAD
    w2 = 2 * n

    @pl.when(jnp.logical_and(c < ncp, c % cps == 0))
    def _():
        st_ref[...] = jnp.zeros(st_ref.shape, F32)

    @pl.when(c >= ncp)
    def _():
        zero = jnp.zeros((n, n), F32)
        for p in range(pairs):
            top = jnp.concatenate([s0_ref[0, 2 * p], zero], axis=1)
            bot = jnp.concatenate([zero, s0_ref[0, 2 * p + 1]], axis=1)
            st_ref[p] = jnp.concatenate([top, bot], axis=0)

    def paired(ref):
        x = ref[...]
        return jnp.stack([x[:, p * w2:(p + 1) * w2] for p in range(pairs)])

    lane = lax.broadcasted_iota(jnp.int32, (1, 1, w2), 2)
    first = lane < n

    def head_sum(x):
        s_a = jnp.sum(jnp.where(first, x, 0.0), -1, keepdims=True)
        s_b = jnp.sum(jnp.where(first, 0.0, x), -1, keepdims=True)
        return jnp.where(first, s_a, s_b)

    def stacked(x):
        return jnp.concatenate([jnp.where(first, x, 0.0), jnp.where(first, 0.0, x)], axis=1)

    def bmm(a, b):
        return lax.dot_general(a.astype(BF16), b.astype(BF16), (((2,), (1,)), ((0,), (0,))),
                               preferred_element_type=F32)

    def bmm_nt(a, b):
        return lax.dot_general(a.astype(BF16), b.astype(BF16), (((2,), (2,)), ((0,), (0,))),
                               preferred_element_type=F32)

    r = paired(r_ref)
    k_raw = paired(k_ref)
    v = paired(v_ref)
    lw = paired(lw_ref)
    a = paired(a_ref)
    k_k = paired(kk_ref)
    k_a = paired(ka_ref)
    r_k = paired(rk_ref)

    kk = k_raw * k_k
    kk = kk / jnp.maximum(jnp.sqrt(head_sum(kk * kk)), 1e-12)
    k_h = k_raw * (1.0 + (a - 1.0) * k_a)
    a_vec = -kk
    b_vec = kk * a

    t_row = lax.broadcasted_iota(jnp.int32, (CHUNK, CHUNK), 0)
    t_col = lax.broadcasted_iota(jnp.int32, (CHUNK, CHUNK), 1)
    lower_ones = (t_row >= t_col).astype(F32)
    cum = paired_value(jnp.dot(lower_ones, lw_ref[...], precision=_HI, preferred_element_type=F32),
                       pairs, w2)
    total = cum[:, CHUNK - 1:CHUNK, :]
    p_now = jnp.exp(cum)
    p_prev = jnp.exp(cum - lw)
    p_inv = jnp.exp(-cum)
    p_tail = jnp.exp(total - cum)

    lhs = jnp.concatenate([stacked(a_vec * p_prev), stacked(r * p_now)], axis=1)
    rhs = jnp.concatenate([stacked(b_vec * p_inv), stacked(k_h * p_inv)], axis=1)
    sc = bmm_nt(lhs, rhs)
    s_prev = st_ref[...]
    sr = bmm_nt(lhs, s_prev)

    i_row = lax.broadcasted_iota(jnp.int32, (1, w2, w2), 1)
    i_col = lax.broadcasted_iota(jnp.int32, (1, w2, w2), 2)
    same = (i_row // n) == (i_col // n)
    strict = jnp.logical_and(same, (i_col % n) < (i_row % n))
    incl = jnp.logical_and(same, (i_col % n) <= (i_row % n))
    a_ab = jnp.where(strict, sc[:, :w2, :w2], 0.0)
    a_ak = jnp.where(strict, sc[:, :w2, w2:], 0.0)
    a_rb = jnp.where(incl, sc[:, w2:, :w2], 0.0)
    a_rk = jnp.where(incl, sc[:, w2:, w2:], 0.0)
    u0 = sr[:, :w2]
    y0 = sr[:, w2:]
    v_st = stacked(v)

    eye = (i_row == i_col).astype(F32)
    t_inv = eye + a_ab
    power = a_ab
    span = 1
    while 2 * span < CHUNK:
        power = bmm(power, power)
        t_inv = t_inv + bmm(t_inv, power)
        span *= 2

    u_st = bmm(t_inv, u0 + bmm(a_ak, v_st))
    uv = jnp.concatenate([u_st, v_st], axis=1)
    y_st = y0 + bmm(jnp.concatenate([a_rb, a_rk], axis=2), uv)
    y = y_st[:, :n] + y_st[:, n:]

    tails = jnp.concatenate([stacked(b_vec * p_tail), stacked(k_h * p_tail)], axis=1)
    uv_t = jnp.swapaxes(uv, 1, 2)
    s_new = s_prev * jnp.exp(total) + bmm(uv_t, tails)
    st_ref[...] = s_new

    mu = head_sum(y) * (1.0 / n)
    yc = y - mu
    var = head_sum(yc * yc) * (1.0 / n)
    yn = yc * lax.rsqrt(var + RWKV_GN_EPS) * paired(gng_ref) + paired(gnb_ref)
    out = yn + head_sum(r * k_h * r_k) * v
    gate = paired(g_ref)
    for p in range(pairs):
        o_ref[:, p * w2:(p + 1) * w2] = (out[p] * gate[p]).astype(BF16)

    def store_state(dst_ref):
        for p in range(pairs):
            dst_ref[0, 2 * p] = s_new[p, :n, :n]
            dst_ref[0, 2 * p + 1] = s_new[p, n:, n:]

    @pl.when(jnp.logical_and(c < ncp, c % cps == cps - 1))
    def _():
        store_state(sp_ref)

    @pl.when(c >= ncp)
    def _():
        store_state(ss_ref)


def paired_value(x, pairs, w2):
    return jnp.stack([x[:, p * w2:(p + 1) * w2] for p in range(pairs)])


def rwkv7_attention(r, k, v, lw, a, g, k_k, k_a, r_k, gn_g, gn_b, s0, slot, *, n_prompt, cps):
    nt, d = r.shape
    heads = d // RWKV_HEAD
    pairs = heads // 2
    n_sample = s0.shape[1]
    ncp = n_prompt * cps
    assert nt // CHUNK == ncp + n_sample
    row = pl.BlockSpec((CHUNK, d), lambda c: (c, 0))
    vec = pl.BlockSpec((1, d), lambda c: (0, 0))
    s0_spec, sp_spec, ss_spec = _seq_state_specs((heads, RWKV_HEAD, RWKV_HEAD), ncp, cps, n_prompt,
                                                 slot)
    vecs = [u.reshape(1, d) for u in (k_k, k_a, r_k, gn_g, gn_b)]
    return pl.pallas_call(
        functools.partial(_rwkv_body, pairs=pairs, ncp=ncp, cps=cps),
        grid=(nt // CHUNK,),
        in_specs=[row] * 6 + [vec] * 5 + [s0_spec],
        out_specs=[row, sp_spec, ss_spec],
        out_shape=[jax.ShapeDtypeStruct((nt, d), BF16),
                   jax.ShapeDtypeStruct((n_prompt, heads, RWKV_HEAD, RWKV_HEAD), F32),
                   jax.ShapeDtypeStruct((n_sample, heads, RWKV_HEAD, RWKV_HEAD), F32)],
        scratch_shapes=[pltpu.VMEM((pairs, 2 * RWKV_HEAD, 2 * RWKV_HEAD), F32)],
        compiler_params=_cparams(("arbitrary",)),
        name="rwkv7_attention",
    )(r, k, v, lw, a, g, *vecs, s0)


def _gate_up_body(be_ref, chg_ref, nu_ref, x_ref, w1_ref, w3_ref, h_ref, w1c_ref, w3c_ref):
    b = pl.program_id(1)

    @pl.when(chg_ref[b] == 1)
    def _():
        w1c_ref[...] = w1_ref[...].astype(BF16)
        w3c_ref[...] = w3_ref[...].astype(BF16)

    @pl.when(b < nu_ref[0])
    def _():
        x = x_ref[...]
        gate = jnp.dot(x, w1c_ref[...], preferred_element_type=F32)
        up = jnp.dot(x, w3c_ref[...], preferred_element_type=F32)
        h_ref[...] = (gate * jax.nn.sigmoid(gate) * up).astype(BF16)

    @pl.when(b >= nu_ref[0])
    def _():
        h_ref[...] = jnp.zeros(h_ref.shape, BF16)


def _down_body(be_ref, chg_ref, nu_ref, h_ref, w2_ref, y_ref, w2c_ref):
    b = pl.program_id(1)

    @pl.when(chg_ref[b] == 1)
    def _():
        w2c_ref[...] = w2_ref[...].astype(BF16)

    @pl.when(b < nu_ref[0])
    def _():
        y_ref[...] = jnp.dot(h_ref[...], w2c_ref[...], preferred_element_type=F32)

    @pl.when(b >= nu_ref[0])
    def _():
        y_ref[...] = jnp.zeros(y_ref.shape, F32)


def _block_meta(block_expert, n_used):
    be = block_expert.astype(jnp.int32)
    changed = jnp.concatenate([jnp.ones((1,), jnp.int32),
                               (be[1:] != be[:-1]).astype(jnp.int32)])
    return be, changed, jnp.reshape(n_used, (1,)).astype(jnp.int32)


def swiglu_gate_up(xs, block_expert, n_used, w1, w3, block):
    rows, d = xs.shape
    f = w1.shape[-1]
    tf = _pick(f, (512, 256, 128))
    return pl.pallas_call(
        _gate_up_body,
        grid_spec=pltpu.PrefetchScalarGridSpec(
            num_scalar_prefetch=3,
            grid=(f // tf, rows // block),
            in_specs=[pl.BlockSpec((block, d), lambda j, b, be, ch, nu: (b, 0)),
                      pl.BlockSpec((None, d, tf), lambda j, b, be, ch, nu: (be[b], 0, j)),
                      pl.BlockSpec((None, d, tf), lambda j, b, be, ch, nu: (be[b], 0, j))],
            out_specs=pl.BlockSpec((block, tf), lambda j, b, be, ch, nu: (b, j)),
            scratch_shapes=[pltpu.VMEM((d, tf), BF16), pltpu.VMEM((d, tf), BF16)]),
        out_shape=jax.ShapeDtypeStruct((rows, f), BF16),
        compiler_params=_cparams(("arbitrary", "arbitrary")),
        name="swiglu_gate_up",
    )(*_block_meta(block_expert, n_used), xs, w1, w3)


def swiglu_down(h, block_expert, n_used, w2, block):
    rows, f = h.shape
    d = w2.shape[-1]
    tn = _pick(d, (512, 256, 128))
    w2_spec = pl.BlockSpec((None, f, tn), lambda j, b, be, ch, nu: (be[b], 0, j),
                           pipeline_mode=pl.Buffered(1))
    return pl.pallas_call(
        _down_body,
        grid_spec=pltpu.PrefetchScalarGridSpec(
            num_scalar_prefetch=3,
            grid=(d // tn, rows // block),
            in_specs=[pl.BlockSpec((block, f), lambda j, b, be, ch, nu: (b, 0)), w2_spec],
            out_specs=pl.BlockSpec((block, tn), lambda j, b, be, ch, nu: (b, j)),
            scratch_shapes=[pltpu.VMEM((f, tn), BF16)]),
        out_shape=jax.ShapeDtypeStruct((rows, d), F32),
        compiler_params=_cparams(("arbitrary", "arbitrary")),
        name="swiglu_down",
    )(*_block_meta(block_expert, n_used), h, w2)


def _router_body(x_ref, w_ref, o_ref):
    o_ref[...] = jnp.dot(x_ref[...], w_ref[...], precision=_HI, preferred_element_type=F32)


def router_logits(x, w_pad):
    m, d = x.shape
    n = w_pad.shape[1]
    tm = _pick(m, (512, 256, 128, 64, 32, 16, 8))
    return pl.pallas_call(
        _router_body,
        grid=(m // tm,),
        in_specs=[pl.BlockSpec((tm, d), lambda i: (i, 0)), pl.BlockSpec((d, n), lambda i: (0, 0))],
        out_specs=pl.BlockSpec((tm, n), lambda i: (i, 0)),
        out_shape=jax.ShapeDtypeStruct((m, n), F32),
        compiler_params=_cparams(("parallel",)),
        name="moe_router",
    )(x, w_pad)


def _pad_cols(w, mult=LANES):
    pad = (-w.shape[-1]) % mult
    return jnp.pad(w, [(0, 0)] * (w.ndim - 1) + [(0, pad)]) if pad else w


def _pad_rows(w, mult=LANES):
    pad = (-w.shape[-2]) % mult
    return jnp.pad(w, [(0, 0)] * (w.ndim - 2) + [(0, pad), (0, 0)]) if pad else w


DENSE_GATE_UP_BLOCK = 1024
DENSE_DOWN_BLOCK = 512


def dense_swiglu(xb, w1, w3, w2, slot):
    rows = xb.shape[0]
    assert rows % DENSE_GATE_UP_BLOCK == 0 and rows % DENSE_DOWN_BLOCK == 0
    nb_a = rows // DENSE_GATE_UP_BLOCK
    nb_b = rows // DENSE_DOWN_BLOCK
    h = swiglu_gate_up(xb, jnp.full((nb_a,), slot, jnp.int32), jnp.int32(nb_a), w1, w3,
                       DENSE_GATE_UP_BLOCK)
    return swiglu_down(h, jnp.full((nb_b,), slot, jnp.int32), jnp.int32(nb_b), w2, DENSE_DOWN_BLOCK)


def moe_swiglu(xf, xb, w_router, w1, w3, w2, slot):
    n, d = xf.shape
    n_exp = w_router.shape[-1]
    logits = router_logits(xf, _pad_cols(w_router[slot]))[:, :n_exp]
    top_val, top_idx = lax.top_k(logits, MOE_TOP_K)
    gates = jax.nn.softmax(top_val, axis=-1)
    slots = n * MOE_TOP_K
    flat_e = top_idx.reshape(-1)
    onehot = (flat_e[:, None] == jnp.arange(n_exp, dtype=flat_e.dtype)[None, :]).astype(jnp.int32)
    rank = jnp.sum((jnp.cumsum(onehot, axis=0) - onehot) * onehot, axis=1)
    counts = jnp.sum(onehot, axis=0)
    padded = (counts + MOE_BLOCK - 1) // MOE_BLOCK * MOE_BLOCK
    pend = jnp.cumsum(padded)
    pstart = pend - padded
    nb = (slots + n_exp * (MOE_BLOCK - 1) + MOE_BLOCK - 1) // MOE_BLOCK
    dest = (pstart[flat_e] + rank).astype(jnp.int32)
    flat_tok = jnp.arange(slots, dtype=jnp.int32) // MOE_TOP_K
    src_tok = jnp.zeros((nb * MOE_BLOCK,), jnp.int32).at[dest].set(flat_tok)
    block_start = jnp.arange(nb, dtype=jnp.int32) * MOE_BLOCK
    block_expert = jnp.clip(jnp.searchsorted(pend, block_start, side="right"), 0, n_exp - 1)
    block_expert = block_expert.astype(jnp.int32) + slot * n_exp
    n_used = pend[-1] // MOE_BLOCK
    xs = jnp.take(xb, src_tok, axis=0, mode="clip")
    e1 = w1.reshape((-1,) + w1.shape[2:])
    e3 = w3.reshape((-1,) + w3.shape[2:])
    e2 = w2.reshape((-1,) + w2.shape[2:])
    hs = swiglu_gate_up(xs, block_expert, n_used, e1, e3, MOE_BLOCK)
    ys = swiglu_down(hs, block_expert, n_used, e2, MOE_BLOCK)
    dest2 = dest.reshape(n, MOE_TOP_K)
    picked = [jnp.take(ys, dest2[:, j], axis=0, mode="clip") for j in range(MOE_TOP_K)]
    return picked, gates


def _rope_tables(dk, cps, past_len):
    half = dk // 2
    inv_freq = 1.0 / (RET_ROPE_BASE ** jnp.linspace(0.0, 1.0, half, dtype=F32))
    pos = jnp.concatenate([jnp.arange(cps * CHUNK, dtype=jnp.int32),
                           past_len + jnp.arange(CHUNK, dtype=jnp.int32)])
    ang = pos.astype(F32)[:, None] * inv_freq[None, :]
    cos = jnp.repeat(jnp.cos(ang), 2, axis=1)
    sin = jnp.stack([-jnp.sin(ang), jnp.sin(ang)], axis=-1).reshape(pos.shape[0], dk)
    return cos, sin


def kernel(x_prompt, x_sample, state_gla, state_rwkv, state_shift, state_ret, ln_g, ln_b,
           gla_wq, gla_wk, gla_wv, gla_wr, gla_wa1, gla_wa2, gla_ba, gla_norm_g, gla_wo,
           rwkv_mu, rwkv_wr, rwkv_wk, rwkv_wv, rwkv_wo, rwkv_w0, rwkv_w1, rwkv_w2, rwkv_a0, rwkv_a1,
           rwkv_a2, rwkv_g1, rwkv_g2, rwkv_k_k, rwkv_k_a, rwkv_r_k, rwkv_gn_g, rwkv_gn_b,
           ret_wq, ret_wk, ret_wv, ret_wg, ret_gn_g, ret_wo,
           ffn_w1, ffn_w3, ffn_w2, moe_router, moe_w1, moe_w3, moe_w2):
    bp, tp, d = x_prompt.shape
    bs, ts, _ = x_sample.shape
    assert ts == CHUNK and tp % CHUNK == 0
    depth = ln_g.shape[0]
    alpha = (2.0 * depth) ** 0.25
    cps = tp // CHUNK
    past_len = tp
    n_prompt_rows = bp * tp
    seq = dict(n_prompt=bp, cps=cps)

    gla_heads = state_gla.shape[2]
    gla_dk = state_gla.shape[3]
    ret_heads = state_ret.shape[2]
    ret_dk = state_ret.shape[3]

    xf = jnp.concatenate([x_prompt.reshape(-1, d), x_sample.reshape(-1, d)], axis=0)
    xb = xf.astype(BF16)

    new_gla_p, new_gla_s, new_rwkv_p, new_rwkv_s = [], [], [], []
    new_shift_p, new_shift_s, new_ret_p, new_ret_s = [], [], [], []
    for i in range(depth):
        kind, slot = i % 3, i // 3
        if kind == 0:
            q = matmul(xb, gla_wq, slot, scale=gla_dk ** -0.5)
            k = matmul(xb, gla_wk, slot)
            v = matmul(xb, gla_wv, slot)
            gate = matmul(xb, gla_wr, slot, act="silu")
            low = matmul(xb, _pad_cols(gla_wa1), slot, out_dtype=BF16)
            log_alpha = matmul(low, _pad_rows(gla_wa2), slot, act="gla_gate", bias=gla_ba[slot])
            o, sp, ss = decay_attention(q, k, v, (log_alpha,), gate, gla_norm_g[slot], state_gla, slot,
                                        mode="gla", heads=gla_heads, **seq)
            h = matmul(o, gla_wo, slot)
            new_gla_p.append(sp)
            new_gla_s.append(ss)
        elif kind == 1:
            mixes = shift_mix(xf, state_shift[slot], rwkv_mu[slot], n_prompt_rows=n_prompt_rows,
                              tp=tp, ts=ts)
            i_r, i_w, i_k, i_v, i_a, i_g = range(6)
            r = matmul(mixes, rwkv_wr, slot, x_slot=i_r)
            k = matmul(mixes, rwkv_wk, slot, x_slot=i_k)
            v = matmul(mixes, rwkv_wv, slot, x_slot=i_v)
            w_mid = matmul(mixes, _pad_cols(rwkv_w1), slot, x_slot=i_w, act="tanh", out_dtype=BF16)
            log_decay = matmul(w_mid, _pad_rows(rwkv_w2), slot, act="rwkv_decay", bias=rwkv_w0[slot])
            a_mid = matmul(mixes, _pad_cols(rwkv_a1), slot, x_slot=i_a, out_dtype=BF16)
            a = matmul(a_mid, _pad_rows(rwkv_a2), slot, act="sigmoid", bias=rwkv_a0[slot])
            g_mid = matmul(mixes, _pad_cols(rwkv_g1), slot, x_slot=i_g, act="sigmoid", out_dtype=BF16)
            g = matmul(g_mid, _pad_rows(rwkv_g2), slot)
            o, sp, ss = rwkv7_attention(r, k, v, log_decay, a, g, rwkv_k_k[slot], rwkv_k_a[slot],
                                        rwkv_r_k[slot].reshape(-1), rwkv_gn_g[slot], rwkv_gn_b[slot],
                                        state_rwkv, slot, **seq)
            h = matmul(o, rwkv_wo, slot)
            new_rwkv_p.append(sp)
            new_rwkv_s.append(ss)
            new_shift_p.append(xf[tp - 1:n_prompt_rows:tp])
            new_shift_s.append(xf[n_prompt_rows + ts - 1::ts])
        else:
            q = matmul(xb, ret_wq, slot)
            k = matmul(xb, ret_wk, slot)
            v = matmul(xb, ret_wv, slot)
            gate = matmul(xb, ret_wg, slot, act="silu")
            cos, sin = _rope_tables(ret_dk, cps, past_len)
            o, sp, ss = decay_attention(q, k, v, (cos, sin), gate, ret_gn_g[slot], state_ret, slot,
                                        mode="ret", heads=ret_heads, **seq)
            h = matmul(o, ret_wo, slot)
            new_ret_p.append(sp)
            new_ret_s.append(ss)
        xf, xb = residual_layer_norm(xf, h, ln_g[i, 0], ln_b[i, 0], alpha)
        fslot = i // 2
        split = n_prompt_rows if i == depth - 1 else None
        if i % 2 == 0:
            h, gates = dense_swiglu(xb, ffn_w1, ffn_w3, ffn_w2, fslot), None
        else:
            h, gates = moe_swiglu(xf, xb, moe_router, moe_w1, moe_w3, moe_w2, fslot)
        out_a, out_b = residual_layer_norm(xf, h, ln_g[i, 1], ln_b[i, 1], alpha, gates=gates,
                                           split_rows=split)
        if split is None:
            xf, xb = out_a, out_b
        else:
            y_prompt, y_sample = out_a.reshape(bp, tp, d), out_b.reshape(bs, ts, d)

    return (y_prompt, y_sample,
            jnp.stack(new_gla_p), jnp.stack(new_rwkv_p), jnp.stack(new_shift_p), jnp.stack(new_ret_p),
            jnp.stack(new_gla_s), jnp.stack(new_rwkv_s), jnp.stack(new_shift_s), jnp.stack(new_ret_s))
```

```python
import functools
import math

import jax
import jax.numpy as jnp
from jax import lax
from jax.experimental import pallas as pl
from jax.experimental.pallas import tpu as pltpu

F32 = jnp.float32
BF16 = jnp.bfloat16

CHUNK = 64
LANES = 128
VMEM_LIMIT_BYTES = 56 * 1024 * 1024

LN_EPS = 1e-5
GLA_TAU = 16.0
GLA_NORM_EPS = 1e-5
RWKV_HEAD = 64
RWKV_GN_EPS = 64e-5
RET_ROPE_BASE = 10000.0
RET_GN_EPS = 1e-5
MOE_TOP_K = 2
MOE_BLOCK = 512

_HI = lax.Precision.HIGHEST


def _cparams(sem):
    return pltpu.CompilerParams(dimension_semantics=sem, vmem_limit_bytes=VMEM_LIMIT_BYTES)


def _bdot(a, b):
    return jnp.dot(a.astype(BF16), b.astype(BF16), preferred_element_type=F32)


def _bdot_nt(a, b):
    return lax.dot_general(a.astype(BF16), b.astype(BF16), (((1,), (1,)), ((), ())),
                           preferred_element_type=F32)


def _log_sigmoid(z):
    return -(jnp.maximum(-z, 0.0) + jnp.log1p(jnp.exp(-jnp.abs(z))))


def _act(name, z):
    if name == "none":
        return z
    if name == "silu":
        return z * jax.nn.sigmoid(z)
    if name == "sigmoid":
        return jax.nn.sigmoid(z)
    if name == "tanh":
        return jnp.tanh(z)
    if name == "gla_gate":
        return _log_sigmoid(z) / GLA_TAU
    if name == "rwkv_decay":
        return -jnp.exp(_log_sigmoid(z) - 0.5)
    raise ValueError(name)


def _mm_body(*refs, act, has_bias, scale):
    if has_bias:
        x_ref, w_ref, b_ref, o_ref, wc_ref = refs
    else:
        x_ref, w_ref, o_ref, wc_ref = refs
        b_ref = None

    @pl.when(pl.program_id(1) == 0)
    def _():
        wc_ref[...] = w_ref[...].astype(BF16)

    acc = jnp.dot(x_ref[...], wc_ref[...], preferred_element_type=F32)
    if scale != 1.0:
        acc = acc * scale
    if has_bias:
        acc = acc + b_ref[...]
    o_ref[...] = _act(act, acc).astype(o_ref.dtype)


def _pick(n, pref):
    for t in pref:
        if n % t == 0:
            return t
    return n


MM_WEIGHT_TILE_ELEMS = 2 * 1024 * 1024


def matmul(x, w, slot=0, *, x_slot=None, act="none", bias=None, scale=1.0, out_dtype=F32):
    m, kdim = x.shape[-2:]
    _, kw, n = w.shape
    assert kw == kdim, (w.shape, x.shape)
    tm = _pick(m, (1024, 512, 256, 128, 64, 32, 16, 8))
    tn = _pick(n, tuple(t for t in (1024, 512, 256, 128) if t * kdim <= MM_WEIGHT_TILE_ELEMS))
    if x_slot is None:
        x_spec = pl.BlockSpec((tm, kdim), lambda j, i: (i, 0))
    else:
        x_spec = pl.BlockSpec((None, tm, kdim), lambda j, i: (x_slot, i, 0))
    in_specs = [x_spec, pl.BlockSpec((None, kdim, tn), lambda j, i: (slot, 0, j))]
    args = [x, w]
    if bias is not None:
        in_specs.append(pl.BlockSpec((1, tn), lambda j, i: (0, j)))
        args.append(bias.reshape(1, n).astype(F32))
    return pl.pallas_call(
        functools.partial(_mm_body, act=act, has_bias=bias is not None, scale=scale),
        grid=(n // tn, m // tm),
        in_specs=in_specs,
        out_specs=pl.BlockSpec((tm, tn), lambda j, i: (i, j)),
        out_shape=jax.ShapeDtypeStruct((m, n), out_dtype),
        scratch_shapes=[pltpu.VMEM((kdim, tn), BF16)],
        compiler_params=_cparams(("arbitrary", "arbitrary")),
        name="matmul_" + act,
    )(*args)


def _ln_body(*refs, alpha, gated, routed, head_tiles):
    refs = list(refs)
    x_ref = refs.pop(0)
    if gated:
        h0_ref, h1_ref, gate_ref = refs[:3]
        del refs[:3]
        h = (h0_ref[...].astype(F32) * gate_ref[:, 0:1] + h1_ref[...].astype(F32) * gate_ref[:, 1:2])
    else:
        h = refs.pop(0)[...]
    g_ref, b_ref = refs[:2]
    del refs[:2]
    router_ref = refs.pop(0) if routed else None
    oa_ref, ob_ref = refs[:2]
    z = alpha * x_ref[...] + h
    mu = jnp.mean(z, -1, keepdims=True)
    zc = z - mu
    var = jnp.mean(zc * zc, -1, keepdims=True)
    y = zc * lax.rsqrt(var + LN_EPS) * g_ref[...] + b_ref[...]
    if routed:
        refs[2][...] = jnp.dot(y, router_ref[...], precision=_HI, preferred_element_type=F32)
    if head_tiles is None:
        oa_ref[...] = y
        ob_ref[...] = y.astype(BF16)
    else:
        i = pl.program_id(0)

        @pl.when(i < head_tiles)
        def _():
            oa_ref[...] = y

        @pl.when(i >= head_tiles)
        def _():
            ob_ref[...] = y


def residual_layer_norm(x, h, g, b, alpha, gates=None, split_rows=None, router_w=None):
    m, d = x.shape
    tm = _pick(m, (512, 256, 128, 64, 32, 16, 8))
    row = pl.BlockSpec((tm, d), lambda i: (i, 0))
    vec = pl.BlockSpec((1, d), lambda i: (0, 0))
    if gates is None:
        h_specs, h_args = [row], [h]
    else:
        h_specs = [row, row, pl.BlockSpec((tm, gates.shape[1]), lambda i: (i, 0))]
        h_args = [h[0], h[1], gates]
    if split_rows is None:
        head_tiles = None
        out_specs = [row, row]
        out_shape = [jax.ShapeDtypeStruct((m, d), F32), jax.ShapeDtypeStruct((m, d), BF16)]
    else:
        assert split_rows % tm == 0 and 0 < split_rows < m
        head_tiles = split_rows // tm
        out_specs = [pl.BlockSpec((tm, d), lambda i: (jnp.minimum(i, head_tiles - 1), 0)),
                     pl.BlockSpec((tm, d), lambda i: (jnp.maximum(i - head_tiles, 0), 0))]
        out_shape = [jax.ShapeDtypeStruct((split_rows, d), F32),
                     jax.ShapeDtypeStruct((m - split_rows, d), F32)]
    r_specs, r_args = [], []
    if router_w is not None:
        n_logit = router_w.shape[1]
        r_specs, r_args = [pl.BlockSpec((d, n_logit), lambda i: (0, 0))], [router_w]
        out_specs = out_specs + [pl.BlockSpec((tm, n_logit), lambda i: (i, 0))]
        out_shape = out_shape + [jax.ShapeDtypeStruct((m, n_logit), F32)]
    return pl.pallas_call(
        functools.partial(_ln_body, alpha=alpha, gated=gates is not None,
                          routed=router_w is not None, head_tiles=head_tiles),
        grid=(m // tm,),
        in_specs=[row] + h_specs + [vec, vec] + r_specs,
        out_specs=out_specs,
        out_shape=out_shape,
        compiler_params=_cparams(("parallel",) if split_rows is None else ("arbitrary",)),
        name="residual_layer_norm",
    )(x, *h_args, g.reshape(1, d), b.reshape(1, d), *r_args)


SUBLANES = 8


def _shift_mix_body(x_ref, halo_ref, shift_ref, mu_ref, o_ref, *, tm, tp, ts, prompt_tiles):
    i = pl.program_id(0)
    x = x_ref[...]
    row = lax.broadcasted_iota(jnp.int32, (tm, 1), 0)
    prev = jnp.where(row == 0, halo_ref[SUBLANES - 1:SUBLANES, :], pltpu.roll(x, 1, 0))
    prompt_start = jnp.logical_and(row == 0, (i * tm) % tp == 0)
    prev_prompt = jnp.where(prompt_start, 0.0, prev)
    nseq = tm // ts
    carried = jnp.broadcast_to(shift_ref[...][:, None, :], (nseq, ts, x.shape[1])).reshape(x.shape)
    prev_sample = jnp.where(row % ts == 0, carried, prev)
    xx = jnp.where(i < prompt_tiles, prev_prompt, prev_sample) - x
    for j in range(o_ref.shape[0]):
        o_ref[j] = (x + xx * mu_ref[j:j + 1, :]).astype(BF16)


def shift_mix(x, shift0, mu, *, n_prompt_rows, tp, ts):
    m, d = x.shape
    nmix = mu.shape[0]
    tm = 512
    assert tp % tm == 0 and tm % ts == 0 and n_prompt_rows % tm == 0 and m % tm == 0
    prompt_tiles = n_prompt_rows // tm
    nseq = tm // ts
    assert shift0.shape[0] % nseq == 0
    halo_blocks = tm // SUBLANES
    return pl.pallas_call(
        functools.partial(_shift_mix_body, tm=tm, tp=tp, ts=ts, prompt_tiles=prompt_tiles),
        grid=(m // tm,),
        in_specs=[pl.BlockSpec((tm, d), lambda i: (i, 0)),
                  pl.BlockSpec((SUBLANES, d), lambda i: (jnp.maximum(i * halo_blocks - 1, 0), 0)),
                  pl.BlockSpec((nseq, d), lambda i: (jnp.maximum(i - prompt_tiles, 0), 0)),
                  pl.BlockSpec((nmix, d), lambda i: (0, 0))],
        out_specs=pl.BlockSpec((nmix, tm, d), lambda i: (0, i, 0)),
        out_shape=jax.ShapeDtypeStruct((nmix, m, d), BF16),
        compiler_params=_cparams(("parallel",)),
        name="rwkv_shift_mix",
    )(x, x, shift0, mu)


def _chunk_state_init(c, ncp, cps, state_ref, s0_ref):
    @pl.when(jnp.logical_and(c < ncp, c % cps == 0))
    def _():
        state_ref[...] = jnp.zeros(state_ref.shape, state_ref.dtype)

    @pl.when(c >= ncp)
    def _():
        state_ref[...] = s0_ref[0]


def _dla_body(*refs, heads, dk, dv, mode, ncp, cps, n_aliased):
    o_ref, sp_ref, ss_ref, st_ref = refs[-4:]
    refs = refs[:len(refs) - 4 - n_aliased]
    if mode == "gla":
        q_ref, k_ref, v_ref, g_ref, gate_ref, ng_ref, s0_ref = refs
    else:
        q_ref, k_ref, v_ref, cos_ref, sin_ref, gate_ref, ng_ref, s0_ref = refs
    c = pl.program_id(0)
    _chunk_state_init(c, ncp, cps, st_ref, s0_ref)

    t_row = lax.broadcasted_iota(jnp.int32, (CHUNK, CHUNK), 0)
    t_col = lax.broadcasted_iota(jnp.int32, (CHUNK, CHUNK), 1)
    causal = t_row >= t_col
    if mode == "gla":
        lower_ones = causal.astype(F32)
        ones_cols = jnp.ones((CHUNK, LANES), F32)
    else:
        width = heads * dk
        even = (lax.broadcasted_iota(jnp.int32, (CHUNK, width), 1) % 2) == 0
        q_all = q_ref[...]
        k_all = k_ref[...]
        q_sw = jnp.where(even, pltpu.roll(q_all, width - 1, 1), pltpu.roll(q_all, 1, 1))
        k_sw = jnp.where(even, pltpu.roll(k_all, width - 1, 1), pltpu.roll(k_all, 1, 1))
        cos = cos_ref[...]
        sin = sin_ref[...]
        frame = (lax.broadcasted_iota(jnp.int32, (CHUNK, 1), 0) + 1).astype(F32)

    for h in range(heads):
        ks = slice(h * dk, (h + 1) * dk)
        vs = slice(h * dv, (h + 1) * dv)
        v = v_ref[:, vs]
        s_prev = st_ref[h]
        if mode == "gla":
            q = q_ref[:, ks]
            k = k_ref[:, ks]
            g = g_ref[:, ks]
            cum = jnp.dot(lower_ones, g, precision=_HI, preferred_element_type=F32)
            total = cum[CHUNK - 1:CHUNK, :]
            total_col = lax.dot_general(g, ones_cols, (((0,), (0,)), ((), ())), precision=_HI,
                                        preferred_element_type=F32)[:, :1]
            state_decay = jnp.exp(total_col)
            q_dec = q * jnp.exp(cum)
            k_inv = k * jnp.exp(-cum)
            k_tail = k * jnp.exp(total - cum)
        else:
            q = q_all[:, ks] * cos + q_sw[:, ks] * sin
            k = (k_all[:, ks] * cos + k_sw[:, ks] * sin) * (dk ** -0.5)
            log_gamma = math.log1p(-(2.0 ** (-5.0 - h)))
            cum = frame * log_gamma
            total = CHUNK * log_gamma
            state_decay = math.exp(total)
            q_dec = q * jnp.exp(cum)
            k_inv = k * jnp.exp(-cum)
            k_tail = k * jnp.exp(total - cum)
        scores = jnp.where(causal, _bdot_nt(q_dec, k_inv), 0.0)
        o = _bdot(q_dec, s_prev) + _bdot(scores, v)
        k_tail_t = jnp.transpose(k_tail)
        st_ref[h] = state_decay * s_prev + _bdot(k_tail_t, v)
        if mode == "gla":
            o = o * lax.rsqrt(jnp.mean(o * o, -1, keepdims=True) + GLA_NORM_EPS) * ng_ref[...]
        else:
            mu = jnp.mean(o, -1, keepdims=True)
            oc = o - mu
            var = jnp.mean(oc * oc, -1, keepdims=True)
            o = oc * lax.rsqrt(var + RET_GN_EPS) * ng_ref[:, vs]
        if mode == "gla":
            o_ref[:, vs] = (o * gate_ref[:, vs]).astype(BF16)
        else:
            o_ref[:, vs] = (gate_ref[:, vs] * o).astype(BF16)

    @pl.when(jnp.logical_and(c < ncp, c % cps == cps - 1))
    def _():
        sp_ref[0] = st_ref[...]

    @pl.when(c >= ncp)
    def _():
        ss_ref[0] = st_ref[...]


def _seq_state_specs(state_shape, ncp, cps, n_prompt, slot):
    blk = (None, 1) + tuple(state_shape)
    zeros = (0,) * len(state_shape)
    s0_spec = pl.BlockSpec(blk, lambda c: (slot, jnp.maximum(c - ncp, 0)) + zeros)
    sp_spec = pl.BlockSpec(blk, lambda c: (slot, jnp.minimum(c // cps, n_prompt - 1)) + zeros)
    ss_spec = pl.BlockSpec(blk, lambda c: (slot, jnp.maximum(c - ncp, 0)) + zeros)
    return s0_spec, sp_spec, ss_spec


def _stacked_state_io(n_slots, n_prompt, n_sample, state_shape, n_inputs, prev_states):
    shapes = [jax.ShapeDtypeStruct((n_slots, n_prompt) + tuple(state_shape), F32),
              jax.ShapeDtypeStruct((n_slots, n_sample) + tuple(state_shape), F32)]
    if prev_states is None:
        return shapes, [], [], {}
    any_spec = pl.BlockSpec(memory_space=pl.ANY)
    return shapes, [any_spec, any_spec], list(prev_states), {n_inputs: 1, n_inputs + 1: 2}


def decay_attention(q, k, v, extra, gate, norm_g, s0, slot, prev_states=None, *, mode, heads,
                    n_prompt, cps):
    nt = q.shape[0]
    dk = q.shape[1] // heads
    dv = v.shape[1] // heads
    n_sample = s0.shape[1]
    ncp = n_prompt * cps
    nchunks = nt // CHUNK
    assert nchunks == ncp + n_sample
    rowk = pl.BlockSpec((CHUNK, heads * dk), lambda c: (c, 0))
    rowv = pl.BlockSpec((CHUNK, heads * dv), lambda c: (c, 0))
    s0_spec, sp_spec, ss_spec = _seq_state_specs((heads, dk, dv), ncp, cps, n_prompt, slot)
    if mode == "gla":
        extra_specs = [rowk]
        ng_spec = pl.BlockSpec((1, dv), lambda c: (0, 0))
        norm_g = norm_g.reshape(1, dv)
    else:
        pos_spec = pl.BlockSpec((CHUNK, dk), lambda c: (jnp.where(c < ncp, c % cps, cps), 0))
        extra_specs = [pos_spec, pos_spec]
        ng_spec = pl.BlockSpec((1, heads * dv), lambda c: (0, 0))
        norm_g = norm_g.reshape(1, heads * dv)
    in_specs = [rowk, rowk, rowv] + extra_specs + [rowv, ng_spec, s0_spec]
    state_shapes, alias_specs, alias_args, aliases = _stacked_state_io(
        s0.shape[0], n_prompt, n_sample, (heads, dk, dv), len(in_specs), prev_states)
    return pl.pallas_call(
        functools.partial(_dla_body, heads=heads, dk=dk, dv=dv, mode=mode, ncp=ncp, cps=cps,
                          n_aliased=len(alias_args)),
        grid=(nchunks,),
        in_specs=in_specs + alias_specs,
        out_specs=[rowv, sp_spec, ss_spec],
        out_shape=[jax.ShapeDtypeStruct((nt, heads * dv), BF16)] + state_shapes,
        input_output_aliases=aliases,
        scratch_shapes=[pltpu.VMEM((heads, dk, dv), F32)],
        compiler_params=_cparams(("arbitrary",)),
        name="decay_attention_" + mode,
    )(q, k, v, *extra, gate, norm_g, s0, *alias_args)


def _rwkv_body(r_ref, k_ref, v_ref, lw_ref, a_ref, g_ref, kk_ref, ka_ref, rk_ref, gng_ref, gnb_ref,
               s0_ref, *rest, pairs, ncp, cps):
    o_ref, sp_ref, ss_ref, st_ref = rest[-4:]
    c = pl.program_id(0)
    n = RWKV_HEAD
    w2 = 2 * n

    @pl.when(jnp.logical_and(c < ncp, c % cps == 0))
    def _():
        st_ref[...] = jnp.zeros(st_ref.shape, F32)

    @pl.when(c >= ncp)
    def _():
        zero = jnp.zeros((n, n), F32)
        for p in range(pairs):
            top = jnp.concatenate([s0_ref[0, 2 * p], zero], axis=1)
            bot = jnp.concatenate([zero, s0_ref[0, 2 * p + 1]], axis=1)
            st_ref[p] = jnp.concatenate([top, bot], axis=0)

    def paired(ref):
        x = ref[...]
        return jnp.stack([x[:, p * w2:(p + 1) * w2] for p in range(pairs)])

    lane = lax.broadcasted_iota(jnp.int32, (1, 1, w2), 2)
    first = lane < n

    def head_sum(x):
        s_a = jnp.sum(jnp.where(first, x, 0.0), -1, keepdims=True)
        s_b = jnp.sum(jnp.where(first, 0.0, x), -1, keepdims=True)
        return jnp.where(first, s_a, s_b)

    def stacked(x):
        return jnp.concatenate([jnp.where(first, x, 0.0), jnp.where(first, 0.0, x)], axis=1)

    def bmm(a, b):
        return lax.dot_general(a.astype(BF16), b.astype(BF16), (((2,), (1,)), ((0,), (0,))),
                               preferred_element_type=F32)

    def bmm_nt(a, b):
        return lax.dot_general(a.astype(BF16), b.astype(BF16), (((2,), (2,)), ((0,), (0,))),
                               preferred_element_type=F32)

    r = paired(r_ref)
    k_raw = paired(k_ref)
    v = paired(v_ref)
    lw = paired(lw_ref)
    a = paired(a_ref)
    k_k = paired(kk_ref)
    k_a = paired(ka_ref)
    r_k = paired(rk_ref)

    kk = k_raw * k_k
    kk = kk / jnp.maximum(jnp.sqrt(head_sum(kk * kk)), 1e-12)
    k_h = k_raw * (1.0 + (a - 1.0) * k_a)
    a_vec = -kk
    b_vec = kk * a

    t_row = lax.broadcasted_iota(jnp.int32, (CHUNK, CHUNK), 0)
    t_col = lax.broadcasted_iota(jnp.int32, (CHUNK, CHUNK), 1)
    lower_ones = (t_row >= t_col).astype(F32)
    cum = paired_value(jnp.dot(lower_ones, lw_ref[...], precision=_HI, preferred_element_type=F32),
                       pairs, w2)
    total = cum[:, CHUNK - 1:CHUNK, :]
    p_now = jnp.exp(cum)
    p_prev = jnp.exp(cum - lw)
    p_inv = jnp.exp(-cum)
    p_tail = jnp.exp(total - cum)

    lhs = jnp.concatenate([stacked(a_vec * p_prev), stacked(r * p_now)], axis=1)
    rhs = jnp.concatenate([stacked(b_vec * p_inv), stacked(k_h * p_inv)], axis=1)
    sc = bmm_nt(lhs, rhs)
    s_prev = st_ref[...]
    sr = bmm_nt(lhs, s_prev)

    i_row = lax.broadcasted_iota(jnp.int32, (1, w2, w2), 1)
    i_col = lax.broadcasted_iota(jnp.int32, (1, w2, w2), 2)
    same = (i_row // n) == (i_col // n)
    strict = jnp.logical_and(same, (i_col % n) < (i_row % n))
    incl = jnp.logical_and(same, (i_col % n) <= (i_row % n))
    a_ab = jnp.where(strict, sc[:, :w2, :w2], 0.0)
    a_ak = jnp.where(strict, sc[:, :w2, w2:], 0.0)
    a_rb = jnp.where(incl, sc[:, w2:, :w2], 0.0)
    a_rk = jnp.where(incl, sc[:, w2:, w2:], 0.0)
    u0 = sr[:, :w2]
    y0 = sr[:, w2:]
    v_st = stacked(v)

    eye = (i_row == i_col).astype(F32)
    t_inv = eye + a_ab
    power = a_ab
    span = 1
    while 2 * span < CHUNK:
        power = bmm(power, power)
        t_inv = t_inv + bmm(t_inv, power)
        span *= 2

    u_st = bmm(t_inv, u0 + bmm(a_ak, v_st))
    uv = jnp.concatenate([u_st, v_st], axis=1)
    y_st = y0 + bmm(jnp.concatenate([a_rb, a_rk], axis=2), uv)
    y = y_st[:, :n] + y_st[:, n:]

    tails = jnp.concatenate([stacked(b_vec * p_tail), stacked(k_h * p_tail)], axis=1)
    uv_t = jnp.swapaxes(uv, 1, 2)
    s_new = s_prev * jnp.exp(total) + bmm(uv_t, tails)
    st_ref[...] = s_new

    mu = head_sum(y) * (1.0 / n)
    yc = y - mu
    var = head_sum(yc * yc) * (1.0 / n)
    yn = yc * lax.rsqrt(var + RWKV_GN_EPS) * paired(gng_ref) + paired(gnb_ref)
    out = yn + head_sum(r * k_h * r_k) * v
    gate = paired(g_ref)
    for p in range(pairs):
        o_ref[:, p * w2:(p + 1) * w2] = (out[p] * gate[p]).astype(BF16)

    def store_state(dst_ref):
        for p in range(pairs):
            dst_ref[0, 2 * p] = s_new[p, :n, :n]
            dst_ref[0, 2 * p + 1] = s_new[p, n:, n:]

    @pl.when(jnp.logical_and(c < ncp, c % cps == cps - 1))
    def _():
        store_state(sp_ref)

    @pl.when(c >= ncp)
    def _():
        store_state(ss_ref)


def paired_value(x, pairs, w2):
    return jnp.stack([x[:, p * w2:(p + 1) * w2] for p in range(pairs)])


def rwkv7_attention(r, k, v, lw, a, g, k_k, k_a, r_k, gn_g, gn_b, s0, slot, prev_states=None, *,
                    n_prompt, cps):
    nt, d = r.shape
    heads = d // RWKV_HEAD
    pairs = heads // 2
    n_sample = s0.shape[1]
    ncp = n_prompt * cps
    assert nt // CHUNK == ncp + n_sample
    row = pl.BlockSpec((CHUNK, d), lambda c: (c, 0))
    vec = pl.BlockSpec((1, d), lambda c: (0, 0))
    s0_spec, sp_spec, ss_spec = _seq_state_specs((heads, RWKV_HEAD, RWKV_HEAD), ncp, cps, n_prompt,
                                                 slot)
    vecs = [u.reshape(1, d) for u in (k_k, k_a, r_k, gn_g, gn_b)]
    in_specs = [row] * 6 + [vec] * 5 + [s0_spec]
    state_shapes, alias_specs, alias_args, aliases = _stacked_state_io(
        s0.shape[0], n_prompt, n_sample, (heads, RWKV_HEAD, RWKV_HEAD), len(in_specs), prev_states)
    return pl.pallas_call(
        functools.partial(_rwkv_body, pairs=pairs, ncp=ncp, cps=cps),
        grid=(nt // CHUNK,),
        in_specs=in_specs + alias_specs,
        out_specs=[row, sp_spec, ss_spec],
        out_shape=[jax.ShapeDtypeStruct((nt, d), BF16)] + state_shapes,
        input_output_aliases=aliases,
        scratch_shapes=[pltpu.VMEM((pairs, 2 * RWKV_HEAD, 2 * RWKV_HEAD), F32)],
        compiler_params=_cparams(("arbitrary",)),
        name="rwkv7_attention",
    )(r, k, v, lw, a, g, *vecs, s0, *alias_args)


def _gate_up_body(be_ref, chg_ref, nu_ref, x_ref, w1_ref, w3_ref, h_ref, w1c_ref, w3c_ref):
    b = pl.program_id(1)

    @pl.when(chg_ref[b] == 1)
    def _():
        w1c_ref[...] = w1_ref[...].astype(BF16)
        w3c_ref[...] = w3_ref[...].astype(BF16)

    @pl.when(b < nu_ref[0])
    def _():
        x = x_ref[...]
        gate = jnp.dot(x, w1c_ref[...], preferred_element_type=F32)
        up = jnp.dot(x, w3c_ref[...], preferred_element_type=F32)
        h_ref[...] = (gate * jax.nn.sigmoid(gate) * up).astype(BF16)

    @pl.when(b >= nu_ref[0])
    def _():
        h_ref[...] = jnp.zeros(h_ref.shape, BF16)


def _down_body(be_ref, chg_ref, nu_ref, h_ref, w2_ref, y_ref, w2c_ref):
    b = pl.program_id(1)

    @pl.when(chg_ref[b] == 1)
    def _():
        w2c_ref[...] = w2_ref[...].astype(BF16)

    @pl.when(b < nu_ref[0])
    def _():
        y_ref[...] = jnp.dot(h_ref[...], w2c_ref[...],
                             preferred_element_type=F32).astype(y_ref.dtype)

    @pl.when(b >= nu_ref[0])
    def _():
        y_ref[...] = jnp.zeros(y_ref.shape, y_ref.dtype)


def _block_meta(block_expert, n_used):
    be = block_expert.astype(jnp.int32)
    changed = jnp.concatenate([jnp.ones((1,), jnp.int32),
                               (be[1:] != be[:-1]).astype(jnp.int32)])
    return be, changed, jnp.reshape(n_used, (1,)).astype(jnp.int32)


def swiglu_gate_up(xs, block_expert, n_used, w1, w3, block, tile_f=(512, 256, 128)):
    rows, d = xs.shape
    f = w1.shape[-1]
    tf = _pick(f, tile_f)
    return pl.pallas_call(
        _gate_up_body,
        grid_spec=pltpu.PrefetchScalarGridSpec(
            num_scalar_prefetch=3,
            grid=(f // tf, rows // block),
            in_specs=[pl.BlockSpec((block, d), lambda j, b, be, ch, nu: (b, 0)),
                      pl.BlockSpec((None, d, tf), lambda j, b, be, ch, nu: (be[b], 0, j)),
                      pl.BlockSpec((None, d, tf), lambda j, b, be, ch, nu: (be[b], 0, j))],
            out_specs=pl.BlockSpec((block, tf), lambda j, b, be, ch, nu: (b, j)),
            scratch_shapes=[pltpu.VMEM((d, tf), BF16), pltpu.VMEM((d, tf), BF16)]),
        out_shape=jax.ShapeDtypeStruct((rows, f), BF16),
        compiler_params=_cparams(("arbitrary", "arbitrary")),
        name="swiglu_gate_up",
    )(*_block_meta(block_expert, n_used), xs, w1, w3)


def swiglu_down(h, block_expert, n_used, w2, block, out_dtype=F32):
    rows, f = h.shape
    d = w2.shape[-1]
    tn = _pick(d, (512, 256, 128))
    w2_spec = pl.BlockSpec((None, f, tn), lambda j, b, be, ch, nu: (be[b], 0, j),
                           pipeline_mode=pl.Buffered(1))
    return pl.pallas_call(
        _down_body,
        grid_spec=pltpu.PrefetchScalarGridSpec(
            num_scalar_prefetch=3,
            grid=(d // tn, rows // block),
            in_specs=[pl.BlockSpec((block, f), lambda j, b, be, ch, nu: (b, 0)), w2_spec],
            out_specs=pl.BlockSpec((block, tn), lambda j, b, be, ch, nu: (b, j)),
            scratch_shapes=[pltpu.VMEM((f, tn), BF16)]),
        out_shape=jax.ShapeDtypeStruct((rows, d), out_dtype),
        compiler_params=_cparams(("arbitrary", "arbitrary")),
        name="swiglu_down",
    )(*_block_meta(block_expert, n_used), h, w2)


def _pad_cols(w, mult=LANES):
    pad = (-w.shape[-1]) % mult
    return jnp.pad(w, [(0, 0)] * (w.ndim - 1) + [(0, pad)]) if pad else w


def _pad_rows(w, mult=LANES):
    pad = (-w.shape[-2]) % mult
    return jnp.pad(w, [(0, 0)] * (w.ndim - 2) + [(0, pad), (0, 0)]) if pad else w


DENSE_GATE_UP_BLOCK = 1024
DENSE_DOWN_BLOCK = 512
MOE_GATE_UP_TILE_F = (896, 512, 256, 128)


def dense_swiglu(xb, w1, w3, w2, slot):
    rows = xb.shape[0]
    assert rows % DENSE_GATE_UP_BLOCK == 0 and rows % DENSE_DOWN_BLOCK == 0
    nb_a = rows // DENSE_GATE_UP_BLOCK
    nb_b = rows // DENSE_DOWN_BLOCK
    h = swiglu_gate_up(xb, jnp.full((nb_a,), slot, jnp.int32), jnp.int32(nb_a), w1, w3,
                       DENSE_GATE_UP_BLOCK)
    return swiglu_down(h, jnp.full((nb_b,), slot, jnp.int32), jnp.int32(nb_b), w2, DENSE_DOWN_BLOCK)


def moe_swiglu(logits, xb, w1, w3, w2, slot):
    n, d = xb.shape
    n_exp = logits.shape[-1]
    top_val, top_idx = lax.top_k(logits, MOE_TOP_K)
    gates = jax.nn.softmax(top_val, axis=-1)
    slots = n * MOE_TOP_K
    flat_e = top_idx.reshape(-1)
    onehot = (flat_e[:, None] == jnp.arange(n_exp, dtype=flat_e.dtype)[None, :]).astype(jnp.int32)
    rank = jnp.sum((jnp.cumsum(onehot, axis=0) - onehot) * onehot, axis=1)
    counts = jnp.sum(onehot, axis=0)
    start = jnp.cumsum(counts) - counts
    padded = (counts + MOE_BLOCK - 1) // MOE_BLOCK * MOE_BLOCK
    pend = jnp.cumsum(padded)
    pstart = pend - padded
    nb = (slots + n_exp * (MOE_BLOCK - 1) + MOE_BLOCK - 1) // MOE_BLOCK
    dest = (pstart[flat_e] + rank).astype(jnp.int32)
    block_start = jnp.arange(nb, dtype=jnp.int32) * MOE_BLOCK
    expert_of_block = jnp.clip(jnp.searchsorted(pend, block_start, side="right"), 0, n_exp - 1)
    order = jnp.argsort(flat_e, stable=True)
    e_row = jnp.repeat(expert_of_block, MOE_BLOCK)
    rank_row = jnp.arange(nb * MOE_BLOCK, dtype=jnp.int32) - pstart[e_row]
    valid = rank_row < counts[e_row]
    slot_row = order[jnp.clip(start[e_row] + rank_row, 0, slots - 1)]
    src_tok = jnp.where(valid, slot_row // MOE_TOP_K, 0).astype(jnp.int32)
    block_expert = expert_of_block.astype(jnp.int32) + slot * n_exp
    n_used = pend[-1] // MOE_BLOCK
    xs = jnp.take(xb, src_tok, axis=0, mode="clip")
    e1 = w1.reshape((-1,) + w1.shape[2:])
    e3 = w3.reshape((-1,) + w3.shape[2:])
    e2 = w2.reshape((-1,) + w2.shape[2:])
    hs = swiglu_gate_up(xs, block_expert, n_used, e1, e3, MOE_BLOCK, tile_f=MOE_GATE_UP_TILE_F)
    ys = swiglu_down(hs, block_expert, n_used, e2, MOE_BLOCK, out_dtype=BF16)
    dest2 = dest.reshape(n, MOE_TOP_K)
    picked = [jnp.take(ys, dest2[:, j], axis=0, mode="clip") for j in range(MOE_TOP_K)]
    return picked, gates


def _rope_tables(dk, cps, past_len):
    half = dk // 2
    inv_freq = 1.0 / (RET_ROPE_BASE ** jnp.linspace(0.0, 1.0, half, dtype=F32))
    pos = jnp.concatenate([jnp.arange(cps * CHUNK, dtype=jnp.int32),
                           past_len + jnp.arange(CHUNK, dtype=jnp.int32)])
    ang = pos.astype(F32)[:, None] * inv_freq[None, :]
    cos = jnp.repeat(jnp.cos(ang), 2, axis=1)
    sin = jnp.stack([-jnp.sin(ang), jnp.sin(ang)], axis=-1).reshape(pos.shape[0], dk)
    return cos, sin


def kernel(x_prompt, x_sample, state_gla, state_rwkv, state_shift, state_ret, ln_g, ln_b,
           gla_wq, gla_wk, gla_wv, gla_wr, gla_wa1, gla_wa2, gla_ba, gla_norm_g, gla_wo,
           rwkv_mu, rwkv_wr, rwkv_wk, rwkv_wv, rwkv_wo, rwkv_w0, rwkv_w1, rwkv_w2, rwkv_a0, rwkv_a1,
           rwkv_a2, rwkv_g1, rwkv_g2, rwkv_k_k, rwkv_k_a, rwkv_r_k, rwkv_gn_g, rwkv_gn_b,
           ret_wq, ret_wk, ret_wv, ret_wg, ret_gn_g, ret_wo,
           ffn_w1, ffn_w3, ffn_w2, moe_router, moe_w1, moe_w3, moe_w2):
    bp, tp, d = x_prompt.shape
    bs, ts, _ = x_sample.shape
    assert ts == CHUNK and tp % CHUNK == 0
    depth = ln_g.shape[0]
    alpha = (2.0 * depth) ** 0.25
    cps = tp // CHUNK
    past_len = tp
    n_prompt_rows = bp * tp
    seq = dict(n_prompt=bp, cps=cps)

    gla_heads = state_gla.shape[2]
    gla_dk = state_gla.shape[3]
    ret_heads = state_ret.shape[2]
    ret_dk = state_ret.shape[3]

    xf = jnp.concatenate([x_prompt.reshape(-1, d), x_sample.reshape(-1, d)], axis=0)
    xb = xf.astype(BF16)

    gla_states = rwkv_states = ret_states = None
    new_shift_p, new_shift_s = [], []
    last_rows = jnp.concatenate([jnp.arange(bp, dtype=jnp.int32) * tp + (tp - 1),
                                 n_prompt_rows + jnp.arange(bs, dtype=jnp.int32) * ts + (ts - 1)])
    for i in range(depth):
        kind, slot = i % 3, i // 3
        if kind == 0:
            q = matmul(xb, gla_wq, slot, scale=gla_dk ** -0.5)
            k = matmul(xb, gla_wk, slot)
            v = matmul(xb, gla_wv, slot)
            gate = matmul(xb, gla_wr, slot, act="silu")
            low = matmul(xb, _pad_cols(gla_wa1), slot, out_dtype=BF16)
            log_alpha = matmul(low, _pad_rows(gla_wa2), slot, act="gla_gate", bias=gla_ba[slot])
            o, *gla_states = decay_attention(q, k, v, (log_alpha,), gate, gla_norm_g[slot], state_gla,
                                             slot, gla_states, mode="gla", heads=gla_heads, **seq)
            h = matmul(o, gla_wo, slot)
        elif kind == 1:
            mixes = shift_mix(xf, state_shift[slot], rwkv_mu[slot], n_prompt_rows=n_prompt_rows,
                              tp=tp, ts=ts)
            i_r, i_w, i_k, i_v, i_a, i_g = range(6)
            r = matmul(mixes, rwkv_wr, slot, x_slot=i_r)
            k = matmul(mixes, rwkv_wk, slot, x_slot=i_k)
            v = matmul(mixes, rwkv_wv, slot, x_slot=i_v)
            w_mid = matmul(mixes, _pad_cols(rwkv_w1), slot, x_slot=i_w, act="tanh", out_dtype=BF16)
            log_decay = matmul(w_mid, _pad_rows(rwkv_w2), slot, act="rwkv_decay", bias=rwkv_w0[slot])
            a_mid = matmul(mixes, _pad_cols(rwkv_a1), slot, x_slot=i_a, out_dtype=BF16)
            a = matmul(a_mid, _pad_rows(rwkv_a2), slot, act="sigmoid", bias=rwkv_a0[slot])
            g_mid = matmul(mixes, _pad_cols(rwkv_g1), slot, x_slot=i_g, act="sigmoid", out_dtype=BF16)
            g = matmul(g_mid, _pad_rows(rwkv_g2), slot)
            o, *rwkv_states = rwkv7_attention(r, k, v, log_decay, a, g, rwkv_k_k[slot], rwkv_k_a[slot],
                                              rwkv_r_k[slot].reshape(-1), rwkv_gn_g[slot],
                                              rwkv_gn_b[slot], state_rwkv, slot, rwkv_states, **seq)
            h = matmul(o, rwkv_wo, slot)
            ends = jnp.take(xf, last_rows, axis=0, mode="clip")
            new_shift_p.append(ends[:bp])
            new_shift_s.append(ends[bp:])
        else:
            q = matmul(xb, ret_wq, slot)
            k = matmul(xb, ret_wk, slot)
            v = matmul(xb, ret_wv, slot)
            gate = matmul(xb, ret_wg, slot, act="silu")
            cos, sin = _rope_tables(ret_dk, cps, past_len)
            o, *ret_states = decay_attention(q, k, v, (cos, sin), gate, ret_gn_g[slot], state_ret, slot,
                                             ret_states, mode="ret", heads=ret_heads, **seq)
            h = matmul(o, ret_wo, slot)
        fslot = i // 2
        is_moe = i % 2 == 1
        router_w = _pad_cols(moe_router[fslot]) if is_moe else None
        xf, xb, *logits = residual_layer_norm(xf, h, ln_g[i, 0], ln_b[i, 0], alpha, router_w=router_w)
        split = n_prompt_rows if i == depth - 1 else None
        if is_moe:
            h, gates = moe_swiglu(logits[0][:, :moe_router.shape[-1]], xb, moe_w1, moe_w3, moe_w2, fslot)
        else:
            h, gates = dense_swiglu(xb, ffn_w1, ffn_w3, ffn_w2, fslot), None
        out_a, out_b = residual_layer_norm(xf, h, ln_g[i, 1], ln_b[i, 1], alpha, gates=gates,
                                           split_rows=split)
        if split is None:
            xf, xb = out_a, out_b
        else:
            y_prompt, y_sample = out_a.reshape(bp, tp, d), out_b.reshape(bs, ts, d)

    return (y_prompt, y_sample,
            gla_states[0], rwkv_states[0], jnp.stack(new_shift_p), ret_states[0],
            gla_states[1], rwkv_states[1], jnp.stack(new_shift_s), ret_states[1])
```

```python
import functools
import math

import jax
import jax.numpy as jnp
from jax import lax
from jax.experimental import pallas as pl
from jax.experimental.pallas import tpu as pltpu

F32 = jnp.float32
BF16 = jnp.bfloat16

CHUNK = 64
LANES = 128
VMEM_LIMIT_BYTES = 56 * 1024 * 1024

LN_EPS = 1e-5
GLA_TAU = 16.0
GLA_NORM_EPS = 1e-5
RWKV_HEAD = 64
RWKV_GN_EPS = 64e-5
RET_ROPE_BASE = 10000.0
RET_GN_EPS = 1e-5
MOE_TOP_K = 2
MOE_BLOCK = 512

_HI = lax.Precision.HIGHEST


def _cparams(sem):
    return pltpu.CompilerParams(dimension_semantics=sem, vmem_limit_bytes=VMEM_LIMIT_BYTES)


def _bdot(a, b):
    return jnp.dot(a.astype(BF16), b.astype(BF16), preferred_element_type=F32)


def _bdot_nt(a, b):
    return lax.dot_general(a.astype(BF16), b.astype(BF16), (((1,), (1,)), ((), ())),
                           preferred_element_type=F32)


def _log_sigmoid(z):
    return -(jnp.maximum(-z, 0.0) + jnp.log1p(jnp.exp(-jnp.abs(z))))


def _act(name, z):
    if name == "none":
        return z
    if name == "silu":
        return z * jax.nn.sigmoid(z)
    if name == "sigmoid":
        return jax.nn.sigmoid(z)
    if name == "tanh":
        return jnp.tanh(z)
    if name == "gla_gate":
        return _log_sigmoid(z) / GLA_TAU
    if name == "rwkv_decay":
        return -jnp.exp(_log_sigmoid(z) - 0.5)
    raise ValueError(name)


def _mm_body(*refs, act, has_bias, scale):
    if has_bias:
        x_ref, w_ref, b_ref, o_ref, wc_ref = refs
    else:
        x_ref, w_ref, o_ref, wc_ref = refs
        b_ref = None

    @pl.when(pl.program_id(1) == 0)
    def _():
        wc_ref[...] = w_ref[...].astype(BF16)

    acc = jnp.dot(x_ref[...], wc_ref[...], preferred_element_type=F32)
    if scale != 1.0:
        acc = acc * scale
    if has_bias:
        acc = acc + b_ref[...]
    o_ref[...] = _act(act, acc).astype(o_ref.dtype)


def _pick(n, pref):
    for t in pref:
        if n % t == 0:
            return t
    return n


MM_WEIGHT_TILE_ELEMS = 2 * 1024 * 1024


def matmul(x, w, slot=0, *, x_slot=None, act="none", bias=None, scale=1.0, out_dtype=F32):
    m, kdim = x.shape[-2:]
    _, kw, n = w.shape
    assert kw == kdim, (w.shape, x.shape)
    tm = _pick(m, (1024, 512, 256, 128, 64, 32, 16, 8))
    tn = _pick(n, tuple(t for t in (1024, 512, 256, 128) if t * kdim <= MM_WEIGHT_TILE_ELEMS))
    if x_slot is None:
        x_spec = pl.BlockSpec((tm, kdim), lambda j, i: (i, 0))
    else:
        x_spec = pl.BlockSpec((None, tm, kdim), lambda j, i: (x_slot, i, 0))
    in_specs = [x_spec, pl.BlockSpec((None, kdim, tn), lambda j, i: (slot, 0, j))]
    args = [x, w]
    if bias is not None:
        in_specs.append(pl.BlockSpec((1, tn), lambda j, i: (0, j)))
        args.append(bias.reshape(1, n).astype(F32))
    return pl.pallas_call(
        functools.partial(_mm_body, act=act, has_bias=bias is not None, scale=scale),
        grid=(n // tn, m // tm),
        in_specs=in_specs,
        out_specs=pl.BlockSpec((tm, tn), lambda j, i: (i, j)),
        out_shape=jax.ShapeDtypeStruct((m, n), out_dtype),
        scratch_shapes=[pltpu.VMEM((kdim, tn), BF16)],
        compiler_params=_cparams(("arbitrary", "arbitrary")),
        name="matmul_" + act,
    )(*args)


def _ln_body(*refs, alpha, gated, routed, head_tiles):
    refs = list(refs)
    x_ref = refs.pop(0)
    if gated:
        h0_ref, h1_ref, gate_ref = refs[:3]
        del refs[:3]
        h = (h0_ref[...].astype(F32) * gate_ref[:, 0:1] + h1_ref[...].astype(F32) * gate_ref[:, 1:2])
    else:
        h = refs.pop(0)[...]
    g_ref, b_ref = refs[:2]
    del refs[:2]
    router_ref = refs.pop(0) if routed else None
    oa_ref, ob_ref = refs[:2]
    z = alpha * x_ref[...] + h
    mu = jnp.mean(z, -1, keepdims=True)
    zc = z - mu
    var = jnp.mean(zc * zc, -1, keepdims=True)
    y = zc * lax.rsqrt(var + LN_EPS) * g_ref[...] + b_ref[...]
    if routed:
        refs[2][...] = jnp.dot(y, router_ref[...], precision=_HI, preferred_element_type=F32)
    if head_tiles is None:
        oa_ref[...] = y
        ob_ref[...] = y.astype(BF16)
    else:
        i = pl.program_id(0)

        @pl.when(i < head_tiles)
        def _():
            oa_ref[...] = y

        @pl.when(i >= head_tiles)
        def _():
            ob_ref[...] = y


def residual_layer_norm(x, h, g, b, alpha, gates=None, split_rows=None, router_w=None):
    m, d = x.shape
    tm = _pick(m, (512, 256, 128, 64, 32, 16, 8))
    row = pl.BlockSpec((tm, d), lambda i: (i, 0))
    vec = pl.BlockSpec((1, d), lambda i: (0, 0))
    if gates is None:
        h_specs, h_args = [row], [h]
    else:
        h_specs = [row, row, pl.BlockSpec((tm, gates.shape[1]), lambda i: (i, 0))]
        h_args = [h[0], h[1], gates]
    if split_rows is None:
        head_tiles = None
        out_specs = [row, row]
        out_shape = [jax.ShapeDtypeStruct((m, d), F32), jax.ShapeDtypeStruct((m, d), BF16)]
    else:
        assert split_rows % tm == 0 and 0 < split_rows < m
        head_tiles = split_rows // tm
        out_specs = [pl.BlockSpec((tm, d), lambda i: (jnp.minimum(i, head_tiles - 1), 0)),
                     pl.BlockSpec((tm, d), lambda i: (jnp.maximum(i - head_tiles, 0), 0))]
        out_shape = [jax.ShapeDtypeStruct((split_rows, d), F32),
                     jax.ShapeDtypeStruct((m - split_rows, d), F32)]
    r_specs, r_args = [], []
    if router_w is not None:
        n_logit = router_w.shape[1]
        r_specs, r_args = [pl.BlockSpec((d, n_logit), lambda i: (0, 0))], [router_w]
        out_specs = out_specs + [pl.BlockSpec((tm, n_logit), lambda i: (i, 0))]
        out_shape = out_shape + [jax.ShapeDtypeStruct((m, n_logit), F32)]
    return pl.pallas_call(
        functools.partial(_ln_body, alpha=alpha, gated=gates is not None,
                          routed=router_w is not None, head_tiles=head_tiles),
        grid=(m // tm,),
        in_specs=[row] + h_specs + [vec, vec] + r_specs,
        out_specs=out_specs,
        out_shape=out_shape,
        compiler_params=_cparams(("parallel",) if split_rows is None else ("arbitrary",)),
        name="residual_layer_norm",
    )(x, *h_args, g.reshape(1, d), b.reshape(1, d), *r_args)


SUBLANES = 8


def _shift_mix_body(x_ref, halo_ref, shift_ref, mu_ref, o_ref, *, tm, tp, ts, prompt_tiles):
    i = pl.program_id(0)
    x = x_ref[...]
    row = lax.broadcasted_iota(jnp.int32, (tm, 1), 0)
    prev = jnp.where(row == 0, halo_ref[SUBLANES - 1:SUBLANES, :], pltpu.roll(x, 1, 0))
    prompt_start = jnp.logical_and(row == 0, (i * tm) % tp == 0)
    prev_prompt = jnp.where(prompt_start, 0.0, prev)
    nseq = tm // ts
    carried = jnp.broadcast_to(shift_ref[...][:, None, :], (nseq, ts, x.shape[1])).reshape(x.shape)
    prev_sample = jnp.where(row % ts == 0, carried, prev)
    xx = jnp.where(i < prompt_tiles, prev_prompt, prev_sample) - x
    for j in range(o_ref.shape[0]):
        o_ref[j] = (x + xx * mu_ref[j:j + 1, :]).astype(BF16)


def shift_mix(x, shift0, mu, *, n_prompt_rows, tp, ts):
    m, d = x.shape
    nmix = mu.shape[0]
    tm = 512
    assert tp % tm == 0 and tm % ts == 0 and n_prompt_rows % tm == 0 and m % tm == 0
    prompt_tiles = n_prompt_rows // tm
    nseq = tm // ts
    assert shift0.shape[0] % nseq == 0
    halo_blocks = tm // SUBLANES
    return pl.pallas_call(
        functools.partial(_shift_mix_body, tm=tm, tp=tp, ts=ts, prompt_tiles=prompt_tiles),
        grid=(m // tm,),
        in_specs=[pl.BlockSpec((tm, d), lambda i: (i, 0)),
                  pl.BlockSpec((SUBLANES, d), lambda i: (jnp.maximum(i * halo_blocks - 1, 0), 0)),
                  pl.BlockSpec((nseq, d), lambda i: (jnp.maximum(i - prompt_tiles, 0), 0)),
                  pl.BlockSpec((nmix, d), lambda i: (0, 0))],
        out_specs=pl.BlockSpec((nmix, tm, d), lambda i: (0, i, 0)),
        out_shape=jax.ShapeDtypeStruct((nmix, m, d), BF16),
        compiler_params=_cparams(("parallel",)),
        name="rwkv_shift_mix",
    )(x, x, shift0, mu)


def _chunk_state_init(c, ncp, cps, state_ref, s0_ref):
    @pl.when(jnp.logical_and(c < ncp, c % cps == 0))
    def _():
        state_ref[...] = jnp.zeros(state_ref.shape, state_ref.dtype)

    @pl.when(c >= ncp)
    def _():
        state_ref[...] = s0_ref[0]


def _dla_body(*refs, heads, dk, dv, mode, ncp, cps):
    o_ref, sp_ref, ss_ref, st_ref = refs[-4:]
    refs = refs[:-4]
    if mode == "gla":
        q_ref, k_ref, v_ref, g_ref, gate_ref, ng_ref, s0_ref = refs
    else:
        q_ref, k_ref, v_ref, cos_ref, sin_ref, gate_ref, ng_ref, s0_ref = refs
    c = pl.program_id(0)
    _chunk_state_init(c, ncp, cps, st_ref, s0_ref)

    t_row = lax.broadcasted_iota(jnp.int32, (CHUNK, CHUNK), 0)
    t_col = lax.broadcasted_iota(jnp.int32, (CHUNK, CHUNK), 1)
    causal = t_row >= t_col
    if mode == "gla":
        lower_ones = causal.astype(F32)
        ones_cols = jnp.ones((CHUNK, LANES), F32)
    else:
        width = heads * dk
        even = (lax.broadcasted_iota(jnp.int32, (CHUNK, width), 1) % 2) == 0
        q_all = q_ref[...]
        k_all = k_ref[...]
        q_sw = jnp.where(even, pltpu.roll(q_all, width - 1, 1), pltpu.roll(q_all, 1, 1))
        k_sw = jnp.where(even, pltpu.roll(k_all, width - 1, 1), pltpu.roll(k_all, 1, 1))
        cos = cos_ref[...]
        sin = sin_ref[...]
        frame = (lax.broadcasted_iota(jnp.int32, (CHUNK, 1), 0) + 1).astype(F32)

    for h in range(heads):
        ks = slice(h * dk, (h + 1) * dk)
        vs = slice(h * dv, (h + 1) * dv)
        v = v_ref[:, vs]
        s_prev = st_ref[h]
        if mode == "gla":
            q = q_ref[:, ks]
            k = k_ref[:, ks]
            g = g_ref[:, ks]
            cum = jnp.dot(lower_ones, g, precision=_HI, preferred_element_type=F32)
            total = cum[CHUNK - 1:CHUNK, :]
            total_col = lax.dot_general(g, ones_cols, (((0,), (0,)), ((), ())), precision=_HI,
                                        preferred_element_type=F32)[:, :1]
            state_decay = jnp.exp(total_col)
            q_dec = q * jnp.exp(cum)
            k_inv = k * jnp.exp(-cum)
            k_tail = k * jnp.exp(total - cum)
        else:
            q = q_all[:, ks] * cos + q_sw[:, ks] * sin
            k = (k_all[:, ks] * cos + k_sw[:, ks] * sin) * (dk ** -0.5)
            log_gamma = math.log1p(-(2.0 ** (-5.0 - h)))
            cum = frame * log_gamma
            total = CHUNK * log_gamma
            state_decay = math.exp(total)
            q_dec = q * jnp.exp(cum)
            k_inv = k * jnp.exp(-cum)
            k_tail = k * jnp.exp(total - cum)
        scores = jnp.where(causal, _bdot_nt(q_dec, k_inv), 0.0)
        o = _bdot(q_dec, s_prev) + _bdot(scores, v)
        k_tail_t = jnp.transpose(k_tail)
        st_ref[h] = state_decay * s_prev + _bdot(k_tail_t, v)
        if mode == "gla":
            o = o * lax.rsqrt(jnp.mean(o * o, -1, keepdims=True) + GLA_NORM_EPS) * ng_ref[...]
        else:
            mu = jnp.mean(o, -1, keepdims=True)
            oc = o - mu
            var = jnp.mean(oc * oc, -1, keepdims=True)
            o = oc * lax.rsqrt(var + RET_GN_EPS) * ng_ref[:, vs]
        if mode == "gla":
            o_ref[:, vs] = (o * gate_ref[:, vs]).astype(BF16)
        else:
            o_ref[:, vs] = (gate_ref[:, vs] * o).astype(BF16)

    @pl.when(jnp.logical_and(c < ncp, c % cps == cps - 1))
    def _():
        sp_ref[0] = st_ref[...]

    @pl.when(c >= ncp)
    def _():
        ss_ref[0] = st_ref[...]


def _seq_state_specs(state_shape, ncp, cps, n_prompt, slot):
    blk = (None, 1) + tuple(state_shape)
    zeros = (0,) * len(state_shape)
    s0_spec = pl.BlockSpec(blk, lambda c: (slot, jnp.maximum(c - ncp, 0)) + zeros)
    sp_spec = pl.BlockSpec(blk, lambda c: (0, jnp.minimum(c // cps, n_prompt - 1)) + zeros)
    ss_spec = pl.BlockSpec(blk, lambda c: (0, jnp.maximum(c - ncp, 0)) + zeros)
    return s0_spec, sp_spec, ss_spec


def _state_out_shapes(n_prompt, n_sample, state_shape):
    return [jax.ShapeDtypeStruct((1, n_prompt) + tuple(state_shape), F32),
            jax.ShapeDtypeStruct((1, n_sample) + tuple(state_shape), F32)]


def decay_attention(q, k, v, extra, gate, norm_g, s0, slot, *, mode, heads, n_prompt, cps):
    nt = q.shape[0]
    dk = q.shape[1] // heads
    dv = v.shape[1] // heads
    n_sample = s0.shape[1]
    ncp = n_prompt * cps
    nchunks = nt // CHUNK
    assert nchunks == ncp + n_sample
    rowk = pl.BlockSpec((CHUNK, heads * dk), lambda c: (c, 0))
    rowv = pl.BlockSpec((CHUNK, heads * dv), lambda c: (c, 0))
    s0_spec, sp_spec, ss_spec = _seq_state_specs((heads, dk, dv), ncp, cps, n_prompt, slot)
    if mode == "gla":
        extra_specs = [rowk]
        ng_spec = pl.BlockSpec((1, dv), lambda c: (0, 0))
        norm_g = norm_g.reshape(1, dv)
    else:
        pos_spec = pl.BlockSpec((CHUNK, dk), lambda c: (jnp.where(c < ncp, c % cps, cps), 0))
        extra_specs = [pos_spec, pos_spec]
        ng_spec = pl.BlockSpec((1, heads * dv), lambda c: (0, 0))
        norm_g = norm_g.reshape(1, heads * dv)
    return pl.pallas_call(
        functools.partial(_dla_body, heads=heads, dk=dk, dv=dv, mode=mode, ncp=ncp, cps=cps),
        grid=(nchunks,),
        in_specs=[rowk, rowk, rowv] + extra_specs + [rowv, ng_spec, s0_spec],
        out_specs=[rowv, sp_spec, ss_spec],
        out_shape=([jax.ShapeDtypeStruct((nt, heads * dv), BF16)]
                   + _state_out_shapes(n_prompt, n_sample, (heads, dk, dv))),
        scratch_shapes=[pltpu.VMEM((heads, dk, dv), F32)],
        compiler_params=_cparams(("arbitrary",)),
        name="decay_attention_" + mode,
    )(q, k, v, *extra, gate, norm_g, s0)


def _rwkv_body(r_ref, k_ref, v_ref, lw_ref, a_ref, g_ref, kk_ref, ka_ref, rk_ref, gng_ref, gnb_ref,
               s0_ref, o_ref, sp_ref, ss_ref, st_ref, *, pairs, ncp, cps):
    c = pl.program_id(0)
    n = RWKV_HEAD
    w2 = 2 * n

    @pl.when(jnp.logical_and(c < ncp, c % cps == 0))
    def _():
        st_ref[...] = jnp.zeros(st_ref.shape, F32)

    @pl.when(c >= ncp)
    def _():
        zero = jnp.zeros((n, n), F32)
        for p in range(pairs):
            top = jnp.concatenate([s0_ref[0, 2 * p], zero], axis=1)
            bot = jnp.concatenate([zero, s0_ref[0, 2 * p + 1]], axis=1)
            st_ref[p] = jnp.concatenate([top, bot], axis=0)

    def paired(ref):
        x = ref[...]
        return jnp.stack([x[:, p * w2:(p + 1) * w2] for p in range(pairs)])

    lane = lax.broadcasted_iota(jnp.int32, (1, 1, w2), 2)
    first = lane < n

    def head_sum(x):
        s_a = jnp.sum(jnp.where(first, x, 0.0), -1, keepdims=True)
        s_b = jnp.sum(jnp.where(first, 0.0, x), -1, keepdims=True)
        return jnp.where(first, s_a, s_b)

    def stacked(x):
        return jnp.concatenate([jnp.where(first, x, 0.0), jnp.where(first, 0.0, x)], axis=1)

    def bmm(a, b):
        return lax.dot_general(a.astype(BF16), b.astype(BF16), (((2,), (1,)), ((0,), (0,))),
                               preferred_element_type=F32)

    def bmm_nt(a, b):
        return lax.dot_general(a.astype(BF16), b.astype(BF16), (((2,), (2,)), ((0,), (0,))),
                               preferred_element_type=F32)

    r = paired(r_ref)
    k_raw = paired(k_ref)
    v = paired(v_ref)
    lw = paired(lw_ref)
    a = paired(a_ref)
    k_k = paired(kk_ref)
    k_a = paired(ka_ref)
    r_k = paired(rk_ref)

    kk = k_raw * k_k
    kk = kk / jnp.maximum(jnp.sqrt(head_sum(kk * kk)), 1e-12)
    k_h = k_raw * (1.0 + (a - 1.0) * k_a)
    a_vec = -kk
    b_vec = kk * a

    t_row = lax.broadcasted_iota(jnp.int32, (CHUNK, CHUNK), 0)
    t_col = lax.broadcasted_iota(jnp.int32, (CHUNK, CHUNK), 1)
    lower_ones = (t_row >= t_col).astype(F32)
    cum = paired_value(jnp.dot(lower_ones, lw_ref[...], precision=_HI, preferred_element_type=F32),
                       pairs, w2)
    total = cum[:, CHUNK - 1:CHUNK, :]
    p_now = jnp.exp(cum)
    p_prev = jnp.exp(cum - lw)
    p_inv = jnp.exp(-cum)
    p_tail = jnp.exp(total - cum)

    lhs = jnp.concatenate([stacked(a_vec * p_prev), stacked(r * p_now)], axis=1)
    rhs = jnp.concatenate([stacked(b_vec * p_inv), stacked(k_h * p_inv)], axis=1)
    sc = bmm_nt(lhs, rhs)
    s_prev = st_ref[...]
    sr = bmm_nt(lhs, s_prev)

    i_row = lax.broadcasted_iota(jnp.int32, (1, w2, w2), 1)
    i_col = lax.broadcasted_iota(jnp.int32, (1, w2, w2), 2)
    same = (i_row // n) == (i_col // n)
    strict = jnp.logical_and(same, (i_col % n) < (i_row % n))
    incl = jnp.logical_and(same, (i_col % n) <= (i_row % n))
    a_ab = jnp.where(strict, sc[:, :w2, :w2], 0.0)
    a_ak = jnp.where(strict, sc[:, :w2, w2:], 0.0)
    a_rb = jnp.where(incl, sc[:, w2:, :w2], 0.0)
    a_rk = jnp.where(incl, sc[:, w2:, w2:], 0.0)
    u0 = sr[:, :w2]
    y0 = sr[:, w2:]
    v_st = stacked(v)

    eye = (i_row == i_col).astype(F32)
    t_inv = eye + a_ab
    power = a_ab
    span = 1
    while 2 * span < CHUNK:
        power = bmm(power, power)
        t_inv = t_inv + bmm(t_inv, power)
        span *= 2

    u_st = bmm(t_inv, u0 + bmm(a_ak, v_st))
    uv = jnp.concatenate([u_st, v_st], axis=1)
    y_st = y0 + bmm(jnp.concatenate([a_rb, a_rk], axis=2), uv)
    y = y_st[:, :n] + y_st[:, n:]

    tails = jnp.concatenate([stacked(b_vec * p_tail), stacked(k_h * p_tail)], axis=1)
    uv_t = jnp.swapaxes(uv, 1, 2)
    s_new = s_prev * jnp.exp(total) + bmm(uv_t, tails)
    st_ref[...] = s_new

    mu = head_sum(y) * (1.0 / n)
    yc = y - mu
    var = head_sum(yc * yc) * (1.0 / n)
    yn = yc * lax.rsqrt(var + RWKV_GN_EPS) * paired(gng_ref) + paired(gnb_ref)
    out = yn + head_sum(r * k_h * r_k) * v
    gate = paired(g_ref)
    for p in range(pairs):
        o_ref[:, p * w2:(p + 1) * w2] = (out[p] * gate[p]).astype(BF16)

    def store_state(dst_ref):
        for p in range(pairs):
            dst_ref[0, 2 * p] = s_new[p, :n, :n]
            dst_ref[0, 2 * p + 1] = s_new[p, n:, n:]

    @pl.when(jnp.logical_and(c < ncp, c % cps == cps - 1))
    def _():
        store_state(sp_ref)

    @pl.when(c >= ncp)
    def _():
        store_state(ss_ref)


def paired_value(x, pairs, w2):
    return jnp.stack([x[:, p * w2:(p + 1) * w2] for p in range(pairs)])


def rwkv7_attention(r, k, v, lw, a, g, k_k, k_a, r_k, gn_g, gn_b, s0, slot, *, n_prompt, cps):
    nt, d = r.shape
    heads = d // RWKV_HEAD
    pairs = heads // 2
    n_sample = s0.shape[1]
    ncp = n_prompt * cps
    assert nt // CHUNK == ncp + n_sample
    row = pl.BlockSpec((CHUNK, d), lambda c: (c, 0))
    vec = pl.BlockSpec((1, d), lambda c: (0, 0))
    s0_spec, sp_spec, ss_spec = _seq_state_specs((heads, RWKV_HEAD, RWKV_HEAD), ncp, cps, n_prompt,
                                                 slot)
    vecs = [u.reshape(1, d) for u in (k_k, k_a, r_k, gn_g, gn_b)]
    return pl.pallas_call(
        functools.partial(_rwkv_body, pairs=pairs, ncp=ncp, cps=cps),
        grid=(nt // CHUNK,),
        in_specs=[row] * 6 + [vec] * 5 + [s0_spec],
        out_specs=[row, sp_spec, ss_spec],
        out_shape=([jax.ShapeDtypeStruct((nt, d), BF16)]
                   + _state_out_shapes(n_prompt, n_sample, (heads, RWKV_HEAD, RWKV_HEAD))),
        scratch_shapes=[pltpu.VMEM((pairs, 2 * RWKV_HEAD, 2 * RWKV_HEAD), F32)],
        compiler_params=_cparams(("arbitrary",)),
        name="rwkv7_attention",
    )(r, k, v, lw, a, g, *vecs, s0)


def _expert_weight_pipeline(be_ref, chg_ref, nxt_ref, hbm_refs, stage_refs, cache_refs, sem_ref,
                            tile):
    j = pl.program_id(0)
    b = pl.program_id(1)
    n_pass = pl.num_programs(0)

    def copies(expert, col_pass):
        col = pl.multiple_of(col_pass * tile, tile)
        return [pltpu.make_async_copy(hbm.at[expert, :, pl.ds(col, tile)], stage, sem_ref.at[i])
                for i, (hbm, stage) in enumerate(zip(hbm_refs, stage_refs))]

    @pl.when(jnp.logical_and(j == 0, b == 0))
    def _():
        for cp in copies(be_ref[0], 0):
            cp.start()

    @pl.when(chg_ref[b] == 1)
    def _():
        for cp in copies(be_ref[b], j):
            cp.wait()
        for stage, cache in zip(stage_refs, cache_refs):
            cache[...] = stage[...].astype(BF16)
        nxt = nxt_ref[b]

        @pl.when(nxt >= 0)
        def _():
            for cp in copies(be_ref[jnp.maximum(nxt, 0)], j):
                cp.start()

        @pl.when(jnp.logical_and(nxt < 0, j + 1 < n_pass))
        def _():
            for cp in copies(be_ref[0], j + 1):
                cp.start()


def _gate_up_body(be_ref, chg_ref, nxt_ref, nu_ref, x_ref, w1_ref, w3_ref, h_ref,
                  w1s_ref, w3s_ref, w1c_ref, w3c_ref, sem_ref, *, tile):
    b = pl.program_id(1)
    _expert_weight_pipeline(be_ref, chg_ref, nxt_ref, (w1_ref, w3_ref), (w1s_ref, w3s_ref),
                            (w1c_ref, w3c_ref), sem_ref, tile)

    @pl.when(b < nu_ref[0])
    def _():
        x = x_ref[...]
        gate = jnp.dot(x, w1c_ref[...], preferred_element_type=F32)
        up = jnp.dot(x, w3c_ref[...], preferred_element_type=F32)
        h_ref[...] = (gate * jax.nn.sigmoid(gate) * up).astype(BF16)

    @pl.when(b >= nu_ref[0])
    def _():
        h_ref[...] = jnp.zeros(h_ref.shape, BF16)


def _down_body(be_ref, chg_ref, nxt_ref, nu_ref, h_ref, w2_ref, y_ref, w2s_ref, w2c_ref, sem_ref, *,
               tile):
    b = pl.program_id(1)
    _expert_weight_pipeline(be_ref, chg_ref, nxt_ref, (w2_ref,), (w2s_ref,), (w2c_ref,), sem_ref,
                            tile)

    @pl.when(b < nu_ref[0])
    def _():
        y_ref[...] = jnp.dot(h_ref[...], w2c_ref[...],
                             preferred_element_type=F32).astype(y_ref.dtype)

    @pl.when(b >= nu_ref[0])
    def _():
        y_ref[...] = jnp.zeros(y_ref.shape, y_ref.dtype)


def _block_meta(block_expert, n_used):
    be = block_expert.astype(jnp.int32)
    nb = be.shape[0]
    changed = jnp.concatenate([jnp.ones((1,), jnp.int32),
                               (be[1:] != be[:-1]).astype(jnp.int32)])
    idx = jnp.arange(nb, dtype=jnp.int32)
    opener = jnp.where(changed == 1, idx, nb)
    after = lax.cummin(jnp.concatenate([opener[1:], jnp.full((1,), nb, jnp.int32)]), reverse=True)
    nxt = jnp.where(after >= nb, -1, after).astype(jnp.int32)
    return be, changed, nxt, jnp.reshape(n_used, (1,)).astype(jnp.int32)


def swiglu_gate_up(xs, block_expert, n_used, w1, w3, block, tile_f=(512, 256, 128)):
    rows, d = xs.shape
    f = w1.shape[-1]
    tf = _pick(f, tile_f)
    hbm = pl.BlockSpec(memory_space=pl.ANY)
    return pl.pallas_call(
        functools.partial(_gate_up_body, tile=tf),
        grid_spec=pltpu.PrefetchScalarGridSpec(
            num_scalar_prefetch=4,
            grid=(f // tf, rows // block),
            in_specs=[pl.BlockSpec((block, d), lambda j, b, *_: (b, 0)), hbm, hbm],
            out_specs=pl.BlockSpec((block, tf), lambda j, b, *_: (b, j)),
            scratch_shapes=[pltpu.VMEM((d, tf), F32), pltpu.VMEM((d, tf), F32),
                            pltpu.VMEM((d, tf), BF16), pltpu.VMEM((d, tf), BF16),
                            pltpu.SemaphoreType.DMA((2,))]),
        out_shape=jax.ShapeDtypeStruct((rows, f), BF16),
        compiler_params=_cparams(("arbitrary", "arbitrary")),
        name="swiglu_gate_up",
    )(*_block_meta(block_expert, n_used), xs, w1, w3)


def swiglu_down(h, block_expert, n_used, w2, block, out_dtype=F32):
    rows, f = h.shape
    d = w2.shape[-1]
    tn = _pick(d, (512, 256, 128))
    return pl.pallas_call(
        functools.partial(_down_body, tile=tn),
        grid_spec=pltpu.PrefetchScalarGridSpec(
            num_scalar_prefetch=4,
            grid=(d // tn, rows // block),
            in_specs=[pl.BlockSpec((block, f), lambda j, b, *_: (b, 0)),
                      pl.BlockSpec(memory_space=pl.ANY)],
            out_specs=pl.BlockSpec((block, tn), lambda j, b, *_: (b, j)),
            scratch_shapes=[pltpu.VMEM((f, tn), F32), pltpu.VMEM((f, tn), BF16),
                            pltpu.SemaphoreType.DMA((1,))]),
        out_shape=jax.ShapeDtypeStruct((rows, d), out_dtype),
        compiler_params=_cparams(("arbitrary", "arbitrary")),
        name="swiglu_down",
    )(*_block_meta(block_expert, n_used), h, w2)


def _pad_cols(w, mult=LANES):
    pad = (-w.shape[-1]) % mult
    return jnp.pad(w, [(0, 0)] * (w.ndim - 1) + [(0, pad)]) if pad else w


def _pad_rows(w, mult=LANES):
    pad = (-w.shape[-2]) % mult
    return jnp.pad(w, [(0, 0)] * (w.ndim - 2) + [(0, pad), (0, 0)]) if pad else w


DENSE_GATE_UP_BLOCK = 1024
DENSE_DOWN_BLOCK = 512
MOE_GATE_UP_TILE_F = (1024, 512, 256, 128)


def dense_swiglu(xb, w1, w3, w2, slot):
    rows = xb.shape[0]
    assert rows % DENSE_GATE_UP_BLOCK == 0 and rows % DENSE_DOWN_BLOCK == 0
    nb_a = rows // DENSE_GATE_UP_BLOCK
    nb_b = rows // DENSE_DOWN_BLOCK
    h = swiglu_gate_up(xb, jnp.full((nb_a,), slot, jnp.int32), jnp.int32(nb_a), w1, w3,
                       DENSE_GATE_UP_BLOCK)
    return swiglu_down(h, jnp.full((nb_b,), slot, jnp.int32), jnp.int32(nb_b), w2, DENSE_DOWN_BLOCK)


def moe_swiglu(logits, xb, w1, w3, w2, slot):
    n, d = xb.shape
    n_exp = logits.shape[-1]
    top_val, top_idx = lax.top_k(logits, MOE_TOP_K)
    gates = jax.nn.softmax(top_val, axis=-1)
    slots = n * MOE_TOP_K
    flat_e = top_idx.reshape(-1)
    onehot = (flat_e[:, None] == jnp.arange(n_exp, dtype=flat_e.dtype)[None, :]).astype(jnp.int32)
    rank = jnp.sum((jnp.cumsum(onehot, axis=0) - onehot) * onehot, axis=1)
    counts = jnp.sum(onehot, axis=0)
    start = jnp.cumsum(counts) - counts
    padded = (counts + MOE_BLOCK - 1) // MOE_BLOCK * MOE_BLOCK
    pend = jnp.cumsum(padded)
    pstart = pend - padded
    nb = (slots + n_exp * (MOE_BLOCK - 1) + MOE_BLOCK - 1) // MOE_BLOCK
    dest = (pstart[flat_e] + rank).astype(jnp.int32)
    block_start = jnp.arange(nb, dtype=jnp.int32) * MOE_BLOCK
    expert_of_block = jnp.clip(jnp.searchsorted(pend, block_start, side="right"), 0, n_exp - 1)
    order = jnp.argsort(flat_e, stable=True)
    e_row = jnp.repeat(expert_of_block, MOE_BLOCK)
    rank_row = jnp.arange(nb * MOE_BLOCK, dtype=jnp.int32) - pstart[e_row]
    valid = rank_row < counts[e_row]
    slot_row = order[jnp.clip(start[e_row] + rank_row, 0, slots - 1)]
    src_tok = jnp.where(valid, slot_row // MOE_TOP_K, 0).astype(jnp.int32)
    block_expert = expert_of_block.astype(jnp.int32) + slot * n_exp
    n_used = pend[-1] // MOE_BLOCK
    xs = jnp.take(xb, src_tok, axis=0, mode="clip")
    e1 = w1.reshape((-1,) + w1.shape[2:])
    e3 = w3.reshape((-1,) + w3.shape[2:])
    e2 = w2.reshape((-1,) + w2.shape[2:])
    hs = swiglu_gate_up(xs, block_expert, n_used, e1, e3, MOE_BLOCK, tile_f=MOE_GATE_UP_TILE_F)
    ys = swiglu_down(hs, block_expert, n_used, e2, MOE_BLOCK, out_dtype=BF16)
    dest2 = dest.reshape(n, MOE_TOP_K)
    picked = [jnp.take(ys, dest2[:, j], axis=0, mode="clip") for j in range(MOE_TOP_K)]
    return picked, gates


def _rope_tables(dk, cps, past_len):
    half = dk // 2
    inv_freq = 1.0 / (RET_ROPE_BASE ** jnp.linspace(0.0, 1.0, half, dtype=F32))
    pos = jnp.concatenate([jnp.arange(cps * CHUNK, dtype=jnp.int32),
                           past_len + jnp.arange(CHUNK, dtype=jnp.int32)])
    ang = pos.astype(F32)[:, None] * inv_freq[None, :]
    cos = jnp.repeat(jnp.cos(ang), 2, axis=1)
    sin = jnp.stack([-jnp.sin(ang), jnp.sin(ang)], axis=-1).reshape(pos.shape[0], dk)
    return cos, sin


def kernel(x_prompt, x_sample, state_gla, state_rwkv, state_shift, state_ret, ln_g, ln_b,
           gla_wq, gla_wk, gla_wv, gla_wr, gla_wa1, gla_wa2, gla_ba, gla_norm_g, gla_wo,
           rwkv_mu, rwkv_wr, rwkv_wk, rwkv_wv, rwkv_wo, rwkv_w0, rwkv_w1, rwkv_w2, rwkv_a0, rwkv_a1,
           rwkv_a2, rwkv_g1, rwkv_g2, rwkv_k_k, rwkv_k_a, rwkv_r_k, rwkv_gn_g, rwkv_gn_b,
           ret_wq, ret_wk, ret_wv, ret_wg, ret_gn_g, ret_wo,
           ffn_w1, ffn_w3, ffn_w2, moe_router, moe_w1, moe_w3, moe_w2):
    bp, tp, d = x_prompt.shape
    bs, ts, _ = x_sample.shape
    assert ts == CHUNK and tp % CHUNK == 0
    depth = ln_g.shape[0]
    alpha = (2.0 * depth) ** 0.25
    cps = tp // CHUNK
    past_len = tp
    n_prompt_rows = bp * tp
    seq = dict(n_prompt=bp, cps=cps)

    gla_heads = state_gla.shape[2]
    gla_dk = state_gla.shape[3]
    ret_heads = state_ret.shape[2]
    ret_dk = state_ret.shape[3]

    xf = jnp.concatenate([x_prompt.reshape(-1, d), x_sample.reshape(-1, d)], axis=0)
    xb = xf.astype(BF16)

    gla_states, rwkv_states, ret_states = [], [], []
    new_shift_p, new_shift_s = [], []
    last_rows = jnp.concatenate([jnp.arange(bp, dtype=jnp.int32) * tp + (tp - 1),
                                 n_prompt_rows + jnp.arange(bs, dtype=jnp.int32) * ts + (ts - 1)])
    for i in range(depth):
        kind, slot = i % 3, i // 3
        if kind == 0:
            q = matmul(xb, gla_wq, slot, scale=gla_dk ** -0.5)
            k = matmul(xb, gla_wk, slot)
            v = matmul(xb, gla_wv, slot)
            gate = matmul(xb, gla_wr, slot, act="silu")
            low = matmul(xb, _pad_cols(gla_wa1), slot, out_dtype=BF16)
            log_alpha = matmul(low, _pad_rows(gla_wa2), slot, act="gla_gate", bias=gla_ba[slot])
            o, *states = decay_attention(q, k, v, (log_alpha,), gate, gla_norm_g[slot], state_gla,
                                         slot, mode="gla", heads=gla_heads, **seq)
            gla_states.append(states)
            h = matmul(o, gla_wo, slot)
        elif kind == 1:
            mixes = shift_mix(xf, state_shift[slot], rwkv_mu[slot], n_prompt_rows=n_prompt_rows,
                              tp=tp, ts=ts)
            i_r, i_w, i_k, i_v, i_a, i_g = range(6)
            r = matmul(mixes, rwkv_wr, slot, x_slot=i_r)
            k = matmul(mixes, rwkv_wk, slot, x_slot=i_k)
            v = matmul(mixes, rwkv_wv, slot, x_slot=i_v)
            w_mid = matmul(mixes, _pad_cols(rwkv_w1), slot, x_slot=i_w, act="tanh", out_dtype=BF16)
            log_decay = matmul(w_mid, _pad_rows(rwkv_w2), slot, act="rwkv_decay", bias=rwkv_w0[slot])
            a_mid = matmul(mixes, _pad_cols(rwkv_a1), slot, x_slot=i_a, out_dtype=BF16)
            a = matmul(a_mid, _pad_rows(rwkv_a2), slot, act="sigmoid", bias=rwkv_a0[slot])
            g_mid = matmul(mixes, _pad_cols(rwkv_g1), slot, x_slot=i_g, act="sigmoid", out_dtype=BF16)
            g = matmul(g_mid, _pad_rows(rwkv_g2), slot)
            o, *states = rwkv7_attention(r, k, v, log_decay, a, g, rwkv_k_k[slot], rwkv_k_a[slot],
                                         rwkv_r_k[slot].reshape(-1), rwkv_gn_g[slot],
                                         rwkv_gn_b[slot], state_rwkv, slot, **seq)
            rwkv_states.append(states)
            h = matmul(o, rwkv_wo, slot)
            ends = jnp.take(xf, last_rows, axis=0, mode="clip")
            new_shift_p.append(ends[:bp])
            new_shift_s.append(ends[bp:])
        else:
            q = matmul(xb, ret_wq, slot)
            k = matmul(xb, ret_wk, slot)
            v = matmul(xb, ret_wv, slot)
            gate = matmul(xb, ret_wg, slot, act="silu")
            cos, sin = _rope_tables(ret_dk, cps, past_len)
            o, *states = decay_attention(q, k, v, (cos, sin), gate, ret_gn_g[slot], state_ret, slot,
                                         mode="ret", heads=ret_heads, **seq)
            ret_states.append(states)
            h = matmul(o, ret_wo, slot)
        fslot = i // 2
        is_moe = i % 2 == 1
        router_w = _pad_cols(moe_router[fslot]) if is_moe else None
        xf, xb, *logits = residual_layer_norm(xf, h, ln_g[i, 0], ln_b[i, 0], alpha, router_w=router_w)
        split = n_prompt_rows if i == depth - 1 else None
        if is_moe:
            h, gates = moe_swiglu(logits[0][:, :moe_router.shape[-1]], xb, moe_w1, moe_w3, moe_w2, fslot)
        else:
            h, gates = dense_swiglu(xb, ffn_w1, ffn_w3, ffn_w2, fslot), None
        out_a, out_b = residual_layer_norm(xf, h, ln_g[i, 1], ln_b[i, 1], alpha, gates=gates,
                                           split_rows=split)
        if split is None:
            xf, xb = out_a, out_b
        else:
            y_prompt, y_sample = out_a.reshape(bp, tp, d), out_b.reshape(bs, ts, d)

    def stacked(per_layer, which):
        parts = [states[which] for states in per_layer]
        return parts[0] if len(parts) == 1 else jnp.concatenate(parts, axis=0)

    return (y_prompt, y_sample,
            stacked(gla_states, 0), stacked(rwkv_states, 0), jnp.stack(new_shift_p),
            stacked(ret_states, 0),
            stacked(gla_states, 1), stacked(rwkv_states, 1), jnp.stack(new_shift_s),
            stacked(ret_states, 1))
```

```python
import functools
import math

import jax
import jax.numpy as jnp
from jax import lax
from jax.experimental import pallas as pl
from jax.experimental.pallas import tpu as pltpu

F32 = jnp.float32
BF16 = jnp.bfloat16

CHUNK = 64
LANES = 128
VMEM_LIMIT_BYTES = 56 * 1024 * 1024

LN_EPS = 1e-5
GLA_TAU = 16.0
GLA_NORM_EPS = 1e-5
RWKV_HEAD = 64
RWKV_GN_EPS = 64e-5
RET_ROPE_BASE = 10000.0
RET_GN_EPS = 1e-5
MOE_TOP_K = 2
MOE_BLOCK = 512

_HI = lax.Precision.HIGHEST


def _cparams(sem):
    return pltpu.CompilerParams(dimension_semantics=sem, vmem_limit_bytes=VMEM_LIMIT_BYTES)


def _bdot(a, b):
    return jnp.dot(a.astype(BF16), b.astype(BF16), preferred_element_type=F32)


def _bdot_nt(a, b):
    return lax.dot_general(a.astype(BF16), b.astype(BF16), (((1,), (1,)), ((), ())),
                           preferred_element_type=F32)


def _chunk_prefix_sum(x):
    t_row = lax.broadcasted_iota(jnp.int32, (CHUNK, 3 * CHUNK), 0)
    t_col = lax.broadcasted_iota(jnp.int32, (CHUNK, 3 * CHUNK), 1) % CHUNK
    lower_ones = (t_row >= t_col).astype(BF16)
    hi = x.astype(BF16)
    rest = x - hi.astype(F32)
    mid = rest.astype(BF16)
    lo = (rest - mid.astype(F32)).astype(BF16)
    return jnp.dot(lower_ones, jnp.concatenate([hi, mid, lo], axis=0), preferred_element_type=F32)


def _log_sigmoid(z):
    return -(jnp.maximum(-z, 0.0) + jnp.log1p(jnp.exp(-jnp.abs(z))))


def _act(name, z):
    if name == "none":
        return z
    if name == "silu":
        return z * jax.nn.sigmoid(z)
    if name == "sigmoid":
        return jax.nn.sigmoid(z)
    if name == "tanh":
        return jnp.tanh(z)
    if name == "gla_gate":
        return _log_sigmoid(z) / GLA_TAU
    if name == "rwkv_decay":
        return -jnp.exp(_log_sigmoid(z) - 0.5)
    raise ValueError(name)


def _mm_body(*refs, act, has_bias, scale):
    if has_bias:
        x_ref, w_ref, b_ref, o_ref, wc_ref = refs
    else:
        x_ref, w_ref, o_ref, wc_ref = refs
        b_ref = None

    @pl.when(pl.program_id(1) == 0)
    def _():
        wc_ref[...] = w_ref[...].astype(BF16)

    acc = jnp.dot(x_ref[...], wc_ref[...], preferred_element_type=F32)
    if scale != 1.0:
        acc = acc * scale
    if has_bias:
        acc = acc + b_ref[...]
    o_ref[...] = _act(act, acc).astype(o_ref.dtype)


def _pick(n, pref):
    for t in pref:
        if n % t == 0:
            return t
    return n


MM_WEIGHT_TILE_ELEMS = 2 * 1024 * 1024


def matmul(x, w, slot=0, *, x_slot=None, act="none", bias=None, scale=1.0, out_dtype=F32):
    m, kdim = x.shape[-2:]
    _, kw, n = w.shape
    assert kw == kdim, (w.shape, x.shape)
    tm = _pick(m, (1024, 512, 256, 128, 64, 32, 16, 8))
    tn = _pick(n, tuple(t for t in (1024, 512, 256, 128) if t * kdim <= MM_WEIGHT_TILE_ELEMS))
    if x_slot is None:
        x_spec = pl.BlockSpec((tm, kdim), lambda j, i: (i, 0))
    else:
        x_spec = pl.BlockSpec((None, tm, kdim), lambda j, i: (x_slot, i, 0))
    in_specs = [x_spec, pl.BlockSpec((None, kdim, tn), lambda j, i: (slot, 0, j))]
    args = [x, w]
    if bias is not None:
        in_specs.append(pl.BlockSpec((1, tn), lambda j, i: (0, j)))
        args.append(bias.reshape(1, n).astype(F32))
    return pl.pallas_call(
        functools.partial(_mm_body, act=act, has_bias=bias is not None, scale=scale),
        grid=(n // tn, m // tm),
        in_specs=in_specs,
        out_specs=pl.BlockSpec((tm, tn), lambda j, i: (i, j)),
        out_shape=jax.ShapeDtypeStruct((m, n), out_dtype),
        scratch_shapes=[pltpu.VMEM((kdim, tn), BF16)],
        compiler_params=_cparams(("arbitrary", "arbitrary")),
        name="matmul_" + act,
    )(*args)


def _layer_norm_rows(z, g_ref, b_ref):
    mu = jnp.mean(z, -1, keepdims=True)
    zc = z - mu
    var = jnp.mean(zc * zc, -1, keepdims=True)
    return zc * lax.rsqrt(var + LN_EPS) * g_ref[...] + b_ref[...]


def _store_rows(y, oa_ref, ob_ref, head_tiles):
    if head_tiles is None:
        oa_ref[...] = y
        ob_ref[...] = y.astype(BF16)
    else:
        i = pl.program_id(0)

        @pl.when(i < head_tiles)
        def _():
            oa_ref[...] = y

        @pl.when(i >= head_tiles)
        def _():
            ob_ref[...] = y


def _ln_body(*refs, alpha, routed):
    if routed:
        x_ref, h_ref, g_ref, b_ref, router_ref, oa_ref, ob_ref, logit_ref = refs
    else:
        x_ref, h_ref, g_ref, b_ref, oa_ref, ob_ref = refs
    y = _layer_norm_rows(alpha * x_ref[...] + h_ref[...], g_ref, b_ref)
    if routed:
        logit_ref[...] = jnp.dot(y, router_ref[...], precision=_HI, preferred_element_type=F32)
    _store_rows(y, oa_ref, ob_ref, None)


LN_ROWS = 512


def _ln_out_specs(m, d, tm, split_rows):
    row = pl.BlockSpec((tm, d), lambda i, *_: (i, 0))
    if split_rows is None:
        return None, [row, row], [jax.ShapeDtypeStruct((m, d), F32),
                                  jax.ShapeDtypeStruct((m, d), BF16)]
    assert split_rows % tm == 0 and 0 < split_rows < m
    head_tiles = split_rows // tm
    specs = [pl.BlockSpec((tm, d), lambda i, *_: (jnp.minimum(i, head_tiles - 1), 0)),
             pl.BlockSpec((tm, d), lambda i, *_: (jnp.maximum(i - head_tiles, 0), 0))]
    shapes = [jax.ShapeDtypeStruct((split_rows, d), F32),
              jax.ShapeDtypeStruct((m - split_rows, d), F32)]
    return head_tiles, specs, shapes


def residual_layer_norm(x, h, g, b, alpha, router_w=None):
    m, d = x.shape
    tm = _pick(m, (LN_ROWS, 256, 128, 64, 32, 16, 8))
    row = pl.BlockSpec((tm, d), lambda i: (i, 0))
    vec = pl.BlockSpec((1, d), lambda i: (0, 0))
    _, out_specs, out_shape = _ln_out_specs(m, d, tm, None)
    r_specs, r_args = [], []
    if router_w is not None:
        n_logit = router_w.shape[1]
        r_specs, r_args = [pl.BlockSpec((d, n_logit), lambda i: (0, 0))], [router_w]
        out_specs = out_specs + [pl.BlockSpec((tm, n_logit), lambda i: (i, 0))]
        out_shape = out_shape + [jax.ShapeDtypeStruct((m, n_logit), F32)]
    return pl.pallas_call(
        functools.partial(_ln_body, alpha=alpha, routed=router_w is not None),
        grid=(m // tm,),
        in_specs=[row, row, vec, vec] + r_specs,
        out_specs=out_specs,
        out_shape=out_shape,
        compiler_params=_cparams(("parallel",)),
        name="residual_layer_norm",
    )(x, h, g.reshape(1, d), b.reshape(1, d), *r_args)


def _combine_ln_body(dest_ref, x_ref, ys_ref, gate_ref, g_ref, b_ref, oa_ref, ob_ref, buf_ref, sem_ref,
                     *, alpha, head_tiles, tm, top_k):
    i = pl.program_id(0)
    n_tiles = pl.num_programs(0)

    def issue(tile, slot):
        base = tile * (tm * top_k)

        def body(r, carry):
            for k in range(top_k):
                row = dest_ref[base + r * top_k + k]
                pltpu.make_async_copy(ys_ref.at[pl.ds(row, 1)], buf_ref.at[slot, k, pl.ds(r, 1)],
                                      sem_ref.at[slot]).start()
            return carry

        lax.fori_loop(0, tm, body, 0, unroll=8)

    @pl.when(i == 0)
    def _():
        issue(0, 0)

    @pl.when(i + 1 < n_tiles)
    def _():
        issue(i + 1, (i + 1) % 2)

    slot = i % 2
    for k in range(top_k):
        pltpu.make_async_copy(ys_ref.at[pl.ds(0, tm)], buf_ref.at[slot, k], sem_ref.at[slot]).wait()
    h = buf_ref[slot, 0] * gate_ref[:, 0:1]
    for k in range(1, top_k):
        h = h + buf_ref[slot, k] * gate_ref[:, k:k + 1]
    y = _layer_norm_rows(alpha * x_ref[...] + h, g_ref, b_ref)
    _store_rows(y, oa_ref, ob_ref, head_tiles)


def moe_combine_layer_norm(x, ys, dest, gates, g, b, alpha, split_rows=None):
    m, d = x.shape
    top_k = dest.shape[1]
    tm = LN_ROWS
    assert m % tm == 0 and ys.dtype == F32
    head_tiles, out_specs, out_shape = _ln_out_specs(m, d, tm, split_rows)
    row = pl.BlockSpec((tm, d), lambda i, *_: (i, 0))
    vec = pl.BlockSpec((1, d), lambda i, *_: (0, 0))
    return pl.pallas_call(
        functools.partial(_combine_ln_body, alpha=alpha, head_tiles=head_tiles, tm=tm, top_k=top_k),
        grid_spec=pltpu.PrefetchScalarGridSpec(
            num_scalar_prefetch=1,
            grid=(m // tm,),
            in_specs=[row, pl.BlockSpec(memory_space=pl.ANY),
                      pl.BlockSpec((tm, top_k), lambda i, *_: (i, 0)), vec, vec],
            out_specs=out_specs,
            scratch_shapes=[pltpu.VMEM((2, top_k, tm, d), F32), pltpu.SemaphoreType.DMA((2,))]),
        out_shape=out_shape,
        compiler_params=_cparams(("arbitrary",)),
        name="moe_combine_layer_norm",
    )(dest.reshape(-1).astype(jnp.int32), x, ys, gates, g.reshape(1, d), b.reshape(1, d))


SUBLANES = 8


def _shift_mix_body(x_ref, halo_ref, shift_ref, mu_ref, o_ref, *, tm, tp, ts, prompt_tiles):
    i = pl.program_id(0)
    x = x_ref[...]
    row = lax.broadcasted_iota(jnp.int32, (tm, 1), 0)
    prev = jnp.where(row == 0, halo_ref[SUBLANES - 1:SUBLANES, :], pltpu.roll(x, 1, 0))
    prompt_start = jnp.logical_and(row == 0, (i * tm) % tp == 0)
    prev_prompt = jnp.where(prompt_start, 0.0, prev)
    nseq = tm // ts
    carried = jnp.broadcast_to(shift_ref[...][:, None, :], (nseq, ts, x.shape[1])).reshape(x.shape)
    prev_sample = jnp.where(row % ts == 0, carried, prev)
    xx = jnp.where(i < prompt_tiles, prev_prompt, prev_sample) - x
    for j in range(o_ref.shape[0]):
        o_ref[j] = (x + xx * mu_ref[j:j + 1, :]).astype(BF16)


def shift_mix(x, shift0, mu, *, n_prompt_rows, tp, ts):
    m, d = x.shape
    nmix = mu.shape[0]
    tm = 512
    assert tp % tm == 0 and tm % ts == 0 and n_prompt_rows % tm == 0 and m % tm == 0
    prompt_tiles = n_prompt_rows // tm
    nseq = tm // ts
    assert shift0.shape[0] % nseq == 0
    halo_blocks = tm // SUBLANES
    return pl.pallas_call(
        functools.partial(_shift_mix_body, tm=tm, tp=tp, ts=ts, prompt_tiles=prompt_tiles),
        grid=(m // tm,),
        in_specs=[pl.BlockSpec((tm, d), lambda i: (i, 0)),
                  pl.BlockSpec((SUBLANES, d), lambda i: (jnp.maximum(i * halo_blocks - 1, 0), 0)),
                  pl.BlockSpec((nseq, d), lambda i: (jnp.maximum(i - prompt_tiles, 0), 0)),
                  pl.BlockSpec((nmix, d), lambda i: (0, 0))],
        out_specs=pl.BlockSpec((nmix, tm, d), lambda i: (0, i, 0)),
        out_shape=jax.ShapeDtypeStruct((nmix, m, d), BF16),
        compiler_params=_cparams(("parallel",)),
        name="rwkv_shift_mix",
    )(x, x, shift0, mu)


def _chunk_state_init(c, ncp, cps, state_ref, s0_ref):
    @pl.when(jnp.logical_and(c < ncp, c % cps == 0))
    def _():
        state_ref[...] = jnp.zeros(state_ref.shape, state_ref.dtype)

    @pl.when(c >= ncp)
    def _():
        state_ref[...] = s0_ref[0]


def _dla_body(*refs, heads, dk, dv, mode, ncp, cps):
    o_ref, sp_ref, ss_ref, st_ref = refs[-4:]
    refs = refs[:-4]
    if mode == "gla":
        q_ref, k_ref, v_ref, g_ref, gate_ref, ng_ref, s0_ref = refs
    else:
        q_ref, k_ref, v_ref, cos_ref, sin_ref, gate_ref, ng_ref, s0_ref = refs
    c = pl.program_id(0)
    _chunk_state_init(c, ncp, cps, st_ref, s0_ref)

    t_row = lax.broadcasted_iota(jnp.int32, (CHUNK, CHUNK), 0)
    t_col = lax.broadcasted_iota(jnp.int32, (CHUNK, CHUNK), 1)
    causal = t_row >= t_col
    if mode == "gla":
        cum_all = _chunk_prefix_sum(g_ref[...])
    else:
        width = heads * dk
        even = (lax.broadcasted_iota(jnp.int32, (CHUNK, width), 1) % 2) == 0
        q_all = q_ref[...]
        k_all = k_ref[...]
        q_sw = jnp.where(even, pltpu.roll(q_all, width - 1, 1), pltpu.roll(q_all, 1, 1))
        k_sw = jnp.where(even, pltpu.roll(k_all, width - 1, 1), pltpu.roll(k_all, 1, 1))
        cos = cos_ref[...]
        sin = sin_ref[...]
        frame = (lax.broadcasted_iota(jnp.int32, (CHUNK, 1), 0) + 1).astype(F32)

    for h in range(heads):
        ks = slice(h * dk, (h + 1) * dk)
        vs = slice(h * dv, (h + 1) * dv)
        v = v_ref[:, vs]
        s_prev = st_ref[h]
        if mode == "gla":
            q = q_ref[:, ks]
            k = k_ref[:, ks]
            cum = cum_all[:, ks]
            total = cum[CHUNK - 1:CHUNK, :]
            total_col = jnp.transpose(jnp.broadcast_to(total, (LANES, dk)))[:, :1]
            state_decay = jnp.exp(total_col)
            q_dec = q * jnp.exp(cum)
            k_inv = k * jnp.exp(-cum)
            k_tail = k * jnp.exp(total - cum)
        else:
            q = q_all[:, ks] * cos + q_sw[:, ks] * sin
            k = (k_all[:, ks] * cos + k_sw[:, ks] * sin) * (dk ** -0.5)
            log_gamma = math.log1p(-(2.0 ** (-5.0 - h)))
            cum = frame * log_gamma
            total = CHUNK * log_gamma
            state_decay = math.exp(total)
            q_dec = q * jnp.exp(cum)
            k_inv = k * jnp.exp(-cum)
            k_tail = k * jnp.exp(total - cum)
        scores = jnp.where(causal, _bdot_nt(q_dec, k_inv), 0.0)
        o = _bdot(q_dec, s_prev) + _bdot(scores, v)
        k_tail_t = jnp.transpose(k_tail)
        st_ref[h] = state_decay * s_prev + _bdot(k_tail_t, v)
        if mode == "gla":
            o = o * lax.rsqrt(jnp.mean(o * o, -1, keepdims=True) + GLA_NORM_EPS) * ng_ref[...]
        else:
            mu = jnp.mean(o, -1, keepdims=True)
            oc = o - mu
            var = jnp.mean(oc * oc, -1, keepdims=True)
            o = oc * lax.rsqrt(var + RET_GN_EPS) * ng_ref[:, vs]
        if mode == "gla":
            o_ref[:, vs] = (o * gate_ref[:, vs]).astype(BF16)
        else:
            o_ref[:, vs] = (gate_ref[:, vs] * o).astype(BF16)

    @pl.when(jnp.logical_and(c < ncp, c % cps == cps - 1))
    def _():
        sp_ref[0] = st_ref[...]

    @pl.when(c >= ncp)
    def _():
        ss_ref[0] = st_ref[...]


def _seq_state_specs(state_shape, ncp, cps, n_prompt, slot):
    blk = (None, 1) + tuple(state_shape)
    zeros = (0,) * len(state_shape)
    s0_spec = pl.BlockSpec(blk, lambda c: (slot, jnp.maximum(c - ncp, 0)) + zeros)
    sp_spec = pl.BlockSpec(blk, lambda c: (0, jnp.minimum(c // cps, n_prompt - 1)) + zeros)
    ss_spec = pl.BlockSpec(blk, lambda c: (0, jnp.maximum(c - ncp, 0)) + zeros)
    return s0_spec, sp_spec, ss_spec


def _state_out_shapes(n_prompt, n_sample, state_shape):
    return [jax.ShapeDtypeStruct((1, n_prompt) + tuple(state_shape), F32),
            jax.ShapeDtypeStruct((1, n_sample) + tuple(state_shape), F32)]


def decay_attention(q, k, v, extra, gate, norm_g, s0, slot, *, mode, heads, n_prompt, cps):
    nt = q.shape[0]
    dk = q.shape[1] // heads
    dv = v.shape[1] // heads
    n_sample = s0.shape[1]
    ncp = n_prompt * cps
    nchunks = nt // CHUNK
    assert nchunks == ncp + n_sample
    rowk = pl.BlockSpec((CHUNK, heads * dk), lambda c: (c, 0))
    rowv = pl.BlockSpec((CHUNK, heads * dv), lambda c: (c, 0))
    s0_spec, sp_spec, ss_spec = _seq_state_specs((heads, dk, dv), ncp, cps, n_prompt, slot)
    if mode == "gla":
        extra_specs = [rowk]
        ng_spec = pl.BlockSpec((1, dv), lambda c: (0, 0))
        norm_g = norm_g.reshape(1, dv)
    else:
        pos_spec = pl.BlockSpec((CHUNK, dk), lambda c: (jnp.where(c < ncp, c % cps, cps), 0))
        extra_specs = [pos_spec, pos_spec]
        ng_spec = pl.BlockSpec((1, heads * dv), lambda c: (0, 0))
        norm_g = norm_g.reshape(1, heads * dv)
    return pl.pallas_call(
        functools.partial(_dla_body, heads=heads, dk=dk, dv=dv, mode=mode, ncp=ncp, cps=cps),
        grid=(nchunks,),
        in_specs=[rowk, rowk, rowv] + extra_specs + [rowv, ng_spec, s0_spec],
        out_specs=[rowv, sp_spec, ss_spec],
        out_shape=([jax.ShapeDtypeStruct((nt, heads * dv), BF16)]
                   + _state_out_shapes(n_prompt, n_sample, (heads, dk, dv))),
        scratch_shapes=[pltpu.VMEM((heads, dk, dv), F32)],
        compiler_params=_cparams(("arbitrary",)),
        name="decay_attention_" + mode,
    )(q, k, v, *extra, gate, norm_g, s0)


def _rwkv_body(r_ref, k_ref, v_ref, lw_ref, a_ref, g_ref, kk_ref, ka_ref, rk_ref, gng_ref, gnb_ref,
               s0_ref, o_ref, sp_ref, ss_ref, st_ref, *, pairs, ncp, cps):
    c = pl.program_id(0)
    n = RWKV_HEAD
    w2 = 2 * n

    @pl.when(jnp.logical_and(c < ncp, c % cps == 0))
    def _():
        st_ref[...] = jnp.zeros(st_ref.shape, F32)

    @pl.when(c >= ncp)
    def _():
        zero = jnp.zeros((n, n), F32)
        for p in range(pairs):
            top = jnp.concatenate([s0_ref[0, 2 * p], zero], axis=1)
            bot = jnp.concatenate([zero, s0_ref[0, 2 * p + 1]], axis=1)
            st_ref[p] = jnp.concatenate([top, bot], axis=0)

    def paired(ref):
        x = ref[...]
        return jnp.stack([x[:, p * w2:(p + 1) * w2] for p in range(pairs)])

    lane = lax.broadcasted_iota(jnp.int32, (1, 1, w2), 2)
    first = lane < n

    def head_sum(x):
        s_a = jnp.sum(jnp.where(first, x, 0.0), -1, keepdims=True)
        s_b = jnp.sum(jnp.where(first, 0.0, x), -1, keepdims=True)
        return jnp.where(first, s_a, s_b)

    def stacked(x):
        return jnp.concatenate([jnp.where(first, x, 0.0), jnp.where(first, 0.0, x)], axis=1)

    def bmm(a, b):
        return lax.dot_general(a.astype(BF16), b.astype(BF16), (((2,), (1,)), ((0,), (0,))),
                               preferred_element_type=F32)

    def bmm_nt(a, b):
        return lax.dot_general(a.astype(BF16), b.astype(BF16), (((2,), (2,)), ((0,), (0,))),
                               preferred_element_type=F32)

    r = paired(r_ref)
    k_raw = paired(k_ref)
    v = paired(v_ref)
    lw = paired(lw_ref)
    a = paired(a_ref)
    k_k = paired(kk_ref)
    k_a = paired(ka_ref)
    r_k = paired(rk_ref)

    kk = k_raw * k_k
    kk = kk / jnp.maximum(jnp.sqrt(head_sum(kk * kk)), 1e-12)
    k_h = k_raw * (1.0 + (a - 1.0) * k_a)
    a_vec = -kk
    b_vec = kk * a

    cum = paired_value(_chunk_prefix_sum(lw_ref[...]), pairs, w2)
    total = cum[:, CHUNK - 1:CHUNK, :]
    p_now = jnp.exp(cum)
    p_prev = jnp.exp(cum - lw)
    p_inv = jnp.exp(-cum)
    p_tail = jnp.exp(total - cum)

    lhs = jnp.concatenate([stacked(a_vec * p_prev), stacked(r * p_now)], axis=1)
    rhs = jnp.concatenate([stacked(b_vec * p_inv), stacked(k_h * p_inv)], axis=1)
    sc = bmm_nt(lhs, rhs)
    s_prev = st_ref[...]
    sr = bmm_nt(lhs, s_prev)

    i_row = lax.broadcasted_iota(jnp.int32, (1, w2, w2), 1)
    i_col = lax.broadcasted_iota(jnp.int32, (1, w2, w2), 2)
    same = (i_row // n) == (i_col // n)
    strict = jnp.logical_and(same, (i_col % n) < (i_row % n))
    incl = jnp.logical_and(same, (i_col % n) <= (i_row % n))
    a_ab = jnp.where(strict, sc[:, :w2, :w2], 0.0)
    a_ak = jnp.where(strict, sc[:, :w2, w2:], 0.0)
    a_rb = jnp.where(incl, sc[:, w2:, :w2], 0.0)
    a_rk = jnp.where(incl, sc[:, w2:, w2:], 0.0)
    u0 = sr[:, :w2]
    y0 = sr[:, w2:]
    v_st = stacked(v)

    eye = (i_row == i_col).astype(F32)
    t_inv = eye + a_ab
    power = a_ab
    span = 1
    while 2 * span < CHUNK:
        power = bmm(power, power)
        t_inv = t_inv + bmm(t_inv, power)
        span *= 2

    u_st = bmm(t_inv, u0 + bmm(a_ak, v_st))
    uv = jnp.concatenate([u_st, v_st], axis=1)
    y_st = y0 + bmm(jnp.concatenate([a_rb, a_rk], axis=2), uv)
    y = y_st[:, :n] + y_st[:, n:]

    tails = jnp.concatenate([stacked(b_vec * p_tail), stacked(k_h * p_tail)], axis=1)
    uv_t = jnp.swapaxes(uv, 1, 2)
    s_new = s_prev * jnp.exp(total) + bmm(uv_t, tails)
    st_ref[...] = s_new

    mu = head_sum(y) * (1.0 / n)
    yc = y - mu
    var = head_sum(yc * yc) * (1.0 / n)
    yn = yc * lax.rsqrt(var + RWKV_GN_EPS) * paired(gng_ref) + paired(gnb_ref)
    out = yn + head_sum(r * k_h * r_k) * v
    gate = paired(g_ref)
    for p in range(pairs):
        o_ref[:, p * w2:(p + 1) * w2] = (out[p] * gate[p]).astype(BF16)

    def store_state(dst_ref):
        for p in range(pairs):
            dst_ref[0, 2 * p] = s_new[p, :n, :n]
            dst_ref[0, 2 * p + 1] = s_new[p, n:, n:]

    @pl.when(jnp.logical_and(c < ncp, c % cps == cps - 1))
    def _():
        store_state(sp_ref)

    @pl.when(c >= ncp)
    def _():
        store_state(ss_ref)


def paired_value(x, pairs, w2):
    return jnp.stack([x[:, p * w2:(p + 1) * w2] for p in range(pairs)])


def rwkv7_attention(r, k, v, lw, a, g, k_k, k_a, r_k, gn_g, gn_b, s0, slot, *, n_prompt, cps):
    nt, d = r.shape
    heads = d // RWKV_HEAD
    pairs = heads // 2
    n_sample = s0.shape[1]
    ncp = n_prompt * cps
    assert nt // CHUNK == ncp + n_sample
    row = pl.BlockSpec((CHUNK, d), lambda c: (c, 0))
    vec = pl.BlockSpec((1, d), lambda c: (0, 0))
    s0_spec, sp_spec, ss_spec = _seq_state_specs((heads, RWKV_HEAD, RWKV_HEAD), ncp, cps, n_prompt,
                                                 slot)
    vecs = [u.reshape(1, d) for u in (k_k, k_a, r_k, gn_g, gn_b)]
    return pl.pallas_call(
        functools.partial(_rwkv_body, pairs=pairs, ncp=ncp, cps=cps),
        grid=(nt // CHUNK,),
        in_specs=[row] * 6 + [vec] * 5 + [s0_spec],
        out_specs=[row, sp_spec, ss_spec],
        out_shape=([jax.ShapeDtypeStruct((nt, d), BF16)]
                   + _state_out_shapes(n_prompt, n_sample, (heads, RWKV_HEAD, RWKV_HEAD))),
        scratch_shapes=[pltpu.VMEM((pairs, 2 * RWKV_HEAD, 2 * RWKV_HEAD), F32)],
        compiler_params=_cparams(("arbitrary",)),
        name="rwkv7_attention",
    )(r, k, v, lw, a, g, *vecs, s0)


def _expert_weight_pipeline(be_ref, chg_ref, nxt_ref, hbm_refs, stage_refs, cache_refs, sem_ref,
                            tile):
    j = pl.program_id(0)
    b = pl.program_id(1)
    n_pass = pl.num_programs(0)

    def copies(expert, col_pass):
        col = pl.multiple_of(col_pass * tile, tile)
        return [pltpu.make_async_copy(hbm.at[expert, :, pl.ds(col, tile)], stage, sem_ref.at[i])
                for i, (hbm, stage) in enumerate(zip(hbm_refs, stage_refs))]

    @pl.when(jnp.logical_and(j == 0, b == 0))
    def _():
        for cp in copies(be_ref[0], 0):
            cp.start()

    @pl.when(chg_ref[b] == 1)
    def _():
        for cp in copies(be_ref[b], j):
            cp.wait()
        for stage, cache in zip(stage_refs, cache_refs):
            cache[...] = stage[...].astype(BF16)
        nxt = nxt_ref[b]

        @pl.when(nxt >= 0)
        def _():
            for cp in copies(be_ref[jnp.maximum(nxt, 0)], j):
                cp.start()

        @pl.when(jnp.logical_and(nxt < 0, j + 1 < n_pass))
        def _():
            for cp in copies(be_ref[0], j + 1):
                cp.start()


def _gate_up_body(be_ref, chg_ref, nxt_ref, nu_ref, x_ref, w1_ref, w3_ref, h_ref,
                  w1s_ref, w3s_ref, w1c_ref, w3c_ref, sem_ref, *, tile):
    b = pl.program_id(1)
    _expert_weight_pipeline(be_ref, chg_ref, nxt_ref, (w1_ref, w3_ref), (w1s_ref, w3s_ref),
                            (w1c_ref, w3c_ref), sem_ref, tile)

    @pl.when(b < nu_ref[0])
    def _():
        x = x_ref[...]
        gate = jnp.dot(x, w1c_ref[...], preferred_element_type=F32)
        up = jnp.dot(x, w3c_ref[...], preferred_element_type=F32)
        h_ref[...] = (gate * jax.nn.sigmoid(gate) * up).astype(BF16)

    @pl.when(b >= nu_ref[0])
    def _():
        h_ref[...] = jnp.zeros(h_ref.shape, BF16)


def _down_body(be_ref, chg_ref, nxt_ref, nu_ref, h_ref, w2_ref, y_ref, w2s_ref, w2c_ref, sem_ref, *,
               tile):
    b = pl.program_id(1)
    _expert_weight_pipeline(be_ref, chg_ref, nxt_ref, (w2_ref,), (w2s_ref,), (w2c_ref,), sem_ref,
                            tile)

    @pl.when(b < nu_ref[0])
    def _():
        y_ref[...] = jnp.dot(h_ref[...], w2c_ref[...],
                             preferred_element_type=F32).astype(y_ref.dtype)

    @pl.when(b >= nu_ref[0])
    def _():
        y_ref[...] = jnp.zeros(y_ref.shape, y_ref.dtype)


def _block_meta(block_expert, n_used):
    be = block_expert.astype(jnp.int32)
    nb = be.shape[0]
    changed = jnp.concatenate([jnp.ones((1,), jnp.int32),
                               (be[1:] != be[:-1]).astype(jnp.int32)])
    idx = jnp.arange(nb, dtype=jnp.int32)
    opener = jnp.where(changed == 1, idx, nb)
    after = lax.cummin(jnp.concatenate([opener[1:], jnp.full((1,), nb, jnp.int32)]), reverse=True)
    nxt = jnp.where(after >= nb, -1, after).astype(jnp.int32)
    return be, changed, nxt, jnp.reshape(n_used, (1,)).astype(jnp.int32)


def swiglu_gate_up(xs, block_expert, n_used, w1, w3, block, tile_f=(512, 256, 128)):
    rows, d = xs.shape
    f = w1.shape[-1]
    tf = _pick(f, tile_f)
    hbm = pl.BlockSpec(memory_space=pl.ANY)
    return pl.pallas_call(
        functools.partial(_gate_up_body, tile=tf),
        grid_spec=pltpu.PrefetchScalarGridSpec(
            num_scalar_prefetch=4,
            grid=(f // tf, rows // block),
            in_specs=[pl.BlockSpec((block, d), lambda j, b, *_: (b, 0)), hbm, hbm],
            out_specs=pl.BlockSpec((block, tf), lambda j, b, *_: (b, j)),
            scratch_shapes=[pltpu.VMEM((d, tf), F32), pltpu.VMEM((d, tf), F32),
                            pltpu.VMEM((d, tf), BF16), pltpu.VMEM((d, tf), BF16),
                            pltpu.SemaphoreType.DMA((2,))]),
        out_shape=jax.ShapeDtypeStruct((rows, f), BF16),
        compiler_params=_cparams(("arbitrary", "arbitrary")),
        name="swiglu_gate_up",
    )(*_block_meta(block_expert, n_used), xs, w1, w3)


def swiglu_down(h, block_expert, n_used, w2, block, out_dtype=F32):
    rows, f = h.shape
    d = w2.shape[-1]
    tn = _pick(d, (512, 256, 128))
    return pl.pallas_call(
        functools.partial(_down_body, tile=tn),
        grid_spec=pltpu.PrefetchScalarGridSpec(
            num_scalar_prefetch=4,
            grid=(d // tn, rows // block),
            in_specs=[pl.BlockSpec((block, f), lambda j, b, *_: (b, 0)),
                      pl.BlockSpec(memory_space=pl.ANY)],
            out_specs=pl.BlockSpec((block, tn), lambda j, b, *_: (b, j)),
            scratch_shapes=[pltpu.VMEM((f, tn), F32), pltpu.VMEM((f, tn), BF16),
                            pltpu.SemaphoreType.DMA((1,))]),
        out_shape=jax.ShapeDtypeStruct((rows, d), out_dtype),
        compiler_params=_cparams(("arbitrary", "arbitrary")),
        name="swiglu_down",
    )(*_block_meta(block_expert, n_used), h, w2)


def _pad_cols(w, mult=LANES):
    pad = (-w.shape[-1]) % mult
    return jnp.pad(w, [(0, 0)] * (w.ndim - 1) + [(0, pad)]) if pad else w


def _pad_rows(w, mult=LANES):
    pad = (-w.shape[-2]) % mult
    return jnp.pad(w, [(0, 0)] * (w.ndim - 2) + [(0, pad), (0, 0)]) if pad else w


DENSE_GATE_UP_BLOCK = 1024
DENSE_DOWN_BLOCK = 512
MOE_GATE_UP_TILE_F = (1024, 512, 256, 128)


def dense_swiglu(xb, w1, w3, w2, slot):
    rows = xb.shape[0]
    assert rows % DENSE_GATE_UP_BLOCK == 0 and rows % DENSE_DOWN_BLOCK == 0
    nb_a = rows // DENSE_GATE_UP_BLOCK
    nb_b = rows // DENSE_DOWN_BLOCK
    h = swiglu_gate_up(xb, jnp.full((nb_a,), slot, jnp.int32), jnp.int32(nb_a), w1, w3,
                       DENSE_GATE_UP_BLOCK)
    return swiglu_down(h, jnp.full((nb_b,), slot, jnp.int32), jnp.int32(nb_b), w2, DENSE_DOWN_BLOCK)


def moe_swiglu(logits, xb, w1, w3, w2, slot):
    n, d = xb.shape
    n_exp = logits.shape[-1]
    top_val, top_idx = lax.top_k(logits, MOE_TOP_K)
    gates = jax.nn.softmax(top_val, axis=-1)
    slots = n * MOE_TOP_K
    flat_e = top_idx.reshape(-1)
    onehot = (flat_e[:, None] == jnp.arange(n_exp, dtype=flat_e.dtype)[None, :]).astype(jnp.int32)
    rank = jnp.sum((jnp.cumsum(onehot, axis=0) - onehot) * onehot, axis=1)
    counts = jnp.sum(onehot, axis=0)
    start = jnp.cumsum(counts) - counts
    padded = (counts + MOE_BLOCK - 1) // MOE_BLOCK * MOE_BLOCK
    pend = jnp.cumsum(padded)
    pstart = pend - padded
    nb = (slots + n_exp * (MOE_BLOCK - 1) + MOE_BLOCK - 1) // MOE_BLOCK
    dest = (pstart[flat_e] + rank).astype(jnp.int32)
    block_start = jnp.arange(nb, dtype=jnp.int32) * MOE_BLOCK
    expert_of_block = jnp.clip(jnp.searchsorted(pend, block_start, side="right"), 0, n_exp - 1)
    order = jnp.argsort(flat_e, stable=True)
    e_row = jnp.repeat(expert_of_block, MOE_BLOCK)
    rank_row = jnp.arange(nb * MOE_BLOCK, dtype=jnp.int32) - pstart[e_row]
    valid = rank_row < counts[e_row]
    slot_row = order[jnp.clip(start[e_row] + rank_row, 0, slots - 1)]
    src_tok = jnp.where(valid, slot_row // MOE_TOP_K, 0).astype(jnp.int32)
    block_expert = expert_of_block.astype(jnp.int32) + slot * n_exp
    n_used = pend[-1] // MOE_BLOCK
    xs = jnp.take(xb, src_tok, axis=0, mode="clip")
    e1 = w1.reshape((-1,) + w1.shape[2:])
    e3 = w3.reshape((-1,) + w3.shape[2:])
    e2 = w2.reshape((-1,) + w2.shape[2:])
    hs = swiglu_gate_up(xs, block_expert, n_used, e1, e3, MOE_BLOCK, tile_f=MOE_GATE_UP_TILE_F)
    ys = swiglu_down(hs, block_expert, n_used, e2, MOE_BLOCK)
    return ys, dest.reshape(n, MOE_TOP_K), gates


def _rope_tables(dk, cps, past_len):
    half = dk // 2
    inv_freq = 1.0 / (RET_ROPE_BASE ** jnp.linspace(0.0, 1.0, half, dtype=F32))
    pos = jnp.concatenate([jnp.arange(cps * CHUNK, dtype=jnp.int32),
                           past_len + jnp.arange(CHUNK, dtype=jnp.int32)])
    ang = pos.astype(F32)[:, None] * inv_freq[None, :]
    cos = jnp.repeat(jnp.cos(ang), 2, axis=1)
    sin = jnp.stack([-jnp.sin(ang), jnp.sin(ang)], axis=-1).reshape(pos.shape[0], dk)
    return cos, sin


def kernel(x_prompt, x_sample, state_gla, state_rwkv, state_shift, state_ret, ln_g, ln_b,
           gla_wq, gla_wk, gla_wv, gla_wr, gla_wa1, gla_wa2, gla_ba, gla_norm_g, gla_wo,
           rwkv_mu, rwkv_wr, rwkv_wk, rwkv_wv, rwkv_wo, rwkv_w0, rwkv_w1, rwkv_w2, rwkv_a0, rwkv_a1,
           rwkv_a2, rwkv_g1, rwkv_g2, rwkv_k_k, rwkv_k_a, rwkv_r_k, rwkv_gn_g, rwkv_gn_b,
           ret_wq, ret_wk, ret_wv, ret_wg, ret_gn_g, ret_wo,
           ffn_w1, ffn_w3, ffn_w2, moe_router, moe_w1, moe_w3, moe_w2):
    bp, tp, d = x_prompt.shape
    bs, ts, _ = x_sample.shape
    assert ts == CHUNK and tp % CHUNK == 0
    depth = ln_g.shape[0]
    alpha = (2.0 * depth) ** 0.25
    cps = tp // CHUNK
    past_len = tp
    n_prompt_rows = bp * tp
    seq = dict(n_prompt=bp, cps=cps)

    gla_heads = state_gla.shape[2]
    gla_dk = state_gla.shape[3]
    ret_heads = state_ret.shape[2]
    ret_dk = state_ret.shape[3]

    xf = jnp.concatenate([x_prompt.reshape(-1, d), x_sample.reshape(-1, d)], axis=0)
    xb = xf.astype(BF16)

    gla_states, rwkv_states, ret_states = [], [], []
    new_shift_p, new_shift_s = [], []
    last_rows = jnp.concatenate([jnp.arange(bp, dtype=jnp.int32) * tp + (tp - 1),
                                 n_prompt_rows + jnp.arange(bs, dtype=jnp.int32) * ts + (ts - 1)])
    for i in range(depth):
        kind, slot = i % 3, i // 3
        if kind == 0:
            q = matmul(xb, gla_wq, slot, scale=gla_dk ** -0.5)
            k = matmul(xb, gla_wk, slot)
            v = matmul(xb, gla_wv, slot)
            gate = matmul(xb, gla_wr, slot, act="silu")
            low = matmul(xb, _pad_cols(gla_wa1), slot, out_dtype=BF16)
            log_alpha = matmul(low, _pad_rows(gla_wa2), slot, act="gla_gate", bias=gla_ba[slot])
            o, *states = decay_attention(q, k, v, (log_alpha,), gate, gla_norm_g[slot], state_gla,
                                         slot, mode="gla", heads=gla_heads, **seq)
            gla_states.append(states)
            h = matmul(o, gla_wo, slot)
        elif kind == 1:
            mixes = shift_mix(xf, state_shift[slot], rwkv_mu[slot], n_prompt_rows=n_prompt_rows,
                              tp=tp, ts=ts)
            i_r, i_w, i_k, i_v, i_a, i_g = range(6)
            r = matmul(mixes, rwkv_wr, slot, x_slot=i_r)
            k = matmul(mixes, rwkv_wk, slot, x_slot=i_k)
            v = matmul(mixes, rwkv_wv, slot, x_slot=i_v)
            w_mid = matmul(mixes, _pad_cols(rwkv_w1), slot, x_slot=i_w, act="tanh", out_dtype=BF16)
            log_decay = matmul(w_mid, _pad_rows(rwkv_w2), slot, act="rwkv_decay", bias=rwkv_w0[slot])
            a_mid = matmul(mixes, _pad_cols(rwkv_a1), slot, x_slot=i_a, out_dtype=BF16)
            a = matmul(a_mid, _pad_rows(rwkv_a2), slot, act="sigmoid", bias=rwkv_a0[slot])
            g_mid = matmul(mixes, _pad_cols(rwkv_g1), slot, x_slot=i_g, act="sigmoid", out_dtype=BF16)
            g = matmul(g_mid, _pad_rows(rwkv_g2), slot)
            o, *states = rwkv7_attention(r, k, v, log_decay, a, g, rwkv_k_k[slot], rwkv_k_a[slot],
                                         rwkv_r_k[slot].reshape(-1), rwkv_gn_g[slot],
                                         rwkv_gn_b[slot], state_rwkv, slot, **seq)
            rwkv_states.append(states)
            h = matmul(o, rwkv_wo, slot)
            ends = jnp.take(xf, last_rows, axis=0, mode="clip")
            new_shift_p.append(ends[:bp])
            new_shift_s.append(ends[bp:])
        else:
            q = matmul(xb, ret_wq, slot)
            k = matmul(xb, ret_wk, slot)
            v = matmul(xb, ret_wv, slot)
            gate = matmul(xb, ret_wg, slot, act="silu")
            cos, sin = _rope_tables(ret_dk, cps, past_len)
            o, *states = decay_attention(q, k, v, (cos, sin), gate, ret_gn_g[slot], state_ret, slot,
                                         mode="ret", heads=ret_heads, **seq)
            ret_states.append(states)
            h = matmul(o, ret_wo, slot)
        fslot = i // 2
        is_moe = i % 2 == 1
        router_w = _pad_cols(moe_router[fslot]) if is_moe else None
        xf, xb, *logits = residual_layer_norm(xf, h, ln_g[i, 0], ln_b[i, 0], alpha, router_w=router_w)
        last = i == depth - 1
        if is_moe:
            ys, dest, gates = moe_swiglu(logits[0][:, :moe_router.shape[-1]], xb, moe_w1, moe_w3,
                                         moe_w2, fslot)
            out_a, out_b = moe_combine_layer_norm(xf, ys, dest, gates, ln_g[i, 1], ln_b[i, 1], alpha,
                                                  split_rows=n_prompt_rows if last else None)
        else:
            h = dense_swiglu(xb, ffn_w1, ffn_w3, ffn_w2, fslot)
            out_a, out_b = residual_layer_norm(xf, h, ln_g[i, 1], ln_b[i, 1], alpha)
            if last:
                out_a, out_b = out_a[:n_prompt_rows], out_a[n_prompt_rows:]
        if last:
            y_prompt, y_sample = out_a.reshape(bp, tp, d), out_b.reshape(bs, ts, d)
        else:
            xf, xb = out_a, out_b

    def stacked(per_layer, which):
        parts = [states[which] for states in per_layer]
        return parts[0] if len(parts) == 1 else jnp.concatenate(parts, axis=0)

    return (y_prompt, y_sample,
            stacked(gla_states, 0), stacked(rwkv_states, 0), jnp.stack(new_shift_p),
            stacked(ret_states, 0),
            stacked(gla_states, 1), stacked(rwkv_states, 1), jnp.stack(new_shift_s),
            stacked(ret_states, 1))
```

```python
import functools
import math

import jax
import jax.numpy as jnp
from jax import lax
from jax.experimental import pallas as pl
from jax.experimental.pallas import tpu as pltpu

F32 = jnp.float32
BF16 = jnp.bfloat16

CHUNK = 64
LANES = 128
VMEM_LIMIT_BYTES = 56 * 1024 * 1024

LN_EPS = 1e-5
GLA_TAU = 16.0
GLA_NORM_EPS = 1e-5
RWKV_HEAD = 64
RWKV_GN_EPS = 64e-5
RET_ROPE_BASE = 10000.0
RET_GN_EPS = 1e-5
MOE_TOP_K = 2
MOE_BLOCK = 512

_HI = lax.Precision.HIGHEST


def _cparams(sem):
    return pltpu.CompilerParams(dimension_semantics=sem, vmem_limit_bytes=VMEM_LIMIT_BYTES)


def _bdot(a, b):
    return jnp.dot(a.astype(BF16), b.astype(BF16), preferred_element_type=F32)


def _bdot_nt(a, b):
    return lax.dot_general(a.astype(BF16), b.astype(BF16), (((1,), (1,)), ((), ())),
                           preferred_element_type=F32)


def _chunk_prefix_sum(x):
    t_row = lax.broadcasted_iota(jnp.int32, (CHUNK, 3 * CHUNK), 0)
    t_col = lax.broadcasted_iota(jnp.int32, (CHUNK, 3 * CHUNK), 1) % CHUNK
    lower_ones = (t_row >= t_col).astype(BF16)
    hi = x.astype(BF16)
    rest = x - hi.astype(F32)
    mid = rest.astype(BF16)
    lo = (rest - mid.astype(F32)).astype(BF16)
    return jnp.dot(lower_ones, jnp.concatenate([hi, mid, lo], axis=0), preferred_element_type=F32)


def _log_sigmoid(z):
    return -(jnp.maximum(-z, 0.0) + jnp.log1p(jnp.exp(-jnp.abs(z))))


def _act(name, z):
    if name == "none":
        return z
    if name == "silu":
        return z * jax.nn.sigmoid(z)
    if name == "sigmoid":
        return jax.nn.sigmoid(z)
    if name == "tanh":
        return jnp.tanh(z)
    if name == "gla_gate":
        return _log_sigmoid(z) / GLA_TAU
    if name == "rwkv_decay":
        return -jnp.exp(_log_sigmoid(z) - 0.5)
    raise ValueError(name)


def _mm_body(*refs, act, has_bias, scale):
    if has_bias:
        x_ref, w_ref, b_ref, o_ref, wc_ref = refs
    else:
        x_ref, w_ref, o_ref, wc_ref = refs
        b_ref = None

    @pl.when(pl.program_id(1) == 0)
    def _():
        wc_ref[...] = w_ref[...].astype(BF16)

    acc = jnp.dot(x_ref[...], wc_ref[...], preferred_element_type=F32)
    if scale != 1.0:
        acc = acc * scale
    if has_bias:
        acc = acc + b_ref[...]
    o_ref[...] = _act(act, acc).astype(o_ref.dtype)


def _pick(n, pref):
    for t in pref:
        if n % t == 0:
            return t
    return n


MM_WEIGHT_TILE_ELEMS = 2 * 1024 * 1024


def matmul(x, w, slot=0, *, x_slot=None, act="none", bias=None, scale=1.0, out_dtype=F32):
    m, kdim = x.shape[-2:]
    _, kw, n = w.shape
    assert kw == kdim, (w.shape, x.shape)
    tm = _pick(m, (1024, 512, 256, 128, 64, 32, 16, 8))
    tn = _pick(n, tuple(t for t in (1024, 512, 256, 128) if t * kdim <= MM_WEIGHT_TILE_ELEMS))
    if x_slot is None:
        x_spec = pl.BlockSpec((tm, kdim), lambda j, i: (i, 0))
    else:
        x_spec = pl.BlockSpec((None, tm, kdim), lambda j, i: (x_slot, i, 0))
    in_specs = [x_spec, pl.BlockSpec((None, kdim, tn), lambda j, i: (slot, 0, j))]
    args = [x, w]
    if bias is not None:
        in_specs.append(pl.BlockSpec((1, tn), lambda j, i: (0, j)))
        args.append(bias.reshape(1, n).astype(F32))
    return pl.pallas_call(
        functools.partial(_mm_body, act=act, has_bias=bias is not None, scale=scale),
        grid=(n // tn, m // tm),
        in_specs=in_specs,
        out_specs=pl.BlockSpec((tm, tn), lambda j, i: (i, j)),
        out_shape=jax.ShapeDtypeStruct((m, n), out_dtype),
        scratch_shapes=[pltpu.VMEM((kdim, tn), BF16)],
        compiler_params=_cparams(("arbitrary", "arbitrary")),
        name="matmul_" + act,
    )(*args)


def _layer_norm_rows(z, g_ref, b_ref):
    mu = jnp.mean(z, -1, keepdims=True)
    zc = z - mu
    var = jnp.mean(zc * zc, -1, keepdims=True)
    return zc * lax.rsqrt(var + LN_EPS) * g_ref[...] + b_ref[...]


def _store_rows(y, oa_ref, ob_ref, head_tiles):
    if head_tiles is None:
        oa_ref[...] = y
        ob_ref[...] = y.astype(BF16)
    else:
        i = pl.program_id(0)

        @pl.when(i < head_tiles)
        def _():
            oa_ref[...] = y

        @pl.when(i >= head_tiles)
        def _():
            ob_ref[...] = y


def _ln_body(*refs, alpha, routed):
    if routed:
        x_ref, h_ref, g_ref, b_ref, router_ref, oa_ref, ob_ref, logit_ref = refs
    else:
        x_ref, h_ref, g_ref, b_ref, oa_ref, ob_ref = refs
    y = _layer_norm_rows(alpha * x_ref[...] + h_ref[...], g_ref, b_ref)
    if routed:
        logit_ref[...] = jnp.dot(y, router_ref[...], precision=_HI, preferred_element_type=F32)
    _store_rows(y, oa_ref, ob_ref, None)


LN_ROWS = 512


def _ln_out_specs(m, d, tm, split_rows):
    row = pl.BlockSpec((tm, d), lambda i, *_: (i, 0))
    if split_rows is None:
        return None, [row, row], [jax.ShapeDtypeStruct((m, d), F32),
                                  jax.ShapeDtypeStruct((m, d), BF16)]
    assert split_rows % tm == 0 and 0 < split_rows < m
    head_tiles = split_rows // tm
    specs = [pl.BlockSpec((tm, d), lambda i, *_: (jnp.minimum(i, head_tiles - 1), 0)),
             pl.BlockSpec((tm, d), lambda i, *_: (jnp.maximum(i - head_tiles, 0), 0))]
    shapes = [jax.ShapeDtypeStruct((split_rows, d), F32),
              jax.ShapeDtypeStruct((m - split_rows, d), F32)]
    return head_tiles, specs, shapes


def residual_layer_norm(x, h, g, b, alpha, router_w=None):
    m, d = x.shape
    tm = _pick(m, (LN_ROWS, 256, 128, 64, 32, 16, 8))
    row = pl.BlockSpec((tm, d), lambda i: (i, 0))
    vec = pl.BlockSpec((1, d), lambda i: (0, 0))
    _, out_specs, out_shape = _ln_out_specs(m, d, tm, None)
    r_specs, r_args = [], []
    if router_w is not None:
        n_logit = router_w.shape[1]
        r_specs, r_args = [pl.BlockSpec((d, n_logit), lambda i: (0, 0))], [router_w]
        out_specs = out_specs + [pl.BlockSpec((tm, n_logit), lambda i: (i, 0))]
        out_shape = out_shape + [jax.ShapeDtypeStruct((m, n_logit), F32)]
    return pl.pallas_call(
        functools.partial(_ln_body, alpha=alpha, routed=router_w is not None),
        grid=(m // tm,),
        in_specs=[row, row, vec, vec] + r_specs,
        out_specs=out_specs,
        out_shape=out_shape,
        compiler_params=_cparams(("parallel",)),
        name="residual_layer_norm",
    )(x, h, g.reshape(1, d), b.reshape(1, d), *r_args)


def _combine_ln_body(dest_ref, x_ref, ys_ref, gate_ref, g_ref, b_ref, oa_ref, ob_ref, buf_ref, sem_ref,
                     *, alpha, head_tiles, tm, top_k):
    i = pl.program_id(0)
    n_tiles = pl.num_programs(0)

    def issue(tile, slot):
        base = tile * (tm * top_k)

        def body(r, carry):
            for k in range(top_k):
                row = dest_ref[base + r * top_k + k]
                pltpu.make_async_copy(ys_ref.at[pl.ds(row, 1)], buf_ref.at[slot, k, pl.ds(r, 1)],
                                      sem_ref.at[slot]).start()
            return carry

        lax.fori_loop(0, tm, body, 0, unroll=8)

    @pl.when(i == 0)
    def _():
        issue(0, 0)

    @pl.when(i + 1 < n_tiles)
    def _():
        issue(i + 1, (i + 1) % 2)

    slot = i % 2
    for k in range(top_k):
        pltpu.make_async_copy(ys_ref.at[pl.ds(0, tm)], buf_ref.at[slot, k], sem_ref.at[slot]).wait()
    h = buf_ref[slot, 0] * gate_ref[:, 0:1]
    for k in range(1, top_k):
        h = h + buf_ref[slot, k] * gate_ref[:, k:k + 1]
    y = _layer_norm_rows(alpha * x_ref[...] + h, g_ref, b_ref)
    _store_rows(y, oa_ref, ob_ref, head_tiles)


def moe_combine_layer_norm(x, ys, dest, gates, g, b, alpha, split_rows=None):
    m, d = x.shape
    top_k = dest.shape[1]
    tm = LN_ROWS
    assert m % tm == 0 and ys.dtype == F32
    head_tiles, out_specs, out_shape = _ln_out_specs(m, d, tm, split_rows)
    row = pl.BlockSpec((tm, d), lambda i, *_: (i, 0))
    vec = pl.BlockSpec((1, d), lambda i, *_: (0, 0))
    return pl.pallas_call(
        functools.partial(_combine_ln_body, alpha=alpha, head_tiles=head_tiles, tm=tm, top_k=top_k),
        grid_spec=pltpu.PrefetchScalarGridSpec(
            num_scalar_prefetch=1,
            grid=(m // tm,),
            in_specs=[row, pl.BlockSpec(memory_space=pl.ANY),
                      pl.BlockSpec((tm, top_k), lambda i, *_: (i, 0)), vec, vec],
            out_specs=out_specs,
            scratch_shapes=[pltpu.VMEM((2, top_k, tm, d), F32), pltpu.SemaphoreType.DMA((2,))]),
        out_shape=out_shape,
        compiler_params=_cparams(("arbitrary",)),
        name="moe_combine_layer_norm",
    )(dest.reshape(-1).astype(jnp.int32), x, ys, gates, g.reshape(1, d), b.reshape(1, d))


SUBLANES = 8


def _shift_mix_body(x_ref, halo_ref, shift_ref, mu_ref, o_ref, *, tm, tp, ts, prompt_tiles):
    i = pl.program_id(0)
    x = x_ref[...]
    row = lax.broadcasted_iota(jnp.int32, (tm, 1), 0)
    prev = jnp.where(row == 0, halo_ref[SUBLANES - 1:SUBLANES, :], pltpu.roll(x, 1, 0))
    prompt_start = jnp.logical_and(row == 0, (i * tm) % tp == 0)
    prev_prompt = jnp.where(prompt_start, 0.0, prev)
    nseq = tm // ts
    carried = jnp.broadcast_to(shift_ref[...][:, None, :], (nseq, ts, x.shape[1])).reshape(x.shape)
    prev_sample = jnp.where(row % ts == 0, carried, prev)
    xx = jnp.where(i < prompt_tiles, prev_prompt, prev_sample) - x
    for j in range(o_ref.shape[0]):
        o_ref[j] = (x + xx * mu_ref[j:j + 1, :]).astype(BF16)


def shift_mix(x, shift0, mu, *, n_prompt_rows, tp, ts):
    m, d = x.shape
    nmix = mu.shape[0]
    tm = 512
    assert tp % tm == 0 and tm % ts == 0 and n_prompt_rows % tm == 0 and m % tm == 0
    prompt_tiles = n_prompt_rows // tm
    nseq = tm // ts
    assert shift0.shape[0] % nseq == 0
    halo_blocks = tm // SUBLANES
    return pl.pallas_call(
        functools.partial(_shift_mix_body, tm=tm, tp=tp, ts=ts, prompt_tiles=prompt_tiles),
        grid=(m // tm,),
        in_specs=[pl.BlockSpec((tm, d), lambda i: (i, 0)),
                  pl.BlockSpec((SUBLANES, d), lambda i: (jnp.maximum(i * halo_blocks - 1, 0), 0)),
                  pl.BlockSpec((nseq, d), lambda i: (jnp.maximum(i - prompt_tiles, 0), 0)),
                  pl.BlockSpec((nmix, d), lambda i: (0, 0))],
        out_specs=pl.BlockSpec((nmix, tm, d), lambda i: (0, i, 0)),
        out_shape=jax.ShapeDtypeStruct((nmix, m, d), BF16),
        compiler_params=_cparams(("parallel",)),
        name="rwkv_shift_mix",
    )(x, x, shift0, mu)


def _chunk_state_init(c, ncp, cps, state_ref, s0_ref):
    @pl.when(jnp.logical_and(c < ncp, c % cps == 0))
    def _():
        state_ref[...] = jnp.zeros(state_ref.shape, state_ref.dtype)

    @pl.when(c >= ncp)
    def _():
        state_ref[...] = s0_ref[0]


def _dla_body(*refs, heads, dk, dv, mode, ncp, cps, n_prev):
    if n_prev:
        prev_sp_ref, prev_ss_ref, o_ref, sp_ref, ss_ref, st_ref, sem_ref = refs[-7:]
        refs = refs[:-7]
    else:
        o_ref, sp_ref, ss_ref, st_ref = refs[-4:]
        refs = refs[:-4]
    if mode == "gla":
        q_ref, k_ref, v_ref, g_ref, gate_ref, ng_ref, s0_ref = refs
    else:
        q_ref, k_ref, v_ref, cos_ref, sin_ref, gate_ref, ng_ref, s0_ref = refs
    c = pl.program_id(0)
    _chunk_state_init(c, ncp, cps, st_ref, s0_ref)

    t_row = lax.broadcasted_iota(jnp.int32, (CHUNK, CHUNK), 0)
    t_col = lax.broadcasted_iota(jnp.int32, (CHUNK, CHUNK), 1)
    causal = t_row >= t_col
    if mode == "gla":
        cum_all = _chunk_prefix_sum(g_ref[...])
    else:
        width = heads * dk
        even = (lax.broadcasted_iota(jnp.int32, (CHUNK, width), 1) % 2) == 0
        q_all = q_ref[...]
        k_all = k_ref[...]
        q_sw = jnp.where(even, pltpu.roll(q_all, width - 1, 1), pltpu.roll(q_all, 1, 1))
        k_sw = jnp.where(even, pltpu.roll(k_all, width - 1, 1), pltpu.roll(k_all, 1, 1))
        cos = cos_ref[...]
        sin = sin_ref[...]
        frame = (lax.broadcasted_iota(jnp.int32, (CHUNK, 1), 0) + 1).astype(F32)

    for h in range(heads):
        ks = slice(h * dk, (h + 1) * dk)
        vs = slice(h * dv, (h + 1) * dv)
        v = v_ref[:, vs]
        s_prev = st_ref[h]
        if mode == "gla":
            q = q_ref[:, ks]
            k = k_ref[:, ks]
            cum = cum_all[:, ks]
            total = cum[CHUNK - 1:CHUNK, :]
            total_col = jnp.transpose(jnp.broadcast_to(total, (LANES, dk)))[:, :1]
            state_decay = jnp.exp(total_col)
            q_dec = q * jnp.exp(cum)
            k_inv = k * jnp.exp(-cum)
            k_tail = k * jnp.exp(total - cum)
        else:
            q = q_all[:, ks] * cos + q_sw[:, ks] * sin
            k = (k_all[:, ks] * cos + k_sw[:, ks] * sin) * (dk ** -0.5)
            log_gamma = math.log1p(-(2.0 ** (-5.0 - h)))
            cum = frame * log_gamma
            total = CHUNK * log_gamma
            state_decay = math.exp(total)
            q_dec = q * jnp.exp(cum)
            k_inv = k * jnp.exp(-cum)
            k_tail = k * jnp.exp(total - cum)
        scores = jnp.where(causal, _bdot_nt(q_dec, k_inv), 0.0)
        o = _bdot(q_dec, s_prev) + _bdot(scores, v)
        k_tail_t = jnp.transpose(k_tail)
        st_ref[h] = state_decay * s_prev + _bdot(k_tail_t, v)
        if mode == "gla":
            o = o * lax.rsqrt(jnp.mean(o * o, -1, keepdims=True) + GLA_NORM_EPS) * ng_ref[...]
        else:
            mu = jnp.mean(o, -1, keepdims=True)
            oc = o - mu
            var = jnp.mean(oc * oc, -1, keepdims=True)
            o = oc * lax.rsqrt(var + RET_GN_EPS) * ng_ref[:, vs]
        if mode == "gla":
            o_ref[:, vs] = (o * gate_ref[:, vs]).astype(BF16)
        else:
            o_ref[:, vs] = (gate_ref[:, vs] * o).astype(BF16)

    prompt_end = jnp.logical_and(c < ncp, c % cps == cps - 1)
    if not n_prev:
        @pl.when(prompt_end)
        def _():
            sp_ref[0] = st_ref[...]

        @pl.when(c >= ncp)
        def _():
            ss_ref[0] = st_ref[...]
    else:
        carry_over = [pltpu.make_async_copy(prev_sp_ref, sp_ref.at[pl.ds(0, n_prev)], sem_ref.at[0]),
                      pltpu.make_async_copy(prev_ss_ref, ss_ref.at[pl.ds(0, n_prev)], sem_ref.at[1])]

        @pl.when(c == 0)
        def _():
            for cp in carry_over:
                cp.start()

        def write_state(dst):
            cp = pltpu.make_async_copy(st_ref, dst, sem_ref.at[2])
            cp.start()
            cp.wait()

        @pl.when(prompt_end)
        def _():
            write_state(sp_ref.at[n_prev, c // cps])

        @pl.when(c >= ncp)
        def _():
            write_state(ss_ref.at[n_prev, c - ncp])

        @pl.when(c == pl.num_programs(0) - 1)
        def _():
            for cp in carry_over:
                cp.wait()


def _seq_state_specs(state_shape, ncp, cps, n_prompt, slot):
    blk = (None, 1) + tuple(state_shape)
    zeros = (0,) * len(state_shape)
    s0_spec = pl.BlockSpec(blk, lambda c: (slot, jnp.maximum(c - ncp, 0)) + zeros)
    sp_spec = pl.BlockSpec(blk, lambda c: (0, jnp.minimum(c // cps, n_prompt - 1)) + zeros)
    ss_spec = pl.BlockSpec(blk, lambda c: (0, jnp.maximum(c - ncp, 0)) + zeros)
    return s0_spec, sp_spec, ss_spec


def _state_out_shapes(n_prompt, n_sample, state_shape):
    return [jax.ShapeDtypeStruct((1, n_prompt) + tuple(state_shape), F32),
            jax.ShapeDtypeStruct((1, n_sample) + tuple(state_shape), F32)]


def decay_attention(q, k, v, extra, gate, norm_g, s0, slot, earlier_states=None, *, mode, heads,
                    n_prompt, cps):
    nt = q.shape[0]
    dk = q.shape[1] // heads
    dv = v.shape[1] // heads
    n_sample = s0.shape[1]
    ncp = n_prompt * cps
    nchunks = nt // CHUNK
    assert nchunks == ncp + n_sample
    rowk = pl.BlockSpec((CHUNK, heads * dk), lambda c: (c, 0))
    rowv = pl.BlockSpec((CHUNK, heads * dv), lambda c: (c, 0))
    s0_spec, sp_spec, ss_spec = _seq_state_specs((heads, dk, dv), ncp, cps, n_prompt, slot)
    if mode == "gla":
        extra_specs = [rowk]
        ng_spec = pl.BlockSpec((1, dv), lambda c: (0, 0))
        norm_g = norm_g.reshape(1, dv)
    else:
        pos_spec = pl.BlockSpec((CHUNK, dk), lambda c: (jnp.where(c < ncp, c % cps, cps), 0))
        extra_specs = [pos_spec, pos_spec]
        ng_spec = pl.BlockSpec((1, heads * dv), lambda c: (0, 0))
        norm_g = norm_g.reshape(1, heads * dv)
    in_specs = [rowk, rowk, rowv] + extra_specs + [rowv, ng_spec, s0_spec]
    state_shapes = _state_out_shapes(n_prompt, n_sample, (heads, dk, dv))
    scratch = [pltpu.VMEM((heads, dk, dv), F32)]
    args = [q, k, v, *extra, gate, norm_g, s0]
    n_prev = 0
    if earlier_states is not None:
        hbm = pl.BlockSpec(memory_space=pl.ANY)
        n_prev = earlier_states[0].shape[0]
        in_specs += [hbm, hbm]
        args += list(earlier_states)
        sp_spec = ss_spec = hbm
        state_shapes = [jax.ShapeDtypeStruct((n_prev + 1,) + s.shape[1:], F32) for s in state_shapes]
        scratch.append(pltpu.SemaphoreType.DMA((3,)))
    return pl.pallas_call(
        functools.partial(_dla_body, heads=heads, dk=dk, dv=dv, mode=mode, ncp=ncp, cps=cps,
                          n_prev=n_prev),
        grid=(nchunks,),
        in_specs=in_specs,
        out_specs=[rowv, sp_spec, ss_spec],
        out_shape=[jax.ShapeDtypeStruct((nt, heads * dv), BF16)] + state_shapes,
        scratch_shapes=scratch,
        compiler_params=_cparams(("arbitrary",)),
        name="decay_attention_" + mode,
    )(*args)


def _rwkv_body(r_ref, k_ref, v_ref, lw_ref, a_ref, g_ref, kk_ref, ka_ref, rk_ref, gng_ref, gnb_ref,
               s0_ref, o_ref, sp_ref, ss_ref, st_ref, *, pairs, ncp, cps):
    c = pl.program_id(0)
    n = RWKV_HEAD
    w2 = 2 * n

    @pl.when(jnp.logical_and(c < ncp, c % cps == 0))
    def _():
        st_ref[...] = jnp.zeros(st_ref.shape, F32)

    @pl.when(c >= ncp)
    def _():
        zero = jnp.zeros((n, n), F32)
        for p in range(pairs):
            top = jnp.concatenate([s0_ref[0, 2 * p], zero], axis=1)
            bot = jnp.concatenate([zero, s0_ref[0, 2 * p + 1]], axis=1)
            st_ref[p] = jnp.concatenate([top, bot], axis=0)

    def paired(ref):
        x = ref[...]
        return jnp.stack([x[:, p * w2:(p + 1) * w2] for p in range(pairs)])

    lane = lax.broadcasted_iota(jnp.int32, (1, 1, w2), 2)
    first = lane < n

    def head_sum(x):
        s_a = jnp.sum(jnp.where(first, x, 0.0), -1, keepdims=True)
        s_b = jnp.sum(jnp.where(first, 0.0, x), -1, keepdims=True)
        return jnp.where(first, s_a, s_b)

    def stacked(x):
        return jnp.concatenate([jnp.where(first, x, 0.0), jnp.where(first, 0.0, x)], axis=1)

    def bmm(a, b):
        return lax.dot_general(a.astype(BF16), b.astype(BF16), (((2,), (1,)), ((0,), (0,))),
                               preferred_element_type=F32)

    def bmm_nt(a, b):
        return lax.dot_general(a.astype(BF16), b.astype(BF16), (((2,), (2,)), ((0,), (0,))),
                               preferred_element_type=F32)

    r = paired(r_ref)
    k_raw = paired(k_ref)
    v = paired(v_ref)
    lw = paired(lw_ref)
    a = paired(a_ref)
    k_k = paired(kk_ref)
    k_a = paired(ka_ref)
    r_k = paired(rk_ref)

    kk = k_raw * k_k
    kk = kk / jnp.maximum(jnp.sqrt(head_sum(kk * kk)), 1e-12)
    k_h = k_raw * (1.0 + (a - 1.0) * k_a)
    a_vec = -kk
    b_vec = kk * a

    cum = paired_value(_chunk_prefix_sum(lw_ref[...]), pairs, w2)
    total = cum[:, CHUNK - 1:CHUNK, :]
    p_now = jnp.exp(cum)
    p_prev = jnp.exp(cum - lw)
    p_inv = jnp.exp(-cum)
    p_tail = jnp.exp(total - cum)

    lhs = jnp.concatenate([stacked(a_vec * p_prev), stacked(r * p_now)], axis=1)
    rhs = jnp.concatenate([stacked(b_vec * p_inv), stacked(k_h * p_inv)], axis=1)
    sc = bmm_nt(lhs, rhs)
    s_prev = st_ref[...]
    sr = bmm_nt(lhs, s_prev)

    i_row = lax.broadcasted_iota(jnp.int32, (1, w2, w2), 1)
    i_col = lax.broadcasted_iota(jnp.int32, (1, w2, w2), 2)
    same = (i_row // n) == (i_col // n)
    strict = jnp.logical_and(same, (i_col % n) < (i_row % n))
    incl = jnp.logical_and(same, (i_col % n) <= (i_row % n))
    a_ab = jnp.where(strict, sc[:, :w2, :w2], 0.0)
    a_ak = jnp.where(strict, sc[:, :w2, w2:], 0.0)
    a_rb = jnp.where(incl, sc[:, w2:, :w2], 0.0)
    a_rk = jnp.where(incl, sc[:, w2:, w2:], 0.0)
    u0 = sr[:, :w2]
    y0 = sr[:, w2:]
    v_st = stacked(v)

    eye = (i_row == i_col).astype(F32)
    t_inv = eye + a_ab
    power = a_ab
    span = 1
    while 2 * span < CHUNK:
        power = bmm(power, power)
        t_inv = t_inv + bmm(t_inv, power)
        span *= 2

    u_st = bmm(t_inv, u0 + bmm(a_ak, v_st))
    uv = jnp.concatenate([u_st, v_st], axis=1)
    y_st = y0 + bmm(jnp.concatenate([a_rb, a_rk], axis=2), uv)
    y = y_st[:, :n] + y_st[:, n:]

    tails = jnp.concatenate([stacked(b_vec * p_tail), stacked(k_h * p_tail)], axis=1)
    uv_t = jnp.swapaxes(uv, 1, 2)
    s_new = s_prev * jnp.exp(total) + bmm(uv_t, tails)
    st_ref[...] = s_new

    mu = head_sum(y) * (1.0 / n)
    yc = y - mu
    var = head_sum(yc * yc) * (1.0 / n)
    yn = yc * lax.rsqrt(var + RWKV_GN_EPS) * paired(gng_ref) + paired(gnb_ref)
    out = yn + head_sum(r * k_h * r_k) * v
    gate = paired(g_ref)
    for p in range(pairs):
        o_ref[:, p * w2:(p + 1) * w2] = (out[p] * gate[p]).astype(BF16)

    def store_state(dst_ref):
        for p in range(pairs):
            dst_ref[0, 2 * p] = s_new[p, :n, :n]
            dst_ref[0, 2 * p + 1] = s_new[p, n:, n:]

    @pl.when(jnp.logical_and(c < ncp, c % cps == cps - 1))
    def _():
        store_state(sp_ref)

    @pl.when(c >= ncp)
    def _():
        store_state(ss_ref)


def paired_value(x, pairs, w2):
    return jnp.stack([x[:, p * w2:(p + 1) * w2] for p in range(pairs)])


def rwkv7_attention(r, k, v, lw, a, g, k_k, k_a, r_k, gn_g, gn_b, s0, slot, *, n_prompt, cps):
    nt, d = r.shape
    heads = d // RWKV_HEAD
    pairs = heads // 2
    n_sample = s0.shape[1]
    ncp = n_prompt * cps
    assert nt // CHUNK == ncp + n_sample
    row = pl.BlockSpec((CHUNK, d), lambda c: (c, 0))
    vec = pl.BlockSpec((1, d), lambda c: (0, 0))
    s0_spec, sp_spec, ss_spec = _seq_state_specs((heads, RWKV_HEAD, RWKV_HEAD), ncp, cps, n_prompt,
                                                 slot)
    vecs = [u.reshape(1, d) for u in (k_k, k_a, r_k, gn_g, gn_b)]
    return pl.pallas_call(
        functools.partial(_rwkv_body, pairs=pairs, ncp=ncp, cps=cps),
        grid=(nt // CHUNK,),
        in_specs=[row] * 6 + [vec] * 5 + [s0_spec],
        out_specs=[row, sp_spec, ss_spec],
        out_shape=([jax.ShapeDtypeStruct((nt, d), BF16)]
                   + _state_out_shapes(n_prompt, n_sample, (heads, RWKV_HEAD, RWKV_HEAD))),
        scratch_shapes=[pltpu.VMEM((pairs, 2 * RWKV_HEAD, 2 * RWKV_HEAD), F32)],
        compiler_params=_cparams(("arbitrary",)),
        name="rwkv7_attention",
    )(r, k, v, lw, a, g, *vecs, s0)


def _expert_weight_pipeline(be_ref, chg_ref, nxt_ref, hbm_refs, stage_refs, cache_refs, sem_ref,
                            tile):
    j = pl.program_id(0)
    b = pl.program_id(1)
    n_pass = pl.num_programs(0)

    def copies(expert, col_pass):
        col = pl.multiple_of(col_pass * tile, tile)
        return [pltpu.make_async_copy(hbm.at[expert, :, pl.ds(col, tile)], stage, sem_ref.at[i])
                for i, (hbm, stage) in enumerate(zip(hbm_refs, stage_refs))]

    @pl.when(jnp.logical_and(j == 0, b == 0))
    def _():
        for cp in copies(be_ref[0], 0):
            cp.start()

    @pl.when(chg_ref[b] == 1)
    def _():
        for cp in copies(be_ref[b], j):
            cp.wait()
        for stage, cache in zip(stage_refs, cache_refs):
            cache[...] = stage[...].astype(BF16)
        nxt = nxt_ref[b]

        @pl.when(nxt >= 0)
        def _():
            for cp in copies(be_ref[jnp.maximum(nxt, 0)], j):
                cp.start()

        @pl.when(jnp.logical_and(nxt < 0, j + 1 < n_pass))
        def _():
            for cp in copies(be_ref[0], j + 1):
                cp.start()


def _gate_up_body(be_ref, chg_ref, nxt_ref, nu_ref, x_ref, w1_ref, w3_ref, h_ref,
                  w1s_ref, w3s_ref, w1c_ref, w3c_ref, sem_ref, *, tile):
    b = pl.program_id(1)
    _expert_weight_pipeline(be_ref, chg_ref, nxt_ref, (w1_ref, w3_ref), (w1s_ref, w3s_ref),
                            (w1c_ref, w3c_ref), sem_ref, tile)

    @pl.when(b < nu_ref[0])
    def _():
        x = x_ref[...]
        gate = jnp.dot(x, w1c_ref[...], preferred_element_type=F32)
        up = jnp.dot(x, w3c_ref[...], preferred_element_type=F32)
        h_ref[...] = (gate * jax.nn.sigmoid(gate) * up).astype(BF16)

    @pl.when(b >= nu_ref[0])
    def _():
        h_ref[...] = jnp.zeros(h_ref.shape, BF16)


def _down_body(be_ref, chg_ref, nxt_ref, nu_ref, h_ref, w2_ref, y_ref, w2s_ref, w2c_ref, sem_ref, *,
               tile):
    b = pl.program_id(1)
    _expert_weight_pipeline(be_ref, chg_ref, nxt_ref, (w2_ref,), (w2s_ref,), (w2c_ref,), sem_ref,
                            tile)

    @pl.when(b < nu_ref[0])
    def _():
        y_ref[...] = jnp.dot(h_ref[...], w2c_ref[...],
                             preferred_element_type=F32).astype(y_ref.dtype)

    @pl.when(b >= nu_ref[0])
    def _():
        y_ref[...] = jnp.zeros(y_ref.shape, y_ref.dtype)


def _block_meta(block_expert, n_used):
    be = block_expert.astype(jnp.int32)
    nb = be.shape[0]
    changed = jnp.concatenate([jnp.ones((1,), jnp.int32),
                               (be[1:] != be[:-1]).astype(jnp.int32)])
    idx = jnp.arange(nb, dtype=jnp.int32)
    opener = jnp.where(changed == 1, idx, nb)
    after = lax.cummin(jnp.concatenate([opener[1:], jnp.full((1,), nb, jnp.int32)]), reverse=True)
    nxt = jnp.where(after >= nb, -1, after).astype(jnp.int32)
    return be, changed, nxt, jnp.reshape(n_used, (1,)).astype(jnp.int32)


def swiglu_gate_up(xs, block_expert, n_used, w1, w3, block, tile_f=(512, 256, 128)):
    rows, d = xs.shape
    f = w1.shape[-1]
    tf = _pick(f, tile_f)
    hbm = pl.BlockSpec(memory_space=pl.ANY)
    return pl.pallas_call(
        functools.partial(_gate_up_body, tile=tf),
        grid_spec=pltpu.PrefetchScalarGridSpec(
            num_scalar_prefetch=4,
            grid=(f // tf, rows // block),
            in_specs=[pl.BlockSpec((block, d), lambda j, b, *_: (b, 0)), hbm, hbm],
            out_specs=pl.BlockSpec((block, tf), lambda j, b, *_: (b, j)),
            scratch_shapes=[pltpu.VMEM((d, tf), F32), pltpu.VMEM((d, tf), F32),
                            pltpu.VMEM((d, tf), BF16), pltpu.VMEM((d, tf), BF16),
                            pltpu.SemaphoreType.DMA((2,))]),
        out_shape=jax.ShapeDtypeStruct((rows, f), BF16),
        compiler_params=_cparams(("arbitrary", "arbitrary")),
        name="swiglu_gate_up",
    )(*_block_meta(block_expert, n_used), xs, w1, w3)


def swiglu_down(h, block_expert, n_used, w2, block, out_dtype=F32):
    rows, f = h.shape
    d = w2.shape[-1]
    tn = _pick(d, (512, 256, 128))
    return pl.pallas_call(
        functools.partial(_down_body, tile=tn),
        grid_spec=pltpu.PrefetchScalarGridSpec(
            num_scalar_prefetch=4,
            grid=(d // tn, rows // block),
            in_specs=[pl.BlockSpec((block, f), lambda j, b, *_: (b, 0)),
                      pl.BlockSpec(memory_space=pl.ANY)],
            out_specs=pl.BlockSpec((block, tn), lambda j, b, *_: (b, j)),
            scratch_shapes=[pltpu.VMEM((f, tn), F32), pltpu.VMEM((f, tn), BF16),
                            pltpu.SemaphoreType.DMA((1,))]),
        out_shape=jax.ShapeDtypeStruct((rows, d), out_dtype),
        compiler_params=_cparams(("arbitrary", "arbitrary")),
        name="swiglu_down",
    )(*_block_meta(block_expert, n_used), h, w2)


def _dispatch_body(src_ref, nu_ref, x_ref, o_ref, buf_ref, sem_ref, *, block):
    b = pl.program_id(0)
    n_used = nu_ref[0]

    def issue(blk, slot):
        base = blk * block

        def body(r, carry):
            pltpu.make_async_copy(x_ref.at[pl.ds(src_ref[base + r], 1)],
                                  buf_ref.at[slot, pl.ds(r, 1)], sem_ref.at[slot]).start()
            return carry

        lax.fori_loop(0, block, body, 0, unroll=8)

    @pl.when(jnp.logical_and(b == 0, n_used > 0))
    def _():
        issue(0, 0)

    @pl.when(b + 1 < n_used)
    def _():
        issue(b + 1, (b + 1) % 2)

    @pl.when(b < n_used)
    def _():
        slot = b % 2
        pltpu.make_async_copy(x_ref.at[pl.ds(0, block)], buf_ref.at[slot], sem_ref.at[slot]).wait()
        o_ref[...] = buf_ref[slot].astype(BF16)

    @pl.when(b >= n_used)
    def _():
        o_ref[...] = jnp.zeros(o_ref.shape, BF16)


def moe_dispatch(x, src_tok, n_used, block):
    n, d = x.shape
    rows = src_tok.shape[0]
    return pl.pallas_call(
        functools.partial(_dispatch_body, block=block),
        grid_spec=pltpu.PrefetchScalarGridSpec(
            num_scalar_prefetch=2,
            grid=(rows // block,),
            in_specs=[pl.BlockSpec(memory_space=pl.ANY)],
            out_specs=pl.BlockSpec((block, d), lambda b, *_: (b, 0)),
            scratch_shapes=[pltpu.VMEM((2, block, d), F32), pltpu.SemaphoreType.DMA((2,))]),
        out_shape=jax.ShapeDtypeStruct((rows, d), BF16),
        compiler_params=_cparams(("arbitrary",)),
        name="moe_dispatch",
    )(src_tok.astype(jnp.int32), jnp.reshape(n_used, (1,)).astype(jnp.int32), x)


def _pad_cols(w, mult=LANES):
    pad = (-w.shape[-1]) % mult
    return jnp.pad(w, [(0, 0)] * (w.ndim - 1) + [(0, pad)]) if pad else w


def _pad_rows(w, mult=LANES):
    pad = (-w.shape[-2]) % mult
    return jnp.pad(w, [(0, 0)] * (w.ndim - 2) + [(0, pad), (0, 0)]) if pad else w


DENSE_GATE_UP_BLOCK = 1024
DENSE_DOWN_BLOCK = 512
MOE_GATE_UP_TILE_F = (1024, 512, 256, 128)


def dense_swiglu(xb, w1, w3, w2, slot):
    rows = xb.shape[0]
    assert rows % DENSE_GATE_UP_BLOCK == 0 and rows % DENSE_DOWN_BLOCK == 0
    nb_a = rows // DENSE_GATE_UP_BLOCK
    nb_b = rows // DENSE_DOWN_BLOCK
    h = swiglu_gate_up(xb, jnp.full((nb_a,), slot, jnp.int32), jnp.int32(nb_a), w1, w3,
                       DENSE_GATE_UP_BLOCK)
    return swiglu_down(h, jnp.full((nb_b,), slot, jnp.int32), jnp.int32(nb_b), w2, DENSE_DOWN_BLOCK)


def moe_swiglu(logits, xf, w1, w3, w2, slot):
    n, d = xf.shape
    n_exp = logits.shape[-1]
    top_val, top_idx = lax.top_k(logits, MOE_TOP_K)
    gates = jax.nn.softmax(top_val, axis=-1)
    slots = n * MOE_TOP_K
    flat_e = top_idx.reshape(-1)
    onehot = (flat_e[:, None] == jnp.arange(n_exp, dtype=flat_e.dtype)[None, :]).astype(jnp.int32)
    rank = jnp.sum((jnp.cumsum(onehot, axis=0) - onehot) * onehot, axis=1)
    counts = jnp.sum(onehot, axis=0)
    start = jnp.cumsum(counts) - counts
    padded = (counts + MOE_BLOCK - 1) // MOE_BLOCK * MOE_BLOCK
    pend = jnp.cumsum(padded)
    pstart = pend - padded
    nb = (slots + n_exp * (MOE_BLOCK - 1) + MOE_BLOCK - 1) // MOE_BLOCK
    dest = (pstart[flat_e] + rank).astype(jnp.int32)
    block_start = jnp.arange(nb, dtype=jnp.int32) * MOE_BLOCK
    expert_of_block = jnp.clip(jnp.searchsorted(pend, block_start, side="right"), 0, n_exp - 1)
    order = jnp.argsort(flat_e, stable=True)
    e_row = jnp.repeat(expert_of_block, MOE_BLOCK)
    rank_row = jnp.arange(nb * MOE_BLOCK, dtype=jnp.int32) - pstart[e_row]
    valid = rank_row < counts[e_row]
    slot_row = order[jnp.clip(start[e_row] + rank_row, 0, slots - 1)]
    src_tok = jnp.where(valid, slot_row // MOE_TOP_K, 0).astype(jnp.int32)
    block_expert = expert_of_block.astype(jnp.int32) + slot * n_exp
    n_used = pend[-1] // MOE_BLOCK
    xs = moe_dispatch(xf, src_tok, n_used, MOE_BLOCK)
    e1 = w1.reshape((-1,) + w1.shape[2:])
    e3 = w3.reshape((-1,) + w3.shape[2:])
    e2 = w2.reshape((-1,) + w2.shape[2:])
    hs = swiglu_gate_up(xs, block_expert, n_used, e1, e3, MOE_BLOCK, tile_f=MOE_GATE_UP_TILE_F)
    ys = swiglu_down(hs, block_expert, n_used, e2, MOE_BLOCK)
    return ys, dest.reshape(n, MOE_TOP_K), gates


def _rope_tables(dk, cps, past_len):
    half = dk // 2
    inv_freq = 1.0 / (RET_ROPE_BASE ** jnp.linspace(0.0, 1.0, half, dtype=F32))
    pos = jnp.concatenate([jnp.arange(cps * CHUNK, dtype=jnp.int32),
                           past_len + jnp.arange(CHUNK, dtype=jnp.int32)])
    ang = pos.astype(F32)[:, None] * inv_freq[None, :]
    cos = jnp.repeat(jnp.cos(ang), 2, axis=1)
    sin = jnp.stack([-jnp.sin(ang), jnp.sin(ang)], axis=-1).reshape(pos.shape[0], dk)
    return cos, sin


def kernel(x_prompt, x_sample, state_gla, state_rwkv, state_shift, state_ret, ln_g, ln_b,
           gla_wq, gla_wk, gla_wv, gla_wr, gla_wa1, gla_wa2, gla_ba, gla_norm_g, gla_wo,
           rwkv_mu, rwkv_wr, rwkv_wk, rwkv_wv, rwkv_wo, rwkv_w0, rwkv_w1, rwkv_w2, rwkv_a0, rwkv_a1,
           rwkv_a2, rwkv_g1, rwkv_g2, rwkv_k_k, rwkv_k_a, rwkv_r_k, rwkv_gn_g, rwkv_gn_b,
           ret_wq, ret_wk, ret_wv, ret_wg, ret_gn_g, ret_wo,
           ffn_w1, ffn_w3, ffn_w2, moe_router, moe_w1, moe_w3, moe_w2):
    bp, tp, d = x_prompt.shape
    bs, ts, _ = x_sample.shape
    assert ts == CHUNK and tp % CHUNK == 0
    depth = ln_g.shape[0]
    alpha = (2.0 * depth) ** 0.25
    cps = tp // CHUNK
    past_len = tp
    n_prompt_rows = bp * tp
    seq = dict(n_prompt=bp, cps=cps)

    gla_heads = state_gla.shape[2]
    gla_dk = state_gla.shape[3]
    ret_heads = state_ret.shape[2]
    ret_dk = state_ret.shape[3]

    xf = jnp.concatenate([x_prompt.reshape(-1, d), x_sample.reshape(-1, d)], axis=0)
    xb = xf.astype(BF16)

    gla_states, rwkv_states, ret_states = [], [], []
    new_shift_p, new_shift_s = [], []

    def stacked(per_layer, which):
        parts = [states[which] for states in per_layer]
        return parts[0] if len(parts) == 1 else jnp.concatenate(parts, axis=0)

    last_rows = jnp.concatenate([jnp.arange(bp, dtype=jnp.int32) * tp + (tp - 1),
                                 n_prompt_rows + jnp.arange(bs, dtype=jnp.int32) * ts + (ts - 1)])
    for i in range(depth):
        kind, slot = i % 3, i // 3
        if kind == 0:
            q = matmul(xb, gla_wq, slot, scale=gla_dk ** -0.5)
            k = matmul(xb, gla_wk, slot)
            v = matmul(xb, gla_wv, slot)
            gate = matmul(xb, gla_wr, slot, act="silu")
            low = matmul(xb, _pad_cols(gla_wa1), slot, out_dtype=BF16)
            log_alpha = matmul(low, _pad_rows(gla_wa2), slot, act="gla_gate", bias=gla_ba[slot])
            closing = bool(gla_states) and slot == state_gla.shape[0] - 1
            earlier = [stacked(gla_states, 0), stacked(gla_states, 1)] if closing else None
            o, *states = decay_attention(q, k, v, (log_alpha,), gate, gla_norm_g[slot], state_gla,
                                         slot, earlier, mode="gla", heads=gla_heads, **seq)
            gla_states = [states] if closing else gla_states + [states]
            h = matmul(o, gla_wo, slot)
        elif kind == 1:
            mixes = shift_mix(xf, state_shift[slot], rwkv_mu[slot], n_prompt_rows=n_prompt_rows,
                              tp=tp, ts=ts)
            i_r, i_w, i_k, i_v, i_a, i_g = range(6)
            r = matmul(mixes, rwkv_wr, slot, x_slot=i_r)
            k = matmul(mixes, rwkv_wk, slot, x_slot=i_k)
            v = matmul(mixes, rwkv_wv, slot, x_slot=i_v)
            w_mid = matmul(mixes, _pad_cols(rwkv_w1), slot, x_slot=i_w, act="tanh", out_dtype=BF16)
            log_decay = matmul(w_mid, _pad_rows(rwkv_w2), slot, act="rwkv_decay", bias=rwkv_w0[slot])
            a_mid = matmul(mixes, _pad_cols(rwkv_a1), slot, x_slot=i_a, out_dtype=BF16)
            a = matmul(a_mid, _pad_rows(rwkv_a2), slot, act="sigmoid", bias=rwkv_a0[slot])
            g_mid = matmul(mixes, _pad_cols(rwkv_g1), slot, x_slot=i_g, act="sigmoid", out_dtype=BF16)
            g = matmul(g_mid, _pad_rows(rwkv_g2), slot)
            o, *states = rwkv7_attention(r, k, v, log_decay, a, g, rwkv_k_k[slot], rwkv_k_a[slot],
                                         rwkv_r_k[slot].reshape(-1), rwkv_gn_g[slot],
                                         rwkv_gn_b[slot], state_rwkv, slot, **seq)
            rwkv_states.append(states)
            h = matmul(o, rwkv_wo, slot)
            ends = jnp.take(xf, last_rows, axis=0, mode="clip")
            new_shift_p.append(ends[:bp])
            new_shift_s.append(ends[bp:])
        else:
            q = matmul(xb, ret_wq, slot)
            k = matmul(xb, ret_wk, slot)
            v = matmul(xb, ret_wv, slot)
            gate = matmul(xb, ret_wg, slot, act="silu")
            cos, sin = _rope_tables(ret_dk, cps, past_len)
            o, *states = decay_attention(q, k, v, (cos, sin), gate, ret_gn_g[slot], state_ret, slot,
                                         mode="ret", heads=ret_heads, **seq)
            ret_states.append(states)
            h = matmul(o, ret_wo, slot)
        fslot = i // 2
        is_moe = i % 2 == 1
        router_w = _pad_cols(moe_router[fslot]) if is_moe else None
        xf, xb, *logits = residual_layer_norm(xf, h, ln_g[i, 0], ln_b[i, 0], alpha, router_w=router_w)
        last = i == depth - 1
        if is_moe:
            ys, dest, gates = moe_swiglu(logits[0][:, :moe_router.shape[-1]], xf, moe_w1, moe_w3,
                                         moe_w2, fslot)
            out_a, out_b = moe_combine_layer_norm(xf, ys, dest, gates, ln_g[i, 1], ln_b[i, 1], alpha,
                                                  split_rows=n_prompt_rows if last else None)
        else:
            h = dense_swiglu(xb, ffn_w1, ffn_w3, ffn_w2, fslot)
            out_a, out_b = residual_layer_norm(xf, h, ln_g[i, 1], ln_b[i, 1], alpha)
            if last:
                out_a, out_b = out_a[:n_prompt_rows], out_a[n_prompt_rows:]
        if last:
            y_prompt, y_sample = out_a.reshape(bp, tp, d), out_b.reshape(bs, ts, d)
        else:
            xf, xb = out_a, out_b

    return (y_prompt, y_sample,
            stacked(gla_states, 0), stacked(rwkv_states, 0), jnp.stack(new_shift_p),
            stacked(ret_states, 0),
            stacked(gla_states, 1), stacked(rwkv_states, 1), jnp.stack(new_shift_s),
            stacked(ret_states, 1))
```

```python
import functools
import math

import jax
import jax.numpy as jnp
from jax import lax
from jax.experimental import pallas as pl
from jax.experimental.pallas import tpu as pltpu

F32 = jnp.float32
BF16 = jnp.bfloat16

CHUNK = 64
LANES = 128
VMEM_LIMIT_BYTES = 56 * 1024 * 1024

LN_EPS = 1e-5
GLA_TAU = 16.0
GLA_NORM_EPS = 1e-5
RWKV_HEAD = 64
RWKV_GN_EPS = 64e-5
RET_ROPE_BASE = 10000.0
RET_GN_EPS = 1e-5
MOE_TOP_K = 2
MOE_BLOCK = 512

_HI = lax.Precision.HIGHEST


def _cparams(sem):
    return pltpu.CompilerParams(dimension_semantics=sem, vmem_limit_bytes=VMEM_LIMIT_BYTES)


def _bdot(a, b):
    return jnp.dot(a.astype(BF16), b.astype(BF16), preferred_element_type=F32)


def _bdot_nt(a, b):
    return lax.dot_general(a.astype(BF16), b.astype(BF16), (((1,), (1,)), ((), ())),
                           preferred_element_type=F32)


def _chunk_prefix_sum(x):
    t_row = lax.broadcasted_iota(jnp.int32, (CHUNK, 3 * CHUNK), 0)
    t_col = lax.broadcasted_iota(jnp.int32, (CHUNK, 3 * CHUNK), 1) % CHUNK
    lower_ones = (t_row >= t_col).astype(BF16)
    hi = x.astype(BF16)
    rest = x - hi.astype(F32)
    mid = rest.astype(BF16)
    lo = (rest - mid.astype(F32)).astype(BF16)
    return jnp.dot(lower_ones, jnp.concatenate([hi, mid, lo], axis=0), preferred_element_type=F32)


def _log_sigmoid(z):
    return -(jnp.maximum(-z, 0.0) + jnp.log1p(jnp.exp(-jnp.abs(z))))


def _act(name, z):
    if name == "none":
        return z
    if name == "silu":
        return z * jax.nn.sigmoid(z)
    if name == "sigmoid":
        return jax.nn.sigmoid(z)
    if name == "tanh":
        return jnp.tanh(z)
    if name == "gla_gate":
        return _log_sigmoid(z) / GLA_TAU
    if name == "rwkv_decay":
        return -jnp.exp(_log_sigmoid(z) - 0.5)
    raise ValueError(name)


def _mm_body(*refs, act, has_bias, scale):
    if has_bias:
        x_ref, w_ref, b_ref, o_ref, wc_ref = refs
    else:
        x_ref, w_ref, o_ref, wc_ref = refs
        b_ref = None

    @pl.when(pl.program_id(1) == 0)
    def _():
        wc_ref[...] = w_ref[...].astype(BF16)

    acc = jnp.dot(x_ref[...], wc_ref[...], preferred_element_type=F32)
    if scale != 1.0:
        acc = acc * scale
    if has_bias:
        acc = acc + b_ref[...]
    o_ref[...] = _act(act, acc).astype(o_ref.dtype)


def _pick(n, pref):
    for t in pref:
        if n % t == 0:
            return t
    return n


MM_WEIGHT_TILE_ELEMS = 2 * 1024 * 1024


def matmul(x, w, slot=0, *, x_slot=None, act="none", bias=None, scale=1.0, out_dtype=F32):
    m, kdim = x.shape[-2:]
    _, kw, n = w.shape
    assert kw == kdim, (w.shape, x.shape)
    tm = _pick(m, (1024, 512, 256, 128, 64, 32, 16, 8))
    tn = _pick(n, tuple(t for t in (1024, 512, 256, 128) if t * kdim <= MM_WEIGHT_TILE_ELEMS))
    if x_slot is None:
        x_spec = pl.BlockSpec((tm, kdim), lambda j, i: (i, 0))
    else:
        x_spec = pl.BlockSpec((None, tm, kdim), lambda j, i: (x_slot, i, 0))
    in_specs = [x_spec, pl.BlockSpec((None, kdim, tn), lambda j, i: (slot, 0, j))]
    args = [x, w]
    if bias is not None:
        in_specs.append(pl.BlockSpec((1, tn), lambda j, i: (0, j)))
        args.append(bias.reshape(1, n).astype(F32))
    return pl.pallas_call(
        functools.partial(_mm_body, act=act, has_bias=bias is not None, scale=scale),
        grid=(n // tn, m // tm),
        in_specs=in_specs,
        out_specs=pl.BlockSpec((tm, tn), lambda j, i: (i, j)),
        out_shape=jax.ShapeDtypeStruct((m, n), out_dtype),
        scratch_shapes=[pltpu.VMEM((kdim, tn), BF16)],
        compiler_params=_cparams(("arbitrary", "arbitrary")),
        name="matmul_" + act,
    )(*args)


def _layer_norm_rows(z, g_ref, b_ref):
    mu = jnp.mean(z, -1, keepdims=True)
    zc = z - mu
    var = jnp.mean(zc * zc, -1, keepdims=True)
    return zc * lax.rsqrt(var + LN_EPS) * g_ref[...] + b_ref[...]


def _store_rows(y, oa_ref, ob_ref, head_tiles):
    if head_tiles is None:
        oa_ref[...] = y
        ob_ref[...] = y.astype(BF16)
    else:
        i = pl.program_id(0)

        @pl.when(i < head_tiles)
        def _():
            oa_ref[...] = y

        @pl.when(i >= head_tiles)
        def _():
            ob_ref[...] = y


def _ln_body(*refs, alpha, routed):
    if routed:
        x_ref, h_ref, g_ref, b_ref, router_ref, oa_ref, ob_ref, logit_ref = refs
    else:
        x_ref, h_ref, g_ref, b_ref, oa_ref, ob_ref = refs
    y = _layer_norm_rows(alpha * x_ref[...] + h_ref[...], g_ref, b_ref)
    if routed:
        logit_ref[...] = jnp.dot(y, router_ref[...], precision=_HI, preferred_element_type=F32)
    _store_rows(y, oa_ref, ob_ref, None)


LN_ROWS = 512


def _ln_out_specs(m, d, tm, split_rows):
    row = pl.BlockSpec((tm, d), lambda i, *_: (i, 0))
    if split_rows is None:
        return None, [row, row], [jax.ShapeDtypeStruct((m, d), F32),
                                  jax.ShapeDtypeStruct((m, d), BF16)]
    assert split_rows % tm == 0 and 0 < split_rows < m
    head_tiles = split_rows // tm
    specs = [pl.BlockSpec((tm, d), lambda i, *_: (jnp.minimum(i, head_tiles - 1), 0)),
             pl.BlockSpec((tm, d), lambda i, *_: (jnp.maximum(i - head_tiles, 0), 0))]
    shapes = [jax.ShapeDtypeStruct((split_rows, d), F32),
              jax.ShapeDtypeStruct((m - split_rows, d), F32)]
    return head_tiles, specs, shapes


def residual_layer_norm(x, h, g, b, alpha, router_w=None):
    m, d = x.shape
    tm = _pick(m, (LN_ROWS, 256, 128, 64, 32, 16, 8))
    row = pl.BlockSpec((tm, d), lambda i: (i, 0))
    vec = pl.BlockSpec((1, d), lambda i: (0, 0))
    _, out_specs, out_shape = _ln_out_specs(m, d, tm, None)
    r_specs, r_args = [], []
    if router_w is not None:
        n_logit = router_w.shape[1]
        r_specs, r_args = [pl.BlockSpec((d, n_logit), lambda i: (0, 0))], [router_w]
        out_specs = out_specs + [pl.BlockSpec((tm, n_logit), lambda i: (i, 0))]
        out_shape = out_shape + [jax.ShapeDtypeStruct((m, n_logit), F32)]
    return pl.pallas_call(
        functools.partial(_ln_body, alpha=alpha, routed=router_w is not None),
        grid=(m // tm,),
        in_specs=[row, row, vec, vec] + r_specs,
        out_specs=out_specs,
        out_shape=out_shape,
        compiler_params=_cparams(("parallel",)),
        name="residual_layer_norm",
    )(x, h, g.reshape(1, d), b.reshape(1, d), *r_args)


def _combine_ln_body(dest_ref, x_ref, ys_ref, gate_ref, g_ref, b_ref, oa_ref, ob_ref, buf_ref, sem_ref,
                     *, alpha, head_tiles, tm, top_k):
    i = pl.program_id(0)
    n_tiles = pl.num_programs(0)

    def issue(tile, slot):
        base = tile * (tm * top_k)

        def body(r, carry):
            for k in range(top_k):
                row = dest_ref[base + r * top_k + k]
                pltpu.make_async_copy(ys_ref.at[pl.ds(row, 1)], buf_ref.at[slot, k, pl.ds(r, 1)],
                                      sem_ref.at[slot]).start()
            return carry

        lax.fori_loop(0, tm, body, 0, unroll=8)

    @pl.when(i == 0)
    def _():
        issue(0, 0)

    @pl.when(i + 1 < n_tiles)
    def _():
        issue(i + 1, (i + 1) % 2)

    slot = i % 2
    for k in range(top_k):
        pltpu.make_async_copy(ys_ref.at[pl.ds(0, tm)], buf_ref.at[slot, k], sem_ref.at[slot]).wait()
    h = buf_ref[slot, 0] * gate_ref[:, 0:1]
    for k in range(1, top_k):
        h = h + buf_ref[slot, k] * gate_ref[:, k:k + 1]
    y = _layer_norm_rows(alpha * x_ref[...] + h, g_ref, b_ref)
    _store_rows(y, oa_ref, ob_ref, head_tiles)


def moe_combine_layer_norm(x, ys, dest, gates, g, b, alpha, split_rows=None):
    m, d = x.shape
    top_k = dest.shape[1]
    tm = LN_ROWS
    assert m % tm == 0 and ys.dtype == F32
    head_tiles, out_specs, out_shape = _ln_out_specs(m, d, tm, split_rows)
    row = pl.BlockSpec((tm, d), lambda i, *_: (i, 0))
    vec = pl.BlockSpec((1, d), lambda i, *_: (0, 0))
    return pl.pallas_call(
        functools.partial(_combine_ln_body, alpha=alpha, head_tiles=head_tiles, tm=tm, top_k=top_k),
        grid_spec=pltpu.PrefetchScalarGridSpec(
            num_scalar_prefetch=1,
            grid=(m // tm,),
            in_specs=[row, pl.BlockSpec(memory_space=pl.ANY),
                      pl.BlockSpec((tm, top_k), lambda i, *_: (i, 0)), vec, vec],
            out_specs=out_specs,
            scratch_shapes=[pltpu.VMEM((2, top_k, tm, d), F32), pltpu.SemaphoreType.DMA((2,))]),
        out_shape=out_shape,
        compiler_params=_cparams(("arbitrary",)),
        name="moe_combine_layer_norm",
    )(dest.reshape(-1).astype(jnp.int32), x, ys, gates, g.reshape(1, d), b.reshape(1, d))


SUBLANES = 8


def _shift_mix_body(x_ref, halo_ref, shift_ref, mu_ref, o_ref, *, tm, tp, ts, prompt_tiles):
    i = pl.program_id(0)
    x = x_ref[...]
    row = lax.broadcasted_iota(jnp.int32, (tm, 1), 0)
    prev = jnp.where(row == 0, halo_ref[SUBLANES - 1:SUBLANES, :], pltpu.roll(x, 1, 0))
    prompt_start = jnp.logical_and(row == 0, (i * tm) % tp == 0)
    prev_prompt = jnp.where(prompt_start, 0.0, prev)
    nseq = tm // ts
    carried = jnp.broadcast_to(shift_ref[...][:, None, :], (nseq, ts, x.shape[1])).reshape(x.shape)
    prev_sample = jnp.where(row % ts == 0, carried, prev)
    xx = jnp.where(i < prompt_tiles, prev_prompt, prev_sample) - x
    for j in range(o_ref.shape[0]):
        o_ref[j] = (x + xx * mu_ref[j:j + 1, :]).astype(BF16)


def shift_mix(x, shift0, mu, *, n_prompt_rows, tp, ts):
    m, d = x.shape
    nmix = mu.shape[0]
    tm = 512
    assert tp % tm == 0 and tm % ts == 0 and n_prompt_rows % tm == 0 and m % tm == 0
    prompt_tiles = n_prompt_rows // tm
    nseq = tm // ts
    assert shift0.shape[0] % nseq == 0
    halo_blocks = tm // SUBLANES
    return pl.pallas_call(
        functools.partial(_shift_mix_body, tm=tm, tp=tp, ts=ts, prompt_tiles=prompt_tiles),
        grid=(m // tm,),
        in_specs=[pl.BlockSpec((tm, d), lambda i: (i, 0)),
                  pl.BlockSpec((SUBLANES, d), lambda i: (jnp.maximum(i * halo_blocks - 1, 0), 0)),
                  pl.BlockSpec((nseq, d), lambda i: (jnp.maximum(i - prompt_tiles, 0), 0)),
                  pl.BlockSpec((nmix, d), lambda i: (0, 0))],
        out_specs=pl.BlockSpec((nmix, tm, d), lambda i: (0, i, 0)),
        out_shape=jax.ShapeDtypeStruct((nmix, m, d), BF16),
        compiler_params=_cparams(("parallel",)),
        name="rwkv_shift_mix",
    )(x, x, shift0, mu)


def _chunk_state_init(c, ncp, cps, state_ref, s0_ref):
    @pl.when(jnp.logical_and(c < ncp, c % cps == 0))
    def _():
        state_ref[...] = jnp.zeros(state_ref.shape, state_ref.dtype)

    @pl.when(c >= ncp)
    def _():
        state_ref[...] = s0_ref[0]


def _dla_body(*refs, heads, dk, dv, mode, ncp, cps):
    o_ref, sp_ref, ss_ref, st_ref = refs[-4:]
    refs = refs[:-4]
    if mode == "gla":
        q_ref, k_ref, v_ref, g_ref, gate_ref, ng_ref, s0_ref = refs
    else:
        q_ref, k_ref, v_ref, cos_ref, sin_ref, gate_ref, ng_ref, s0_ref = refs
    c = pl.program_id(0)
    _chunk_state_init(c, ncp, cps, st_ref, s0_ref)

    t_row = lax.broadcasted_iota(jnp.int32, (CHUNK, CHUNK), 0)
    t_col = lax.broadcasted_iota(jnp.int32, (CHUNK, CHUNK), 1)
    causal = t_row >= t_col
    if mode == "gla":
        cum_all = _chunk_prefix_sum(g_ref[...])
    else:
        width = heads * dk
        even = (lax.broadcasted_iota(jnp.int32, (CHUNK, width), 1) % 2) == 0
        q_all = q_ref[...]
        k_all = k_ref[...]
        q_sw = jnp.where(even, pltpu.roll(q_all, width - 1, 1), pltpu.roll(q_all, 1, 1))
        k_sw = jnp.where(even, pltpu.roll(k_all, width - 1, 1), pltpu.roll(k_all, 1, 1))
        cos = cos_ref[...]
        sin = sin_ref[...]
        frame = (lax.broadcasted_iota(jnp.int32, (CHUNK, 1), 0) + 1).astype(F32)

    for h in range(heads):
        ks = slice(h * dk, (h + 1) * dk)
        vs = slice(h * dv, (h + 1) * dv)
        v = v_ref[:, vs]
        s_prev = st_ref[h]
        if mode == "gla":
            q = q_ref[:, ks]
            k = k_ref[:, ks]
            cum = cum_all[:, ks]
            total = cum[CHUNK - 1:CHUNK, :]
            total_col = jnp.transpose(jnp.broadcast_to(total, (LANES, dk)))[:, :1]
            state_decay = jnp.exp(total_col)
            q_dec = q * jnp.exp(cum)
            k_inv = k * jnp.exp(-cum)
            k_tail = k * jnp.exp(total - cum)
        else:
            q = q_all[:, ks] * cos + q_sw[:, ks] * sin
            k = (k_all[:, ks] * cos + k_sw[:, ks] * sin) * (dk ** -0.5)
            log_gamma = math.log1p(-(2.0 ** (-5.0 - h)))
            cum = frame * log_gamma
            total = CHUNK * log_gamma
            state_decay = math.exp(total)
            q_dec = q * jnp.exp(cum)
            k_inv = k * jnp.exp(-cum)
            k_tail = k * jnp.exp(total - cum)
        scores = jnp.where(causal, _bdot_nt(q_dec, k_inv), 0.0)
        o = _bdot(q_dec, s_prev) + _bdot(scores, v)
        k_tail_t = jnp.transpose(k_tail)
        st_ref[h] = state_decay * s_prev + _bdot(k_tail_t, v)
        if mode == "gla":
            o = o * lax.rsqrt(jnp.mean(o * o, -1, keepdims=True) + GLA_NORM_EPS) * ng_ref[...]
        else:
            mu = jnp.mean(o, -1, keepdims=True)
            oc = o - mu
            var = jnp.mean(oc * oc, -1, keepdims=True)
            o = oc * lax.rsqrt(var + RET_GN_EPS) * ng_ref[:, vs]
        if mode == "gla":
            o_ref[:, vs] = (o * gate_ref[:, vs]).astype(BF16)
        else:
            o_ref[:, vs] = (gate_ref[:, vs] * o).astype(BF16)

    @pl.when(jnp.logical_and(c < ncp, c % cps == cps - 1))
    def _():
        sp_ref[0] = st_ref[...]

    @pl.when(c >= ncp)
    def _():
        ss_ref[0] = st_ref[...]


def _seq_state_specs(state_shape, ncp, cps, n_prompt, slot):
    blk = (None, 1) + tuple(state_shape)
    zeros = (0,) * len(state_shape)
    s0_spec = pl.BlockSpec(blk, lambda c: (slot, jnp.maximum(c - ncp, 0)) + zeros)
    sp_spec = pl.BlockSpec(blk, lambda c: (0, jnp.minimum(c // cps, n_prompt - 1)) + zeros)
    ss_spec = pl.BlockSpec(blk, lambda c: (0, jnp.maximum(c - ncp, 0)) + zeros)
    return s0_spec, sp_spec, ss_spec


def _state_out_shapes(n_prompt, n_sample, state_shape):
    return [jax.ShapeDtypeStruct((1, n_prompt) + tuple(state_shape), F32),
            jax.ShapeDtypeStruct((1, n_sample) + tuple(state_shape), F32)]


def decay_attention(q, k, v, extra, gate, norm_g, s0, slot, *, mode, heads, n_prompt, cps):
    nt = q.shape[0]
    dk = q.shape[1] // heads
    dv = v.shape[1] // heads
    n_sample = s0.shape[1]
    ncp = n_prompt * cps
    nchunks = nt // CHUNK
    assert nchunks == ncp + n_sample
    rowk = pl.BlockSpec((CHUNK, heads * dk), lambda c: (c, 0))
    rowv = pl.BlockSpec((CHUNK, heads * dv), lambda c: (c, 0))
    s0_spec, sp_spec, ss_spec = _seq_state_specs((heads, dk, dv), ncp, cps, n_prompt, slot)
    if mode == "gla":
        extra_specs = [rowk]
        ng_spec = pl.BlockSpec((1, dv), lambda c: (0, 0))
        norm_g = norm_g.reshape(1, dv)
    else:
        pos_spec = pl.BlockSpec((CHUNK, dk), lambda c: (jnp.where(c < ncp, c % cps, cps), 0))
        extra_specs = [pos_spec, pos_spec]
        ng_spec = pl.BlockSpec((1, heads * dv), lambda c: (0, 0))
        norm_g = norm_g.reshape(1, heads * dv)
    return pl.pallas_call(
        functools.partial(_dla_body, heads=heads, dk=dk, dv=dv, mode=mode, ncp=ncp, cps=cps),
        grid=(nchunks,),
        in_specs=[rowk, rowk, rowv] + extra_specs + [rowv, ng_spec, s0_spec],
        out_specs=[rowv, sp_spec, ss_spec],
        out_shape=([jax.ShapeDtypeStruct((nt, heads * dv), BF16)]
                   + _state_out_shapes(n_prompt, n_sample, (heads, dk, dv))),
        scratch_shapes=[pltpu.VMEM((heads, dk, dv), F32)],
        compiler_params=_cparams(("arbitrary",)),
        name="decay_attention_" + mode,
    )(q, k, v, *extra, gate, norm_g, s0)


def _rwkv_body(r_ref, k_ref, v_ref, lw_ref, a_ref, g_ref, kk_ref, ka_ref, rk_ref, gng_ref, gnb_ref,
               s0_ref, o_ref, sp_ref, ss_ref, st_ref, *, pairs, ncp, cps):
    c = pl.program_id(0)
    n = RWKV_HEAD
    w2 = 2 * n

    @pl.when(jnp.logical_and(c < ncp, c % cps == 0))
    def _():
        st_ref[...] = jnp.zeros(st_ref.shape, F32)

    @pl.when(c >= ncp)
    def _():
        zero = jnp.zeros((n, n), F32)
        for p in range(pairs):
            top = jnp.concatenate([s0_ref[0, 2 * p], zero], axis=1)
            bot = jnp.concatenate([zero, s0_ref[0, 2 * p + 1]], axis=1)
            st_ref[p] = jnp.concatenate([top, bot], axis=0)

    def paired(ref):
        x = ref[...]
        return jnp.stack([x[:, p * w2:(p + 1) * w2] for p in range(pairs)])

    lane = lax.broadcasted_iota(jnp.int32, (1, 1, w2), 2)
    first = lane < n

    def head_sum(x):
        s_a = jnp.sum(jnp.where(first, x, 0.0), -1, keepdims=True)
        s_b = jnp.sum(jnp.where(first, 0.0, x), -1, keepdims=True)
        return jnp.where(first, s_a, s_b)

    def stacked(x):
        return jnp.concatenate([jnp.where(first, x, 0.0), jnp.where(first, 0.0, x)], axis=1)

    def bmm(a, b):
        return lax.dot_general(a.astype(BF16), b.astype(BF16), (((2,), (1,)), ((0,), (0,))),
                               preferred_element_type=F32)

    def bmm_nt(a, b):
        return lax.dot_general(a.astype(BF16), b.astype(BF16), (((2,), (2,)), ((0,), (0,))),
                               preferred_element_type=F32)

    r = paired(r_ref)
    k_raw = paired(k_ref)
    v = paired(v_ref)
    lw = paired(lw_ref)
    a = paired(a_ref)
    k_k = paired(kk_ref)
    k_a = paired(ka_ref)
    r_k = paired(rk_ref)

    kk = k_raw * k_k
    kk = kk / jnp.maximum(jnp.sqrt(head_sum(kk * kk)), 1e-12)
    k_h = k_raw * (1.0 + (a - 1.0) * k_a)
    a_vec = -kk
    b_vec = kk * a

    cum = paired_value(_chunk_prefix_sum(lw_ref[...]), pairs, w2)
    total = cum[:, CHUNK - 1:CHUNK, :]
    p_now = jnp.exp(cum)
    p_prev = jnp.exp(cum - lw)
    p_inv = jnp.exp(-cum)
    p_tail = jnp.exp(total - cum)

    lhs = jnp.concatenate([stacked(a_vec * p_prev), stacked(r * p_now)], axis=1)
    rhs = jnp.concatenate([stacked(b_vec * p_inv), stacked(k_h * p_inv)], axis=1)
    sc = bmm_nt(lhs, rhs)
    s_prev = st_ref[...]
    sr = bmm_nt(lhs, s_prev)

    i_row = lax.broadcasted_iota(jnp.int32, (1, w2, w2), 1)
    i_col = lax.broadcasted_iota(jnp.int32, (1, w2, w2), 2)
    same = (i_row // n) == (i_col // n)
    strict = jnp.logical_and(same, (i_col % n) < (i_row % n))
    incl = jnp.logical_and(same, (i_col % n) <= (i_row % n))
    a_ab = jnp.where(strict, sc[:, :w2, :w2], 0.0)
    a_ak = jnp.where(strict, sc[:, :w2, w2:], 0.0)
    a_rb = jnp.where(incl, sc[:, w2:, :w2], 0.0)
    a_rk = jnp.where(incl, sc[:, w2:, w2:], 0.0)
    u0 = sr[:, :w2]
    y0 = sr[:, w2:]
    v_st = stacked(v)

    eye = (i_row == i_col).astype(F32)
    t_inv = eye + a_ab
    power = a_ab
    span = 1
    while 2 * span < CHUNK:
        power = bmm(power, power)
        t_inv = t_inv + bmm(t_inv, power)
        span *= 2

    u_st = bmm(t_inv, u0 + bmm(a_ak, v_st))
    uv = jnp.concatenate([u_st, v_st], axis=1)
    y_st = y0 + bmm(jnp.concatenate([a_rb, a_rk], axis=2), uv)
    y = y_st[:, :n] + y_st[:, n:]

    tails = jnp.concatenate([stacked(b_vec * p_tail), stacked(k_h * p_tail)], axis=1)
    uv_t = jnp.swapaxes(uv, 1, 2)
    s_new = s_prev * jnp.exp(total) + bmm(uv_t, tails)
    st_ref[...] = s_new

    mu = head_sum(y) * (1.0 / n)
    yc = y - mu
    var = head_sum(yc * yc) * (1.0 / n)
    yn = yc * lax.rsqrt(var + RWKV_GN_EPS) * paired(gng_ref) + paired(gnb_ref)
    out = yn + head_sum(r * k_h * r_k) * v
    gate = paired(g_ref)
    for p in range(pairs):
        o_ref[:, p * w2:(p + 1) * w2] = (out[p] * gate[p]).astype(BF16)

    def store_state(dst_ref):
        for p in range(pairs):
            dst_ref[0, 2 * p] = s_new[p, :n, :n]
            dst_ref[0, 2 * p + 1] = s_new[p, n:, n:]

    @pl.when(jnp.logical_and(c < ncp, c % cps == cps - 1))
    def _():
        store_state(sp_ref)

    @pl.when(c >= ncp)
    def _():
        store_state(ss_ref)


def paired_value(x, pairs, w2):
    return jnp.stack([x[:, p * w2:(p + 1) * w2] for p in range(pairs)])


def rwkv7_attention(r, k, v, lw, a, g, k_k, k_a, r_k, gn_g, gn_b, s0, slot, *, n_prompt, cps):
    nt, d = r.shape
    heads = d // RWKV_HEAD
    pairs = heads // 2
    n_sample = s0.shape[1]
    ncp = n_prompt * cps
    assert nt // CHUNK == ncp + n_sample
    row = pl.BlockSpec((CHUNK, d), lambda c: (c, 0))
    vec = pl.BlockSpec((1, d), lambda c: (0, 0))
    s0_spec, sp_spec, ss_spec = _seq_state_specs((heads, RWKV_HEAD, RWKV_HEAD), ncp, cps, n_prompt,
                                                 slot)
    vecs = [u.reshape(1, d) for u in (k_k, k_a, r_k, gn_g, gn_b)]
    return pl.pallas_call(
        functools.partial(_rwkv_body, pairs=pairs, ncp=ncp, cps=cps),
        grid=(nt // CHUNK,),
        in_specs=[row] * 6 + [vec] * 5 + [s0_spec],
        out_specs=[row, sp_spec, ss_spec],
        out_shape=([jax.ShapeDtypeStruct((nt, d), BF16)]
                   + _state_out_shapes(n_prompt, n_sample, (heads, RWKV_HEAD, RWKV_HEAD))),
        scratch_shapes=[pltpu.VMEM((pairs, 2 * RWKV_HEAD, 2 * RWKV_HEAD), F32)],
        compiler_params=_cparams(("arbitrary",)),
        name="rwkv7_attention",
    )(r, k, v, lw, a, g, *vecs, s0)


def _expert_weight_pipeline(be_ref, chg_ref, nxt_ref, hbm_refs, stage_refs, cache_refs, sem_ref,
                            tile):
    j = pl.program_id(0)
    b = pl.program_id(1)
    n_pass = pl.num_programs(0)

    def copies(expert, col_pass):
        col = pl.multiple_of(col_pass * tile, tile)
        return [pltpu.make_async_copy(hbm.at[expert, :, pl.ds(col, tile)], stage, sem_ref.at[i])
                for i, (hbm, stage) in enumerate(zip(hbm_refs, stage_refs))]

    @pl.when(jnp.logical_and(j == 0, b == 0))
    def _():
        for cp in copies(be_ref[0], 0):
            cp.start()

    @pl.when(chg_ref[b] == 1)
    def _():
        for cp in copies(be_ref[b], j):
            cp.wait()
        for stage, cache in zip(stage_refs, cache_refs):
            cache[...] = stage[...].astype(BF16)
        nxt = nxt_ref[b]

        @pl.when(nxt >= 0)
        def _():
            for cp in copies(be_ref[jnp.maximum(nxt, 0)], j):
                cp.start()

        @pl.when(jnp.logical_and(nxt < 0, j + 1 < n_pass))
        def _():
            for cp in copies(be_ref[0], j + 1):
                cp.start()


def _gate_up_body(be_ref, chg_ref, nxt_ref, nu_ref, x_ref, w1_ref, w3_ref, h_ref,
                  w1s_ref, w3s_ref, w1c_ref, w3c_ref, sem_ref, *, tile):
    b = pl.program_id(1)
    _expert_weight_pipeline(be_ref, chg_ref, nxt_ref, (w1_ref, w3_ref), (w1s_ref, w3s_ref),
                            (w1c_ref, w3c_ref), sem_ref, tile)

    @pl.when(b < nu_ref[0])
    def _():
        x = x_ref[...]
        gate = jnp.dot(x, w1c_ref[...], preferred_element_type=F32)
        up = jnp.dot(x, w3c_ref[...], preferred_element_type=F32)
        h_ref[...] = (gate * jax.nn.sigmoid(gate) * up).astype(BF16)

    @pl.when(b >= nu_ref[0])
    def _():
        h_ref[...] = jnp.zeros(h_ref.shape, BF16)


def _down_body(be_ref, chg_ref, nxt_ref, nu_ref, h_ref, w2_ref, y_ref, w2s_ref, w2c_ref, sem_ref, *,
               tile):
    b = pl.program_id(1)
    _expert_weight_pipeline(be_ref, chg_ref, nxt_ref, (w2_ref,), (w2s_ref,), (w2c_ref,), sem_ref,
                            tile)

    @pl.when(b < nu_ref[0])
    def _():
        y_ref[...] = jnp.dot(h_ref[...], w2c_ref[...],
                             preferred_element_type=F32).astype(y_ref.dtype)

    @pl.when(b >= nu_ref[0])
    def _():
        y_ref[...] = jnp.zeros(y_ref.shape, y_ref.dtype)


def _block_meta(block_expert, n_used):
    be = block_expert.astype(jnp.int32)
    nb = be.shape[0]
    changed = jnp.concatenate([jnp.ones((1,), jnp.int32),
                               (be[1:] != be[:-1]).astype(jnp.int32)])
    idx = jnp.arange(nb, dtype=jnp.int32)
    opener = jnp.where(changed == 1, idx, nb)
    after = lax.cummin(jnp.concatenate([opener[1:], jnp.full((1,), nb, jnp.int32)]), reverse=True)
    nxt = jnp.where(after >= nb, -1, after).astype(jnp.int32)
    return be, changed, nxt, jnp.reshape(n_used, (1,)).astype(jnp.int32)


def swiglu_gate_up(xs, block_expert, n_used, w1, w3, block, tile_f=(512, 256, 128)):
    rows, d = xs.shape
    f = w1.shape[-1]
    tf = _pick(f, tile_f)
    hbm = pl.BlockSpec(memory_space=pl.ANY)
    return pl.pallas_call(
        functools.partial(_gate_up_body, tile=tf),
        grid_spec=pltpu.PrefetchScalarGridSpec(
            num_scalar_prefetch=4,
            grid=(f // tf, rows // block),
            in_specs=[pl.BlockSpec((block, d), lambda j, b, *_: (b, 0)), hbm, hbm],
            out_specs=pl.BlockSpec((block, tf), lambda j, b, *_: (b, j)),
            scratch_shapes=[pltpu.VMEM((d, tf), F32), pltpu.VMEM((d, tf), F32),
                            pltpu.VMEM((d, tf), BF16), pltpu.VMEM((d, tf), BF16),
                            pltpu.SemaphoreType.DMA((2,))]),
        out_shape=jax.ShapeDtypeStruct((rows, f), BF16),
        compiler_params=_cparams(("arbitrary", "arbitrary")),
        name="swiglu_gate_up",
    )(*_block_meta(block_expert, n_used), xs, w1, w3)


def swiglu_down(h, block_expert, n_used, w2, block, out_dtype=F32):
    rows, f = h.shape
    d = w2.shape[-1]
    tn = _pick(d, (512, 256, 128))
    return pl.pallas_call(
        functools.partial(_down_body, tile=tn),
        grid_spec=pltpu.PrefetchScalarGridSpec(
            num_scalar_prefetch=4,
            grid=(d // tn, rows // block),
            in_specs=[pl.BlockSpec((block, f), lambda j, b, *_: (b, 0)),
                      pl.BlockSpec(memory_space=pl.ANY)],
            out_specs=pl.BlockSpec((block, tn), lambda j, b, *_: (b, j)),
            scratch_shapes=[pltpu.VMEM((f, tn), F32), pltpu.VMEM((f, tn), BF16),
                            pltpu.SemaphoreType.DMA((1,))]),
        out_shape=jax.ShapeDtypeStruct((rows, d), out_dtype),
        compiler_params=_cparams(("arbitrary", "arbitrary")),
        name="swiglu_down",
    )(*_block_meta(block_expert, n_used), h, w2)


def _dispatch_body(src_ref, nu_ref, x_ref, o_ref, buf_ref, sem_ref, *, block):
    b = pl.program_id(0)
    n_used = nu_ref[0]

    def issue(blk, slot):
        base = blk * block

        def body(r, carry):
            pltpu.make_async_copy(x_ref.at[pl.ds(src_ref[base + r], 1)],
                                  buf_ref.at[slot, pl.ds(r, 1)], sem_ref.at[slot]).start()
            return carry

        lax.fori_loop(0, block, body, 0, unroll=8)

    @pl.when(jnp.logical_and(b == 0, n_used > 0))
    def _():
        issue(0, 0)

    @pl.when(b + 1 < n_used)
    def _():
        issue(b + 1, (b + 1) % 2)

    @pl.when(b < n_used)
    def _():
        slot = b % 2
        pltpu.make_async_copy(x_ref.at[pl.ds(0, block)], buf_ref.at[slot], sem_ref.at[slot]).wait()
        o_ref[...] = buf_ref[slot].astype(BF16)

    @pl.when(b >= n_used)
    def _():
        o_ref[...] = jnp.zeros(o_ref.shape, BF16)


def moe_dispatch(x, src_tok, n_used, block):
    n, d = x.shape
    rows = src_tok.shape[0]
    return pl.pallas_call(
        functools.partial(_dispatch_body, block=block),
        grid_spec=pltpu.PrefetchScalarGridSpec(
            num_scalar_prefetch=2,
            grid=(rows // block,),
            in_specs=[pl.BlockSpec(memory_space=pl.ANY)],
            out_specs=pl.BlockSpec((block, d), lambda b, *_: (b, 0)),
            scratch_shapes=[pltpu.VMEM((2, block, d), F32), pltpu.SemaphoreType.DMA((2,))]),
        out_shape=jax.ShapeDtypeStruct((rows, d), BF16),
        compiler_params=_cparams(("arbitrary",)),
        name="moe_dispatch",
    )(src_tok.astype(jnp.int32), jnp.reshape(n_used, (1,)).astype(jnp.int32), x)


def _pad_cols(w, mult=LANES):
    pad = (-w.shape[-1]) % mult
    return jnp.pad(w, [(0, 0)] * (w.ndim - 1) + [(0, pad)]) if pad else w


def _pad_rows(w, mult=LANES):
    pad = (-w.shape[-2]) % mult
    return jnp.pad(w, [(0, 0)] * (w.ndim - 2) + [(0, pad), (0, 0)]) if pad else w


DENSE_GATE_UP_BLOCK = 1024
DENSE_DOWN_BLOCK = 512
MOE_GATE_UP_TILE_F = (1024, 512, 256, 128)


def dense_swiglu(xb, w1, w3, w2, slot):
    rows = xb.shape[0]
    assert rows % DENSE_GATE_UP_BLOCK == 0 and rows % DENSE_DOWN_BLOCK == 0
    nb_a = rows // DENSE_GATE_UP_BLOCK
    nb_b = rows // DENSE_DOWN_BLOCK
    h = swiglu_gate_up(xb, jnp.full((nb_a,), slot, jnp.int32), jnp.int32(nb_a), w1, w3,
                       DENSE_GATE_UP_BLOCK)
    return swiglu_down(h, jnp.full((nb_b,), slot, jnp.int32), jnp.int32(nb_b), w2, DENSE_DOWN_BLOCK)


def moe_swiglu(logits, xf, w1, w3, w2, slot):
    n, d = xf.shape
    n_exp = logits.shape[-1]
    top_val, top_idx = lax.top_k(logits, MOE_TOP_K)
    gates = jax.nn.softmax(top_val, axis=-1)
    slots = n * MOE_TOP_K
    flat_e = top_idx.reshape(-1)
    onehot = (flat_e[:, None] == jnp.arange(n_exp, dtype=flat_e.dtype)[None, :]).astype(jnp.int32)
    rank = jnp.sum((jnp.cumsum(onehot, axis=0) - onehot) * onehot, axis=1)
    counts = jnp.sum(onehot, axis=0)
    start = jnp.cumsum(counts) - counts
    padded = (counts + MOE_BLOCK - 1) // MOE_BLOCK * MOE_BLOCK
    pend = jnp.cumsum(padded)
    pstart = pend - padded
    nb = (slots + n_exp * (MOE_BLOCK - 1) + MOE_BLOCK - 1) // MOE_BLOCK
    dest = (pstart[flat_e] + rank).astype(jnp.int32)
    block_start = jnp.arange(nb, dtype=jnp.int32) * MOE_BLOCK
    expert_of_block = jnp.clip(jnp.searchsorted(pend, block_start, side="right"), 0, n_exp - 1)
    order = jnp.argsort(flat_e, stable=True)
    e_row = jnp.repeat(expert_of_block, MOE_BLOCK)
    rank_row = jnp.arange(nb * MOE_BLOCK, dtype=jnp.int32) - pstart[e_row]
    valid = rank_row < counts[e_row]
    slot_row = order[jnp.clip(start[e_row] + rank_row, 0, slots - 1)]
    src_tok = jnp.where(valid, slot_row // MOE_TOP_K, 0).astype(jnp.int32)
    block_expert = expert_of_block.astype(jnp.int32) + slot * n_exp
    n_used = pend[-1] // MOE_BLOCK
    xs = moe_dispatch(xf, src_tok, n_used, MOE_BLOCK)
    e1 = w1.reshape((-1,) + w1.shape[2:])
    e3 = w3.reshape((-1,) + w3.shape[2:])
    e2 = w2.reshape((-1,) + w2.shape[2:])
    hs = swiglu_gate_up(xs, block_expert, n_used, e1, e3, MOE_BLOCK, tile_f=MOE_GATE_UP_TILE_F)
    ys = swiglu_down(hs, block_expert, n_used, e2, MOE_BLOCK)
    return ys, dest.reshape(n, MOE_TOP_K), gates


def _rope_tables(dk, cps, past_len):
    half = dk // 2
    inv_freq = 1.0 / (RET_ROPE_BASE ** jnp.linspace(0.0, 1.0, half, dtype=F32))
    pos = jnp.concatenate([jnp.arange(cps * CHUNK, dtype=jnp.int32),
                           past_len + jnp.arange(CHUNK, dtype=jnp.int32)])
    ang = pos.astype(F32)[:, None] * inv_freq[None, :]
    cos = jnp.repeat(jnp.cos(ang), 2, axis=1)
    sin = jnp.stack([-jnp.sin(ang), jnp.sin(ang)], axis=-1).reshape(pos.shape[0], dk)
    return cos, sin


def kernel(x_prompt, x_sample, state_gla, state_rwkv, state_shift, state_ret, ln_g, ln_b,
           gla_wq, gla_wk, gla_wv, gla_wr, gla_wa1, gla_wa2, gla_ba, gla_norm_g, gla_wo,
           rwkv_mu, rwkv_wr, rwkv_wk, rwkv_wv, rwkv_wo, rwkv_w0, rwkv_w1, rwkv_w2, rwkv_a0, rwkv_a1,
           rwkv_a2, rwkv_g1, rwkv_g2, rwkv_k_k, rwkv_k_a, rwkv_r_k, rwkv_gn_g, rwkv_gn_b,
           ret_wq, ret_wk, ret_wv, ret_wg, ret_gn_g, ret_wo,
           ffn_w1, ffn_w3, ffn_w2, moe_router, moe_w1, moe_w3, moe_w2):
    bp, tp, d = x_prompt.shape
    bs, ts, _ = x_sample.shape
    assert ts == CHUNK and tp % CHUNK == 0
    depth = ln_g.shape[0]
    alpha = (2.0 * depth) ** 0.25
    cps = tp // CHUNK
    past_len = tp
    n_prompt_rows = bp * tp
    seq = dict(n_prompt=bp, cps=cps)

    gla_heads = state_gla.shape[2]
    gla_dk = state_gla.shape[3]
    ret_heads = state_ret.shape[2]
    ret_dk = state_ret.shape[3]

    xf = jnp.concatenate([x_prompt.reshape(-1, d), x_sample.reshape(-1, d)], axis=0)
    xb = xf.astype(BF16)

    gla_states, rwkv_states, ret_states = [], [], []
    new_shift_p, new_shift_s = [], []

    def stacked(per_layer, which):
        parts = [states[which] for states in per_layer]
        return parts[0] if len(parts) == 1 else jnp.concatenate(parts, axis=0)

    last_rows = jnp.concatenate([jnp.arange(bp, dtype=jnp.int32) * tp + (tp - 1),
                                 n_prompt_rows + jnp.arange(bs, dtype=jnp.int32) * ts + (ts - 1)])
    for i in range(depth):
        kind, slot = i % 3, i // 3
        if kind == 0:
            q = matmul(xb, gla_wq, slot, scale=gla_dk ** -0.5)
            k = matmul(xb, gla_wk, slot)
            v = matmul(xb, gla_wv, slot)
            gate = matmul(xb, gla_wr, slot, act="silu")
            low = matmul(xb, _pad_cols(gla_wa1), slot, out_dtype=BF16)
            log_alpha = matmul(low, _pad_rows(gla_wa2), slot, act="gla_gate", bias=gla_ba[slot])
            o, *states = decay_attention(q, k, v, (log_alpha,), gate, gla_norm_g[slot], state_gla,
                                         slot, mode="gla", heads=gla_heads, **seq)
            gla_states.append(states)
            h = matmul(o, gla_wo, slot)
        elif kind == 1:
            mixes = shift_mix(xf, state_shift[slot], rwkv_mu[slot], n_prompt_rows=n_prompt_rows,
                              tp=tp, ts=ts)
            i_r, i_w, i_k, i_v, i_a, i_g = range(6)
            r = matmul(mixes, rwkv_wr, slot, x_slot=i_r)
            k = matmul(mixes, rwkv_wk, slot, x_slot=i_k)
            v = matmul(mixes, rwkv_wv, slot, x_slot=i_v)
            w_mid = matmul(mixes, _pad_cols(rwkv_w1), slot, x_slot=i_w, act="tanh", out_dtype=BF16)
            log_decay = matmul(w_mid, _pad_rows(rwkv_w2), slot, act="rwkv_decay", bias=rwkv_w0[slot])
            a_mid = matmul(mixes, _pad_cols(rwkv_a1), slot, x_slot=i_a, out_dtype=BF16)
            a = matmul(a_mid, _pad_rows(rwkv_a2), slot, act="sigmoid", bias=rwkv_a0[slot])
            g_mid = matmul(mixes, _pad_cols(rwkv_g1), slot, x_slot=i_g, act="sigmoid", out_dtype=BF16)
            g = matmul(g_mid, _pad_rows(rwkv_g2), slot)
            o, *states = rwkv7_attention(r, k, v, log_decay, a, g, rwkv_k_k[slot], rwkv_k_a[slot],
                                         rwkv_r_k[slot].reshape(-1), rwkv_gn_g[slot],
                                         rwkv_gn_b[slot], state_rwkv, slot, **seq)
            rwkv_states.append(states)
            h = matmul(o, rwkv_wo, slot)
            ends = jnp.take(xf, last_rows, axis=0, mode="clip")
            new_shift_p.append(ends[:bp])
            new_shift_s.append(ends[bp:])
        else:
            q = matmul(xb, ret_wq, slot)
            k = matmul(xb, ret_wk, slot)
            v = matmul(xb, ret_wv, slot)
            gate = matmul(xb, ret_wg, slot, act="silu")
            cos, sin = _rope_tables(ret_dk, cps, past_len)
            o, *states = decay_attention(q, k, v, (cos, sin), gate, ret_gn_g[slot], state_ret, slot,
                                         mode="ret", heads=ret_heads, **seq)
            ret_states.append(states)
            h = matmul(o, ret_wo, slot)
        fslot = i // 2
        is_moe = i % 2 == 1
        router_w = _pad_cols(moe_router[fslot]) if is_moe else None
        xf, xb, *logits = residual_layer_norm(xf, h, ln_g[i, 0], ln_b[i, 0], alpha, router_w=router_w)
        last = i == depth - 1
        if is_moe:
            ys, dest, gates = moe_swiglu(logits[0][:, :moe_router.shape[-1]], xf, moe_w1, moe_w3,
                                         moe_w2, fslot)
            out_a, out_b = moe_combine_layer_norm(xf, ys, dest, gates, ln_g[i, 1], ln_b[i, 1], alpha,
                                                  split_rows=n_prompt_rows if last else None)
        else:
            h = dense_swiglu(xb, ffn_w1, ffn_w3, ffn_w2, fslot)
            out_a, out_b = residual_layer_norm(xf, h, ln_g[i, 1], ln_b[i, 1], alpha)
            if last:
                out_a, out_b = out_a[:n_prompt_rows], out_a[n_prompt_rows:]
        if last:
            y_prompt, y_sample = out_a.reshape(bp, tp, d), out_b.reshape(bs, ts, d)
        else:
            xf, xb = out_a, out_b

    return (y_prompt, y_sample,
            stacked(gla_states, 0), stacked(rwkv_states, 0), jnp.stack(new_shift_p),
            stacked(ret_states, 0),
            stacked(gla_states, 1), stacked(rwkv_states, 1), jnp.stack(new_shift_s),
            stacked(ret_states, 1))
```

```python
import functools
import math

import jax
import jax.numpy as jnp
from jax import lax
from jax.experimental import pallas as pl
from jax.experimental.pallas import tpu as pltpu

F32 = jnp.float32
BF16 = jnp.bfloat16

CHUNK = 64
LANES = 128
VMEM_LIMIT_BYTES = 56 * 1024 * 1024

LN_EPS = 1e-5
GLA_TAU = 16.0
GLA_NORM_EPS = 1e-5
RWKV_HEAD = 64
RWKV_GN_EPS = 64e-5
RET_ROPE_BASE = 10000.0
RET_GN_EPS = 1e-5
MOE_TOP_K = 2
SUBLAYER_OUT_DTYPE = jnp.bfloat16
MOE_BLOCK = 512

_HI = lax.Precision.HIGHEST


def _cparams(sem):
    return pltpu.CompilerParams(dimension_semantics=sem, vmem_limit_bytes=VMEM_LIMIT_BYTES)


def _bdot(a, b):
    return jnp.dot(a.astype(BF16), b.astype(BF16), preferred_element_type=F32)


def _bdot_nt(a, b):
    return lax.dot_general(a.astype(BF16), b.astype(BF16), (((1,), (1,)), ((), ())),
                           preferred_element_type=F32)


def _chunk_prefix_sum(x):
    t_row = lax.broadcasted_iota(jnp.int32, (CHUNK, 3 * CHUNK), 0)
    t_col = lax.broadcasted_iota(jnp.int32, (CHUNK, 3 * CHUNK), 1) % CHUNK
    lower_ones = (t_row >= t_col).astype(BF16)
    hi = x.astype(BF16)
    rest = x - hi.astype(F32)
    mid = rest.astype(BF16)
    lo = (rest - mid.astype(F32)).astype(BF16)
    return jnp.dot(lower_ones, jnp.concatenate([hi, mid, lo], axis=0), preferred_element_type=F32)


def _log_sigmoid(z):
    return -(jnp.maximum(-z, 0.0) + jnp.log1p(jnp.exp(-jnp.abs(z))))


def _act(name, z):
    if name == "none":
        return z
    if name == "silu":
        return z * jax.nn.sigmoid(z)
    if name == "sigmoid":
        return jax.nn.sigmoid(z)
    if name == "tanh":
        return jnp.tanh(z)
    if name == "gla_gate":
        return _log_sigmoid(z) / GLA_TAU
    if name == "rwkv_decay":
        return -jnp.exp(_log_sigmoid(z) - 0.5)
    raise ValueError(name)


def _mm_body(*refs, act, has_bias, scale):
    if has_bias:
        x_ref, w_ref, b_ref, o_ref, wc_ref = refs
    else:
        x_ref, w_ref, o_ref, wc_ref = refs
        b_ref = None

    @pl.when(pl.program_id(1) == 0)
    def _():
        wc_ref[...] = w_ref[...].astype(BF16)

    acc = jnp.dot(x_ref[...], wc_ref[...], preferred_element_type=F32)
    if scale != 1.0:
        acc = acc * scale
    if has_bias:
        acc = acc + b_ref[...]
    o_ref[...] = _act(act, acc).astype(o_ref.dtype)


def _pick(n, pref):
    for t in pref:
        if n % t == 0:
            return t
    return n


MM_WEIGHT_TILE_ELEMS = 2 * 1024 * 1024


def matmul(x, w, slot=0, *, x_slot=None, act="none", bias=None, scale=1.0, out_dtype=F32):
    m, kdim = x.shape[-2:]
    _, kw, n = w.shape
    assert kw == kdim, (w.shape, x.shape)
    tm = _pick(m, (1024, 512, 256, 128, 64, 32, 16, 8))
    tn = _pick(n, tuple(t for t in (1024, 512, 256, 128) if t * kdim <= MM_WEIGHT_TILE_ELEMS))
    if x_slot is None:
        x_spec = pl.BlockSpec((tm, kdim), lambda j, i: (i, 0))
    else:
        x_spec = pl.BlockSpec((None, tm, kdim), lambda j, i: (x_slot, i, 0))
    in_specs = [x_spec, pl.BlockSpec((None, kdim, tn), lambda j, i: (slot, 0, j))]
    args = [x, w]
    if bias is not None:
        in_specs.append(pl.BlockSpec((1, tn), lambda j, i: (0, j)))
        args.append(bias.reshape(1, n).astype(F32))
    return pl.pallas_call(
        functools.partial(_mm_body, act=act, has_bias=bias is not None, scale=scale),
        grid=(n // tn, m // tm),
        in_specs=in_specs,
        out_specs=pl.BlockSpec((tm, tn), lambda j, i: (i, j)),
        out_shape=jax.ShapeDtypeStruct((m, n), out_dtype),
        scratch_shapes=[pltpu.VMEM((kdim, tn), BF16)],
        compiler_params=_cparams(("arbitrary", "arbitrary")),
        name="matmul_" + act,
    )(*args)


def _layer_norm_rows(z, g_ref, b_ref):
    mu = jnp.mean(z, -1, keepdims=True)
    zc = z - mu
    var = jnp.mean(zc * zc, -1, keepdims=True)
    return zc * lax.rsqrt(var + LN_EPS) * g_ref[...] + b_ref[...]


def _store_rows(y, oa_ref, ob_ref, head_tiles):
    if head_tiles is None:
        oa_ref[...] = y
        ob_ref[...] = y.astype(BF16)
    else:
        i = pl.program_id(0)

        @pl.when(i < head_tiles)
        def _():
            oa_ref[...] = y

        @pl.when(i >= head_tiles)
        def _():
            ob_ref[...] = y


def _ln_body(*refs, alpha, routed):
    if routed:
        x_ref, h_ref, g_ref, b_ref, router_ref, oa_ref, ob_ref, logit_ref = refs
    else:
        x_ref, h_ref, g_ref, b_ref, oa_ref, ob_ref = refs
    y = _layer_norm_rows(alpha * x_ref[...] + h_ref[...].astype(F32), g_ref, b_ref)
    if routed:
        logit_ref[...] = jnp.dot(y, router_ref[...], precision=_HI, preferred_element_type=F32)
    _store_rows(y, oa_ref, ob_ref, None)


LN_ROWS = 512


def _ln_out_specs(m, d, tm, split_rows):
    row = pl.BlockSpec((tm, d), lambda i, *_: (i, 0))
    if split_rows is None:
        return None, [row, row], [jax.ShapeDtypeStruct((m, d), F32),
                                  jax.ShapeDtypeStruct((m, d), BF16)]
    assert split_rows % tm == 0 and 0 < split_rows < m
    head_tiles = split_rows // tm
    specs = [pl.BlockSpec((tm, d), lambda i, *_: (jnp.minimum(i, head_tiles - 1), 0)),
             pl.BlockSpec((tm, d), lambda i, *_: (jnp.maximum(i - head_tiles, 0), 0))]
    shapes = [jax.ShapeDtypeStruct((split_rows, d), F32),
              jax.ShapeDtypeStruct((m - split_rows, d), F32)]
    return head_tiles, specs, shapes


def residual_layer_norm(x, h, g, b, alpha, router_w=None):
    m, d = x.shape
    tm = _pick(m, (LN_ROWS, 256, 128, 64, 32, 16, 8))
    row = pl.BlockSpec((tm, d), lambda i: (i, 0))
    vec = pl.BlockSpec((1, d), lambda i: (0, 0))
    _, out_specs, out_shape = _ln_out_specs(m, d, tm, None)
    r_specs, r_args = [], []
    if router_w is not None:
        n_logit = router_w.shape[1]
        r_specs, r_args = [pl.BlockSpec((d, n_logit), lambda i: (0, 0))], [router_w]
        out_specs = out_specs + [pl.BlockSpec((tm, n_logit), lambda i: (i, 0))]
        out_shape = out_shape + [jax.ShapeDtypeStruct((m, n_logit), F32)]
    return pl.pallas_call(
        functools.partial(_ln_body, alpha=alpha, routed=router_w is not None),
        grid=(m // tm,),
        in_specs=[row, row, vec, vec] + r_specs,
        out_specs=out_specs,
        out_shape=out_shape,
        compiler_params=_cparams(("parallel",)),
        name="residual_layer_norm",
    )(x, h, g.reshape(1, d), b.reshape(1, d), *r_args)


def _combine_ln_body(dest_ref, x_ref, ys_ref, gate_ref, g_ref, b_ref, oa_ref, ob_ref, buf_ref, sem_ref,
                     *, alpha, head_tiles, tm, top_k):
    i = pl.program_id(0)
    n_tiles = pl.num_programs(0)

    def issue(tile, slot):
        base = tile * (tm * top_k)

        def body(r, carry):
            for k in range(top_k):
                row = dest_ref[base + r * top_k + k]
                pltpu.make_async_copy(ys_ref.at[pl.ds(row, 1)], buf_ref.at[slot, k, pl.ds(r, 1)],
                                      sem_ref.at[slot]).start()
            return carry

        lax.fori_loop(0, tm, body, 0, unroll=8)

    @pl.when(i == 0)
    def _():
        issue(0, 0)

    @pl.when(i + 1 < n_tiles)
    def _():
        issue(i + 1, (i + 1) % 2)

    slot = i % 2
    for k in range(top_k):
        pltpu.make_async_copy(ys_ref.at[pl.ds(0, tm)], buf_ref.at[slot, k], sem_ref.at[slot]).wait()
    h = buf_ref[slot, 0] * gate_ref[:, 0:1]
    for k in range(1, top_k):
        h = h + buf_ref[slot, k] * gate_ref[:, k:k + 1]
    y = _layer_norm_rows(alpha * x_ref[...] + h, g_ref, b_ref)
    _store_rows(y, oa_ref, ob_ref, head_tiles)


def moe_combine_layer_norm(x, ys, dest, gates, g, b, alpha, split_rows=None):
    m, d = x.shape
    top_k = dest.shape[1]
    tm = LN_ROWS
    assert m % tm == 0 and ys.dtype == F32
    head_tiles, out_specs, out_shape = _ln_out_specs(m, d, tm, split_rows)
    row = pl.BlockSpec((tm, d), lambda i, *_: (i, 0))
    vec = pl.BlockSpec((1, d), lambda i, *_: (0, 0))
    return pl.pallas_call(
        functools.partial(_combine_ln_body, alpha=alpha, head_tiles=head_tiles, tm=tm, top_k=top_k),
        grid_spec=pltpu.PrefetchScalarGridSpec(
            num_scalar_prefetch=1,
            grid=(m // tm,),
            in_specs=[row, pl.BlockSpec(memory_space=pl.ANY),
                      pl.BlockSpec((tm, top_k), lambda i, *_: (i, 0)), vec, vec],
            out_specs=out_specs,
            scratch_shapes=[pltpu.VMEM((2, top_k, tm, d), F32), pltpu.SemaphoreType.DMA((2,))]),
        out_shape=out_shape,
        compiler_params=_cparams(("arbitrary",)),
        name="moe_combine_layer_norm",
    )(dest.reshape(-1).astype(jnp.int32), x, ys, gates, g.reshape(1, d), b.reshape(1, d))


SUBLANES = 8


def _shift_mix_body(x_ref, halo_ref, shift_ref, mu_ref, o_ref, *, tm, tp, ts, prompt_tiles):
    i = pl.program_id(0)
    x = x_ref[...]
    row = lax.broadcasted_iota(jnp.int32, (tm, 1), 0)
    prev = jnp.where(row == 0, halo_ref[SUBLANES - 1:SUBLANES, :], pltpu.roll(x, 1, 0))
    prompt_start = jnp.logical_and(row == 0, (i * tm) % tp == 0)
    prev_prompt = jnp.where(prompt_start, 0.0, prev)
    nseq = tm // ts
    carried = jnp.broadcast_to(shift_ref[...][:, None, :], (nseq, ts, x.shape[1])).reshape(x.shape)
    prev_sample = jnp.where(row % ts == 0, carried, prev)
    xx = jnp.where(i < prompt_tiles, prev_prompt, prev_sample) - x
    for j in range(o_ref.shape[0]):
        o_ref[j] = (x + xx * mu_ref[j:j + 1, :]).astype(BF16)


def shift_mix(x, shift0, mu, *, n_prompt_rows, tp, ts):
    m, d = x.shape
    nmix = mu.shape[0]
    tm = 512
    assert tp % tm == 0 and tm % ts == 0 and n_prompt_rows % tm == 0 and m % tm == 0
    prompt_tiles = n_prompt_rows // tm
    nseq = tm // ts
    assert shift0.shape[0] % nseq == 0
    halo_blocks = tm // SUBLANES
    return pl.pallas_call(
        functools.partial(_shift_mix_body, tm=tm, tp=tp, ts=ts, prompt_tiles=prompt_tiles),
        grid=(m // tm,),
        in_specs=[pl.BlockSpec((tm, d), lambda i: (i, 0)),
                  pl.BlockSpec((SUBLANES, d), lambda i: (jnp.maximum(i * halo_blocks - 1, 0), 0)),
                  pl.BlockSpec((nseq, d), lambda i: (jnp.maximum(i - prompt_tiles, 0), 0)),
                  pl.BlockSpec((nmix, d), lambda i: (0, 0))],
        out_specs=pl.BlockSpec((nmix, tm, d), lambda i: (0, i, 0)),
        out_shape=jax.ShapeDtypeStruct((nmix, m, d), BF16),
        compiler_params=_cparams(("parallel",)),
        name="rwkv_shift_mix",
    )(x, x, shift0, mu)


def _chunk_state_init(c, ncp, cps, state_ref, s0_ref):
    @pl.when(jnp.logical_and(c < ncp, c % cps == 0))
    def _():
        state_ref[...] = jnp.zeros(state_ref.shape, state_ref.dtype)

    @pl.when(c >= ncp)
    def _():
        state_ref[...] = s0_ref[0]


def _dla_body(*refs, heads, dk, dv, mode, ncp, cps):
    o_ref, sp_ref, ss_ref, st_ref = refs[-4:]
    refs = refs[:-4]
    if mode == "gla":
        q_ref, k_ref, v_ref, g_ref, gate_ref, ng_ref, s0_ref = refs
    else:
        q_ref, k_ref, v_ref, cos_ref, sin_ref, gate_ref, ng_ref, s0_ref = refs
    c = pl.program_id(0)
    _chunk_state_init(c, ncp, cps, st_ref, s0_ref)

    t_row = lax.broadcasted_iota(jnp.int32, (CHUNK, CHUNK), 0)
    t_col = lax.broadcasted_iota(jnp.int32, (CHUNK, CHUNK), 1)
    causal = t_row >= t_col
    if mode == "gla":
        cum_all = _chunk_prefix_sum(g_ref[...])
    else:
        width = heads * dk
        even = (lax.broadcasted_iota(jnp.int32, (CHUNK, width), 1) % 2) == 0
        q_all = q_ref[...]
        k_all = k_ref[...]
        q_sw = jnp.where(even, pltpu.roll(q_all, width - 1, 1), pltpu.roll(q_all, 1, 1))
        k_sw = jnp.where(even, pltpu.roll(k_all, width - 1, 1), pltpu.roll(k_all, 1, 1))
        cos = cos_ref[...]
        sin = sin_ref[...]
        frame = (lax.broadcasted_iota(jnp.int32, (CHUNK, 1), 0) + 1).astype(F32)

    for h in range(heads):
        ks = slice(h * dk, (h + 1) * dk)
        vs = slice(h * dv, (h + 1) * dv)
        v = v_ref[:, vs]
        s_prev = st_ref[h]
        if mode == "gla":
            q = q_ref[:, ks]
            k = k_ref[:, ks]
            cum = cum_all[:, ks]
            total = cum[CHUNK - 1:CHUNK, :]
            total_col = jnp.transpose(jnp.broadcast_to(total, (LANES, dk)))[:, :1]
            state_decay = jnp.exp(total_col)
            q_dec = q * jnp.exp(cum)
            k_inv = k * jnp.exp(-cum)
            k_tail = k * jnp.exp(total - cum)
        else:
            q = q_all[:, ks] * cos + q_sw[:, ks] * sin
            k = (k_all[:, ks] * cos + k_sw[:, ks] * sin) * (dk ** -0.5)
            log_gamma = math.log1p(-(2.0 ** (-5.0 - h)))
            cum = frame * log_gamma
            total = CHUNK * log_gamma
            state_decay = math.exp(total)
            q_dec = q * jnp.exp(cum)
            k_inv = k * jnp.exp(-cum)
            k_tail = k * jnp.exp(total - cum)
        scores = jnp.where(causal, _bdot_nt(q_dec, k_inv), 0.0)
        o = _bdot(q_dec, s_prev) + _bdot(scores, v)
        k_tail_t = jnp.transpose(k_tail)
        st_ref[h] = state_decay * s_prev + _bdot(k_tail_t, v)
        if mode == "gla":
            o = o * lax.rsqrt(jnp.mean(o * o, -1, keepdims=True) + GLA_NORM_EPS) * ng_ref[...]
        else:
            mu = jnp.mean(o, -1, keepdims=True)
            oc = o - mu
            var = jnp.mean(oc * oc, -1, keepdims=True)
            o = oc * lax.rsqrt(var + RET_GN_EPS) * ng_ref[:, vs]
        if mode == "gla":
            o_ref[:, vs] = (o * gate_ref[:, vs]).astype(BF16)
        else:
            o_ref[:, vs] = (gate_ref[:, vs] * o).astype(BF16)

    @pl.when(jnp.logical_and(c < ncp, c % cps == cps - 1))
    def _():
        sp_ref[0] = st_ref[...]

    @pl.when(c >= ncp)
    def _():
        ss_ref[0] = st_ref[...]


def _seq_state_specs(state_shape, ncp, cps, n_prompt, slot):
    blk = (None, 1) + tuple(state_shape)
    zeros = (0,) * len(state_shape)
    s0_spec = pl.BlockSpec(blk, lambda c: (slot, jnp.maximum(c - ncp, 0)) + zeros)
    sp_spec = pl.BlockSpec(blk, lambda c: (0, jnp.minimum(c // cps, n_prompt - 1)) + zeros)
    ss_spec = pl.BlockSpec(blk, lambda c: (0, jnp.maximum(c - ncp, 0)) + zeros)
    return s0_spec, sp_spec, ss_spec


def _state_out_shapes(n_prompt, n_sample, state_shape):
    return [jax.ShapeDtypeStruct((1, n_prompt) + tuple(state_shape), F32),
            jax.ShapeDtypeStruct((1, n_sample) + tuple(state_shape), F32)]


def decay_attention(q, k, v, extra, gate, norm_g, s0, slot, *, mode, heads, n_prompt, cps):
    nt = q.shape[0]
    dk = q.shape[1] // heads
    dv = v.shape[1] // heads
    n_sample = s0.shape[1]
    ncp = n_prompt * cps
    nchunks = nt // CHUNK
    assert nchunks == ncp + n_sample
    rowk = pl.BlockSpec((CHUNK, heads * dk), lambda c: (c, 0))
    rowv = pl.BlockSpec((CHUNK, heads * dv), lambda c: (c, 0))
    s0_spec, sp_spec, ss_spec = _seq_state_specs((heads, dk, dv), ncp, cps, n_prompt, slot)
    if mode == "gla":
        extra_specs = [rowk]
        ng_spec = pl.BlockSpec((1, dv), lambda c: (0, 0))
        norm_g = norm_g.reshape(1, dv)
    else:
        pos_spec = pl.BlockSpec((CHUNK, dk), lambda c: (jnp.where(c < ncp, c % cps, cps), 0))
        extra_specs = [pos_spec, pos_spec]
        ng_spec = pl.BlockSpec((1, heads * dv), lambda c: (0, 0))
        norm_g = norm_g.reshape(1, heads * dv)
    return pl.pallas_call(
        functools.partial(_dla_body, heads=heads, dk=dk, dv=dv, mode=mode, ncp=ncp, cps=cps),
        grid=(nchunks,),
        in_specs=[rowk, rowk, rowv] + extra_specs + [rowv, ng_spec, s0_spec],
        out_specs=[rowv, sp_spec, ss_spec],
        out_shape=([jax.ShapeDtypeStruct((nt, heads * dv), BF16)]
                   + _state_out_shapes(n_prompt, n_sample, (heads, dk, dv))),
        scratch_shapes=[pltpu.VMEM((heads, dk, dv), F32)],
        compiler_params=_cparams(("arbitrary",)),
        name="decay_attention_" + mode,
    )(q, k, v, *extra, gate, norm_g, s0)


def _rwkv_body(r_ref, k_ref, v_ref, lw_ref, a_ref, g_ref, kk_ref, ka_ref, rk_ref, gng_ref, gnb_ref,
               s0_ref, o_ref, sp_ref, ss_ref, st_ref, *, pairs, ncp, cps):
    c = pl.program_id(0)
    n = RWKV_HEAD
    w2 = 2 * n

    @pl.when(jnp.logical_and(c < ncp, c % cps == 0))
    def _():
        st_ref[...] = jnp.zeros(st_ref.shape, F32)

    @pl.when(c >= ncp)
    def _():
        zero = jnp.zeros((n, n), F32)
        for p in range(pairs):
            top = jnp.concatenate([s0_ref[0, 2 * p], zero], axis=1)
            bot = jnp.concatenate([zero, s0_ref[0, 2 * p + 1]], axis=1)
            st_ref[p] = jnp.concatenate([top, bot], axis=0)

    def paired(ref):
        x = ref[...]
        return jnp.stack([x[:, p * w2:(p + 1) * w2] for p in range(pairs)])

    lane = lax.broadcasted_iota(jnp.int32, (1, 1, w2), 2)
    first = lane < n

    def head_sum(x):
        s_a = jnp.sum(jnp.where(first, x, 0.0), -1, keepdims=True)
        s_b = jnp.sum(jnp.where(first, 0.0, x), -1, keepdims=True)
        return jnp.where(first, s_a, s_b)

    def stacked(x):
        return jnp.concatenate([jnp.where(first, x, 0.0), jnp.where(first, 0.0, x)], axis=1)

    def bmm(a, b):
        return lax.dot_general(a.astype(BF16), b.astype(BF16), (((2,), (1,)), ((0,), (0,))),
                               preferred_element_type=F32)

    def bmm_nt(a, b):
        return lax.dot_general(a.astype(BF16), b.astype(BF16), (((2,), (2,)), ((0,), (0,))),
                               preferred_element_type=F32)

    r = paired(r_ref)
    k_raw = paired(k_ref)
    v = paired(v_ref)
    lw = paired(lw_ref)
    a = paired(a_ref)
    k_k = paired(kk_ref)
    k_a = paired(ka_ref)
    r_k = paired(rk_ref)

    kk = k_raw * k_k
    kk = kk / jnp.maximum(jnp.sqrt(head_sum(kk * kk)), 1e-12)
    k_h = k_raw * (1.0 + (a - 1.0) * k_a)
    a_vec = -kk
    b_vec = kk * a

    cum = paired_value(_chunk_prefix_sum(lw_ref[...]), pairs, w2)
    total = cum[:, CHUNK - 1:CHUNK, :]
    p_now = jnp.exp(cum)
    p_prev = jnp.exp(cum - lw)
    p_inv = jnp.exp(-cum)
    p_tail = jnp.exp(total - cum)

    lhs = jnp.concatenate([stacked(a_vec * p_prev), stacked(r * p_now)], axis=1)
    rhs = jnp.concatenate([stacked(b_vec * p_inv), stacked(k_h * p_inv)], axis=1)
    sc = bmm_nt(lhs, rhs)
    s_prev = st_ref[...]
    sr = bmm_nt(lhs, s_prev)

    i_row = lax.broadcasted_iota(jnp.int32, (1, w2, w2), 1)
    i_col = lax.broadcasted_iota(jnp.int32, (1, w2, w2), 2)
    same = (i_row // n) == (i_col // n)
    strict = jnp.logical_and(same, (i_col % n) < (i_row % n))
    incl = jnp.logical_and(same, (i_col % n) <= (i_row % n))
    a_ab = jnp.where(strict, sc[:, :w2, :w2], 0.0)
    a_ak = jnp.where(strict, sc[:, :w2, w2:], 0.0)
    a_rb = jnp.where(incl, sc[:, w2:, :w2], 0.0)
    a_rk = jnp.where(incl, sc[:, w2:, w2:], 0.0)
    u0 = sr[:, :w2]
    y0 = sr[:, w2:]
    v_st = stacked(v)

    eye = (i_row == i_col).astype(F32)
    t_inv = eye + a_ab
    power = a_ab
    span = 1
    while 2 * span < CHUNK:
        power = bmm(power, power)
        t_inv = t_inv + bmm(t_inv, power)
        span *= 2

    u_st = bmm(t_inv, u0 + bmm(a_ak, v_st))
    uv = jnp.concatenate([u_st, v_st], axis=1)
    y_st = y0 + bmm(jnp.concatenate([a_rb, a_rk], axis=2), uv)
    y = y_st[:, :n] + y_st[:, n:]

    tails = jnp.concatenate([stacked(b_vec * p_tail), stacked(k_h * p_tail)], axis=1)
    uv_t = jnp.swapaxes(uv, 1, 2)
    s_new = s_prev * jnp.exp(total) + bmm(uv_t, tails)
    st_ref[...] = s_new

    mu = head_sum(y) * (1.0 / n)
    yc = y - mu
    var = head_sum(yc * yc) * (1.0 / n)
    yn = yc * lax.rsqrt(var + RWKV_GN_EPS) * paired(gng_ref) + paired(gnb_ref)
    out = yn + head_sum(r * k_h * r_k) * v
    gate = paired(g_ref)
    for p in range(pairs):
        o_ref[:, p * w2:(p + 1) * w2] = (out[p] * gate[p]).astype(BF16)

    def store_state(dst_ref):
        for p in range(pairs):
            dst_ref[0, 2 * p] = s_new[p, :n, :n]
            dst_ref[0, 2 * p + 1] = s_new[p, n:, n:]

    @pl.when(jnp.logical_and(c < ncp, c % cps == cps - 1))
    def _():
        store_state(sp_ref)

    @pl.when(c >= ncp)
    def _():
        store_state(ss_ref)


def paired_value(x, pairs, w2):
    return jnp.stack([x[:, p * w2:(p + 1) * w2] for p in range(pairs)])


def rwkv7_attention(r, k, v, lw, a, g, k_k, k_a, r_k, gn_g, gn_b, s0, slot, *, n_prompt, cps):
    nt, d = r.shape
    heads = d // RWKV_HEAD
    pairs = heads // 2
    n_sample = s0.shape[1]
    ncp = n_prompt * cps
    assert nt // CHUNK == ncp + n_sample
    row = pl.BlockSpec((CHUNK, d), lambda c: (c, 0))
    vec = pl.BlockSpec((1, d), lambda c: (0, 0))
    s0_spec, sp_spec, ss_spec = _seq_state_specs((heads, RWKV_HEAD, RWKV_HEAD), ncp, cps, n_prompt,
                                                 slot)
    vecs = [u.reshape(1, d) for u in (k_k, k_a, r_k, gn_g, gn_b)]
    return pl.pallas_call(
        functools.partial(_rwkv_body, pairs=pairs, ncp=ncp, cps=cps),
        grid=(nt // CHUNK,),
        in_specs=[row] * 6 + [vec] * 5 + [s0_spec],
        out_specs=[row, sp_spec, ss_spec],
        out_shape=([jax.ShapeDtypeStruct((nt, d), BF16)]
                   + _state_out_shapes(n_prompt, n_sample, (heads, RWKV_HEAD, RWKV_HEAD))),
        scratch_shapes=[pltpu.VMEM((pairs, 2 * RWKV_HEAD, 2 * RWKV_HEAD), F32)],
        compiler_params=_cparams(("arbitrary",)),
        name="rwkv7_attention",
    )(r, k, v, lw, a, g, *vecs, s0)


def _expert_weight_pipeline(be_ref, chg_ref, nxt_ref, hbm_refs, stage_refs, cache_refs, sem_ref,
                            tile):
    j = pl.program_id(0)
    b = pl.program_id(1)
    n_pass = pl.num_programs(0)

    def copies(expert, col_pass):
        col = pl.multiple_of(col_pass * tile, tile)
        return [pltpu.make_async_copy(hbm.at[expert, :, pl.ds(col, tile)], stage, sem_ref.at[i])
                for i, (hbm, stage) in enumerate(zip(hbm_refs, stage_refs))]

    @pl.when(jnp.logical_and(j == 0, b == 0))
    def _():
        for cp in copies(be_ref[0], 0):
            cp.start()

    @pl.when(chg_ref[b] == 1)
    def _():
        for cp in copies(be_ref[b], j):
            cp.wait()
        for stage, cache in zip(stage_refs, cache_refs):
            cache[...] = stage[...].astype(BF16)
        nxt = nxt_ref[b]

        @pl.when(nxt >= 0)
        def _():
            for cp in copies(be_ref[jnp.maximum(nxt, 0)], j):
                cp.start()

        @pl.when(jnp.logical_and(nxt < 0, j + 1 < n_pass))
        def _():
            for cp in copies(be_ref[0], j + 1):
                cp.start()


def _gate_up_body(be_ref, chg_ref, nxt_ref, nu_ref, x_ref, w1_ref, w3_ref, h_ref,
                  w1s_ref, w3s_ref, w1c_ref, w3c_ref, sem_ref, *, tile):
    b = pl.program_id(1)
    _expert_weight_pipeline(be_ref, chg_ref, nxt_ref, (w1_ref, w3_ref), (w1s_ref, w3s_ref),
                            (w1c_ref, w3c_ref), sem_ref, tile)

    @pl.when(b < nu_ref[0])
    def _():
        x = x_ref[...]
        gate = jnp.dot(x, w1c_ref[...], preferred_element_type=F32)
        up = jnp.dot(x, w3c_ref[...], preferred_element_type=F32)
        h_ref[...] = (gate * jax.nn.sigmoid(gate) * up).astype(BF16)

    @pl.when(b >= nu_ref[0])
    def _():
        h_ref[...] = jnp.zeros(h_ref.shape, BF16)


def _down_body(be_ref, chg_ref, nxt_ref, nu_ref, h_ref, w2_ref, y_ref, w2s_ref, w2c_ref, sem_ref, *,
               tile):
    b = pl.program_id(1)
    _expert_weight_pipeline(be_ref, chg_ref, nxt_ref, (w2_ref,), (w2s_ref,), (w2c_ref,), sem_ref,
                            tile)

    @pl.when(b < nu_ref[0])
    def _():
        y_ref[...] = jnp.dot(h_ref[...], w2c_ref[...],
                             preferred_element_type=F32).astype(y_ref.dtype)

    @pl.when(b >= nu_ref[0])
    def _():
        y_ref[...] = jnp.zeros(y_ref.shape, y_ref.dtype)


def _block_meta(block_expert, n_used):
    be = block_expert.astype(jnp.int32)
    nb = be.shape[0]
    changed = jnp.concatenate([jnp.ones((1,), jnp.int32),
                               (be[1:] != be[:-1]).astype(jnp.int32)])
    idx = jnp.arange(nb, dtype=jnp.int32)
    opener = jnp.where(changed == 1, idx, nb)
    after = lax.cummin(jnp.concatenate([opener[1:], jnp.full((1,), nb, jnp.int32)]), reverse=True)
    nxt = jnp.where(after >= nb, -1, after).astype(jnp.int32)
    return be, changed, nxt, jnp.reshape(n_used, (1,)).astype(jnp.int32)


def swiglu_gate_up(xs, block_expert, n_used, w1, w3, block, tile_f=(512, 256, 128)):
    rows, d = xs.shape
    f = w1.shape[-1]
    tf = _pick(f, tile_f)
    hbm = pl.BlockSpec(memory_space=pl.ANY)
    return pl.pallas_call(
        functools.partial(_gate_up_body, tile=tf),
        grid_spec=pltpu.PrefetchScalarGridSpec(
            num_scalar_prefetch=4,
            grid=(f // tf, rows // block),
            in_specs=[pl.BlockSpec((block, d), lambda j, b, *_: (b, 0)), hbm, hbm],
            out_specs=pl.BlockSpec((block, tf), lambda j, b, *_: (b, j)),
            scratch_shapes=[pltpu.VMEM((d, tf), F32), pltpu.VMEM((d, tf), F32),
                            pltpu.VMEM((d, tf), BF16), pltpu.VMEM((d, tf), BF16),
                            pltpu.SemaphoreType.DMA((2,))]),
        out_shape=jax.ShapeDtypeStruct((rows, f), BF16),
        compiler_params=_cparams(("arbitrary", "arbitrary")),
        name="swiglu_gate_up",
    )(*_block_meta(block_expert, n_used), xs, w1, w3)


def swiglu_down(h, block_expert, n_used, w2, block, out_dtype=F32):
    rows, f = h.shape
    d = w2.shape[-1]
    tn = _pick(d, (512, 256, 128))
    return pl.pallas_call(
        functools.partial(_down_body, tile=tn),
        grid_spec=pltpu.PrefetchScalarGridSpec(
            num_scalar_prefetch=4,
            grid=(d // tn, rows // block),
            in_specs=[pl.BlockSpec((block, f), lambda j, b, *_: (b, 0)),
                      pl.BlockSpec(memory_space=pl.ANY)],
            out_specs=pl.BlockSpec((block, tn), lambda j, b, *_: (b, j)),
            scratch_shapes=[pltpu.VMEM((f, tn), F32), pltpu.VMEM((f, tn), BF16),
                            pltpu.SemaphoreType.DMA((1,))]),
        out_shape=jax.ShapeDtypeStruct((rows, d), out_dtype),
        compiler_params=_cparams(("arbitrary", "arbitrary")),
        name="swiglu_down",
    )(*_block_meta(block_expert, n_used), h, w2)


def _dispatch_body(src_ref, nu_ref, x_ref, o_ref, buf_ref, sem_ref, *, block):
    b = pl.program_id(0)
    n_used = nu_ref[0]

    def issue(blk, slot):
        base = blk * block

        def body(r, carry):
            pltpu.make_async_copy(x_ref.at[pl.ds(src_ref[base + r], 1)],
                                  buf_ref.at[slot, pl.ds(r, 1)], sem_ref.at[slot]).start()
            return carry

        lax.fori_loop(0, block, body, 0, unroll=8)

    @pl.when(jnp.logical_and(b == 0, n_used > 0))
    def _():
        issue(0, 0)

    @pl.when(b + 1 < n_used)
    def _():
        issue(b + 1, (b + 1) % 2)

    @pl.when(b < n_used)
    def _():
        slot = b % 2
        pltpu.make_async_copy(x_ref.at[pl.ds(0, block)], buf_ref.at[slot], sem_ref.at[slot]).wait()
        o_ref[...] = buf_ref[slot].astype(BF16)

    @pl.when(b >= n_used)
    def _():
        o_ref[...] = jnp.zeros(o_ref.shape, BF16)


def moe_dispatch(x, src_tok, n_used, block):
    n, d = x.shape
    rows = src_tok.shape[0]
    return pl.pallas_call(
        functools.partial(_dispatch_body, block=block),
        grid_spec=pltpu.PrefetchScalarGridSpec(
            num_scalar_prefetch=2,
            grid=(rows // block,),
            in_specs=[pl.BlockSpec(memory_space=pl.ANY)],
            out_specs=pl.BlockSpec((block, d), lambda b, *_: (b, 0)),
            scratch_shapes=[pltpu.VMEM((2, block, d), F32), pltpu.SemaphoreType.DMA((2,))]),
        out_shape=jax.ShapeDtypeStruct((rows, d), BF16),
        compiler_params=_cparams(("arbitrary",)),
        name="moe_dispatch",
    )(src_tok.astype(jnp.int32), jnp.reshape(n_used, (1,)).astype(jnp.int32), x)


def _pad_cols(w, mult=LANES):
    pad = (-w.shape[-1]) % mult
    return jnp.pad(w, [(0, 0)] * (w.ndim - 1) + [(0, pad)]) if pad else w


def _pad_rows(w, mult=LANES):
    pad = (-w.shape[-2]) % mult
    return jnp.pad(w, [(0, 0)] * (w.ndim - 2) + [(0, pad), (0, 0)]) if pad else w


DENSE_GATE_UP_BLOCK = 1024
DENSE_DOWN_BLOCK = 512
MOE_GATE_UP_TILE_F = (1024, 512, 256, 128)


def dense_swiglu(xb, w1, w3, w2, slot):
    rows = xb.shape[0]
    assert rows % DENSE_GATE_UP_BLOCK == 0 and rows % DENSE_DOWN_BLOCK == 0
    nb_a = rows // DENSE_GATE_UP_BLOCK
    nb_b = rows // DENSE_DOWN_BLOCK
    h = swiglu_gate_up(xb, jnp.full((nb_a,), slot, jnp.int32), jnp.int32(nb_a), w1, w3,
                       DENSE_GATE_UP_BLOCK)
    return swiglu_down(h, jnp.full((nb_b,), slot, jnp.int32), jnp.int32(nb_b), w2, DENSE_DOWN_BLOCK,
                       out_dtype=SUBLAYER_OUT_DTYPE)


def moe_swiglu(logits, xf, w1, w3, w2, slot):
    n, d = xf.shape
    n_exp = logits.shape[-1]
    top_val, top_idx = lax.top_k(logits, MOE_TOP_K)
    gates = jax.nn.softmax(top_val, axis=-1)
    slots = n * MOE_TOP_K
    flat_e = top_idx.reshape(-1)
    onehot = (flat_e[:, None] == jnp.arange(n_exp, dtype=flat_e.dtype)[None, :]).astype(jnp.int32)
    rank = jnp.sum((jnp.cumsum(onehot, axis=0) - onehot) * onehot, axis=1)
    counts = jnp.sum(onehot, axis=0)
    start = jnp.cumsum(counts) - counts
    padded = (counts + MOE_BLOCK - 1) // MOE_BLOCK * MOE_BLOCK
    pend = jnp.cumsum(padded)
    pstart = pend - padded
    nb = (slots + n_exp * (MOE_BLOCK - 1) + MOE_BLOCK - 1) // MOE_BLOCK
    dest = (pstart[flat_e] + rank).astype(jnp.int32)
    block_start = jnp.arange(nb, dtype=jnp.int32) * MOE_BLOCK
    expert_of_block = jnp.clip(jnp.searchsorted(pend, block_start, side="right"), 0, n_exp - 1)
    order = jnp.argsort(flat_e, stable=True)
    e_row = jnp.repeat(expert_of_block, MOE_BLOCK)
    rank_row = jnp.arange(nb * MOE_BLOCK, dtype=jnp.int32) - pstart[e_row]
    valid = rank_row < counts[e_row]
    slot_row = order[jnp.clip(start[e_row] + rank_row, 0, slots - 1)]
    src_tok = jnp.where(valid, slot_row // MOE_TOP_K, 0).astype(jnp.int32)
    block_expert = expert_of_block.astype(jnp.int32) + slot * n_exp
    n_used = pend[-1] // MOE_BLOCK
    xs = moe_dispatch(xf, src_tok, n_used, MOE_BLOCK)
    e1 = w1.reshape((-1,) + w1.shape[2:])
    e3 = w3.reshape((-1,) + w3.shape[2:])
    e2 = w2.reshape((-1,) + w2.shape[2:])
    hs = swiglu_gate_up(xs, block_expert, n_used, e1, e3, MOE_BLOCK, tile_f=MOE_GATE_UP_TILE_F)
    ys = swiglu_down(hs, block_expert, n_used, e2, MOE_BLOCK)
    return ys, dest.reshape(n, MOE_TOP_K), gates


def _rope_tables(dk, cps, past_len):
    half = dk // 2
    inv_freq = 1.0 / (RET_ROPE_BASE ** jnp.linspace(0.0, 1.0, half, dtype=F32))
    pos = jnp.concatenate([jnp.arange(cps * CHUNK, dtype=jnp.int32),
                           past_len + jnp.arange(CHUNK, dtype=jnp.int32)])
    ang = pos.astype(F32)[:, None] * inv_freq[None, :]
    cos = jnp.repeat(jnp.cos(ang), 2, axis=1)
    sin = jnp.stack([-jnp.sin(ang), jnp.sin(ang)], axis=-1).reshape(pos.shape[0], dk)
    return cos, sin


def kernel(x_prompt, x_sample, state_gla, state_rwkv, state_shift, state_ret, ln_g, ln_b,
           gla_wq, gla_wk, gla_wv, gla_wr, gla_wa1, gla_wa2, gla_ba, gla_norm_g, gla_wo,
           rwkv_mu, rwkv_wr, rwkv_wk, rwkv_wv, rwkv_wo, rwkv_w0, rwkv_w1, rwkv_w2, rwkv_a0, rwkv_a1,
           rwkv_a2, rwkv_g1, rwkv_g2, rwkv_k_k, rwkv_k_a, rwkv_r_k, rwkv_gn_g, rwkv_gn_b,
           ret_wq, ret_wk, ret_wv, ret_wg, ret_gn_g, ret_wo,
           ffn_w1, ffn_w3, ffn_w2, moe_router, moe_w1, moe_w3, moe_w2):
    bp, tp, d = x_prompt.shape
    bs, ts, _ = x_sample.shape
    assert ts == CHUNK and tp % CHUNK == 0
    depth = ln_g.shape[0]
    alpha = (2.0 * depth) ** 0.25
    cps = tp // CHUNK
    past_len = tp
    n_prompt_rows = bp * tp
    seq = dict(n_prompt=bp, cps=cps)

    gla_heads = state_gla.shape[2]
    gla_dk = state_gla.shape[3]
    ret_heads = state_ret.shape[2]
    ret_dk = state_ret.shape[3]

    xf = jnp.concatenate([x_prompt.reshape(-1, d), x_sample.reshape(-1, d)], axis=0)
    xb = xf.astype(BF16)

    gla_states, rwkv_states, ret_states = [], [], []
    new_shift_p, new_shift_s = [], []

    def stacked(per_layer, which):
        parts = [states[which] for states in per_layer]
        return parts[0] if len(parts) == 1 else jnp.concatenate(parts, axis=0)

    last_rows = jnp.concatenate([jnp.arange(bp, dtype=jnp.int32) * tp + (tp - 1),
                                 n_prompt_rows + jnp.arange(bs, dtype=jnp.int32) * ts + (ts - 1)])
    for i in range(depth):
        kind, slot = i % 3, i // 3
        if kind == 0:
            q = matmul(xb, gla_wq, slot, scale=gla_dk ** -0.5)
            k = matmul(xb, gla_wk, slot)
            v = matmul(xb, gla_wv, slot)
            gate = matmul(xb, gla_wr, slot, act="silu")
            low = matmul(xb, _pad_cols(gla_wa1), slot, out_dtype=BF16)
            log_alpha = matmul(low, _pad_rows(gla_wa2), slot, act="gla_gate", bias=gla_ba[slot])
            o, *states = decay_attention(q, k, v, (log_alpha,), gate, gla_norm_g[slot], state_gla,
                                         slot, mode="gla", heads=gla_heads, **seq)
            gla_states.append(states)
            h = matmul(o, gla_wo, slot, out_dtype=SUBLAYER_OUT_DTYPE)
        elif kind == 1:
            mixes = shift_mix(xf, state_shift[slot], rwkv_mu[slot], n_prompt_rows=n_prompt_rows,
                              tp=tp, ts=ts)
            i_r, i_w, i_k, i_v, i_a, i_g = range(6)
            r = matmul(mixes, rwkv_wr, slot, x_slot=i_r)
            k = matmul(mixes, rwkv_wk, slot, x_slot=i_k)
            v = matmul(mixes, rwkv_wv, slot, x_slot=i_v)
            w_mid = matmul(mixes, _pad_cols(rwkv_w1), slot, x_slot=i_w, act="tanh", out_dtype=BF16)
            log_decay = matmul(w_mid, _pad_rows(rwkv_w2), slot, act="rwkv_decay", bias=rwkv_w0[slot])
            a_mid = matmul(mixes, _pad_cols(rwkv_a1), slot, x_slot=i_a, out_dtype=BF16)
            a = matmul(a_mid, _pad_rows(rwkv_a2), slot, act="sigmoid", bias=rwkv_a0[slot])
            g_mid = matmul(mixes, _pad_cols(rwkv_g1), slot, x_slot=i_g, act="sigmoid", out_dtype=BF16)
            g = matmul(g_mid, _pad_rows(rwkv_g2), slot)
            o, *states = rwkv7_attention(r, k, v, log_decay, a, g, rwkv_k_k[slot], rwkv_k_a[slot],
                                         rwkv_r_k[slot].reshape(-1), rwkv_gn_g[slot],
                                         rwkv_gn_b[slot], state_rwkv, slot, **seq)
            rwkv_states.append(states)
            h = matmul(o, rwkv_wo, slot, out_dtype=SUBLAYER_OUT_DTYPE)
            ends = jnp.take(xf, last_rows, axis=0, mode="clip")
            new_shift_p.append(ends[:bp])
            new_shift_s.append(ends[bp:])
        else:
            q = matmul(xb, ret_wq, slot)
            k = matmul(xb, ret_wk, slot)
            v = matmul(xb, ret_wv, slot)
            gate = matmul(xb, ret_wg, slot, act="silu")
            cos, sin = _rope_tables(ret_dk, cps, past_len)
            o, *states = decay_attention(q, k, v, (cos, sin), gate, ret_gn_g[slot], state_ret, slot,
                                         mode="ret", heads=ret_heads, **seq)
            ret_states.append(states)
            h = matmul(o, ret_wo, slot, out_dtype=SUBLAYER_OUT_DTYPE)
        fslot = i // 2
        is_moe = i % 2 == 1
        router_w = _pad_cols(moe_router[fslot]) if is_moe else None
        xf, xb, *logits = residual_layer_norm(xf, h, ln_g[i, 0], ln_b[i, 0], alpha, router_w=router_w)
        last = i == depth - 1
        if is_moe:
            ys, dest, gates = moe_swiglu(logits[0][:, :moe_router.shape[-1]], xf, moe_w1, moe_w3,
                                         moe_w2, fslot)
            out_a, out_b = moe_combine_layer_norm(xf, ys, dest, gates, ln_g[i, 1], ln_b[i, 1], alpha,
                                                  split_rows=n_prompt_rows if last else None)
        else:
            h = dense_swiglu(xb, ffn_w1, ffn_w3, ffn_w2, fslot)
            out_a, out_b = residual_layer_norm(xf, h, ln_g[i, 1], ln_b[i, 1], alpha)
            if last:
                out_a, out_b = out_a[:n_prompt_rows], out_a[n_prompt_rows:]
        if last:
            y_prompt, y_sample = out_a.reshape(bp, tp, d), out_b.reshape(bs, ts, d)
        else:
            xf, xb = out_a, out_b

    return (y_prompt, y_sample,
            stacked(gla_states, 0), stacked(rwkv_states, 0), jnp.stack(new_shift_p),
            stacked(ret_states, 0),
            stacked(gla_states, 1), stacked(rwkv_states, 1), jnp.stack(new_shift_s),
            stacked(ret_states, 1))
```

```python
import functools
import math

import jax
import jax.numpy as jnp
from jax import lax
from jax.experimental import pallas as pl
from jax.experimental.pallas import tpu as pltpu

F32 = jnp.float32
BF16 = jnp.bfloat16

CHUNK = 64
LANES = 128
VMEM_LIMIT_BYTES = 56 * 1024 * 1024

LN_EPS = 1e-5
GLA_TAU = 16.0
GLA_NORM_EPS = 1e-5
RWKV_HEAD = 64
RWKV_GN_EPS = 64e-5
RET_ROPE_BASE = 10000.0
RET_GN_EPS = 1e-5
MOE_TOP_K = 2
MOE_BLOCK = 512

_HI = lax.Precision.HIGHEST


def _cparams(sem):
    return pltpu.CompilerParams(dimension_semantics=sem, vmem_limit_bytes=VMEM_LIMIT_BYTES)


def _bdot(a, b):
    return jnp.dot(a.astype(BF16), b.astype(BF16), preferred_element_type=F32)


def _bdot_nt(a, b):
    return lax.dot_general(a.astype(BF16), b.astype(BF16), (((1,), (1,)), ((), ())),
                           preferred_element_type=F32)


def _chunk_prefix_sum(x):
    t_row = lax.broadcasted_iota(jnp.int32, (CHUNK, 3 * CHUNK), 0)
    t_col = lax.broadcasted_iota(jnp.int32, (CHUNK, 3 * CHUNK), 1) % CHUNK
    lower_ones = (t_row >= t_col).astype(BF16)
    hi = x.astype(BF16)
    rest = x - hi.astype(F32)
    mid = rest.astype(BF16)
    lo = (rest - mid.astype(F32)).astype(BF16)
    return jnp.dot(lower_ones, jnp.concatenate([hi, mid, lo], axis=0), preferred_element_type=F32)


def _log_sigmoid(z):
    return -(jnp.maximum(-z, 0.0) + jnp.log1p(jnp.exp(-jnp.abs(z))))


def _act(name, z):
    if name == "none":
        return z
    if name == "silu":
        return z * jax.nn.sigmoid(z)
    if name == "sigmoid":
        return jax.nn.sigmoid(z)
    if name == "tanh":
        return jnp.tanh(z)
    if name == "gla_gate":
        return _log_sigmoid(z) / GLA_TAU
    if name == "rwkv_decay":
        return -jnp.exp(_log_sigmoid(z) - 0.5)
    raise ValueError(name)


def _mm_body(*refs, act, has_bias, scale):
    if has_bias:
        x_ref, w_ref, b_ref, o_ref, wc_ref = refs
    else:
        x_ref, w_ref, o_ref, wc_ref = refs
        b_ref = None

    @pl.when(pl.program_id(1) == 0)
    def _():
        wc_ref[...] = w_ref[...].astype(BF16)

    acc = jnp.dot(x_ref[...], wc_ref[...], preferred_element_type=F32)
    if scale != 1.0:
        acc = acc * scale
    if has_bias:
        acc = acc + b_ref[...]
    o_ref[...] = _act(act, acc).astype(o_ref.dtype)


def _pick(n, pref):
    for t in pref:
        if n % t == 0:
            return t
    return n


MM_WEIGHT_TILE_ELEMS = 2 * 1024 * 1024


def matmul(x, w, slot=0, *, x_slot=None, act="none", bias=None, scale=1.0, out_dtype=F32):
    m, kdim = x.shape[-2:]
    _, kw, n = w.shape
    assert kw == kdim, (w.shape, x.shape)
    tm = _pick(m, (1024, 512, 256, 128, 64, 32, 16, 8))
    tn = _pick(n, tuple(t for t in (1024, 512, 256, 128) if t * kdim <= MM_WEIGHT_TILE_ELEMS))
    if x_slot is None:
        x_spec = pl.BlockSpec((tm, kdim), lambda j, i: (i, 0))
    else:
        x_spec = pl.BlockSpec((None, tm, kdim), lambda j, i: (x_slot, i, 0))
    in_specs = [x_spec, pl.BlockSpec((None, kdim, tn), lambda j, i: (slot, 0, j))]
    args = [x, w]
    if bias is not None:
        in_specs.append(pl.BlockSpec((1, tn), lambda j, i: (0, j)))
        args.append(bias.reshape(1, n).astype(F32))
    return pl.pallas_call(
        functools.partial(_mm_body, act=act, has_bias=bias is not None, scale=scale),
        grid=(n // tn, m // tm),
        in_specs=in_specs,
        out_specs=pl.BlockSpec((tm, tn), lambda j, i: (i, j)),
        out_shape=jax.ShapeDtypeStruct((m, n), out_dtype),
        scratch_shapes=[pltpu.VMEM((kdim, tn), BF16)],
        compiler_params=_cparams(("arbitrary", "arbitrary")),
        name="matmul_" + act,
    )(*args)


def _layer_norm_rows(z, g_ref, b_ref):
    mu = jnp.mean(z, -1, keepdims=True)
    zc = z - mu
    var = jnp.mean(zc * zc, -1, keepdims=True)
    return zc * lax.rsqrt(var + LN_EPS) * g_ref[...] + b_ref[...]


def _store_rows(y, oa_ref, ob_ref, head_tiles):
    if head_tiles is None:
        oa_ref[...] = y
        ob_ref[...] = y.astype(BF16)
    else:
        i = pl.program_id(0)

        @pl.when(i < head_tiles)
        def _():
            oa_ref[...] = y

        @pl.when(i >= head_tiles)
        def _():
            ob_ref[...] = y


def _ln_body(*refs, alpha, routed):
    if routed:
        x_ref, h_ref, g_ref, b_ref, router_ref, oa_ref, ob_ref, logit_ref = refs
    else:
        x_ref, h_ref, g_ref, b_ref, oa_ref, ob_ref = refs
    y = _layer_norm_rows(alpha * x_ref[...] + h_ref[...], g_ref, b_ref)
    if routed:
        logit_ref[...] = jnp.dot(y, router_ref[...], precision=_HI, preferred_element_type=F32)
    _store_rows(y, oa_ref, ob_ref, None)


LN_ROWS = 512


def _ln_out_specs(m, d, tm, split_rows):
    row = pl.BlockSpec((tm, d), lambda i, *_: (i, 0))
    if split_rows is None:
        return None, [row, row], [jax.ShapeDtypeStruct((m, d), F32),
                                  jax.ShapeDtypeStruct((m, d), BF16)]
    assert split_rows % tm == 0 and 0 < split_rows < m
    head_tiles = split_rows // tm
    specs = [pl.BlockSpec((tm, d), lambda i, *_: (jnp.minimum(i, head_tiles - 1), 0)),
             pl.BlockSpec((tm, d), lambda i, *_: (jnp.maximum(i - head_tiles, 0), 0))]
    shapes = [jax.ShapeDtypeStruct((split_rows, d), F32),
              jax.ShapeDtypeStruct((m - split_rows, d), F32)]
    return head_tiles, specs, shapes


def residual_layer_norm(x, h, g, b, alpha, router_w=None):
    m, d = x.shape
    tm = _pick(m, (LN_ROWS, 256, 128, 64, 32, 16, 8))
    row = pl.BlockSpec((tm, d), lambda i: (i, 0))
    vec = pl.BlockSpec((1, d), lambda i: (0, 0))
    _, out_specs, out_shape = _ln_out_specs(m, d, tm, None)
    r_specs, r_args = [], []
    if router_w is not None:
        n_logit = router_w.shape[1]
        r_specs, r_args = [pl.BlockSpec((d, n_logit), lambda i: (0, 0))], [router_w]
        out_specs = out_specs + [pl.BlockSpec((tm, n_logit), lambda i: (i, 0))]
        out_shape = out_shape + [jax.ShapeDtypeStruct((m, n_logit), F32)]
    return pl.pallas_call(
        functools.partial(_ln_body, alpha=alpha, routed=router_w is not None),
        grid=(m // tm,),
        in_specs=[row, row, vec, vec] + r_specs,
        out_specs=out_specs,
        out_shape=out_shape,
        compiler_params=_cparams(("parallel",)),
        name="residual_layer_norm",
    )(x, h, g.reshape(1, d), b.reshape(1, d), *r_args)


def _combine_ln_body(dest_ref, x_ref, ys_ref, gate_ref, g_ref, b_ref, oa_ref, ob_ref, buf_ref, sem_ref,
                     *, alpha, head_tiles, tm, top_k):
    i = pl.program_id(0)
    n_tiles = pl.num_programs(0)

    def issue(tile, slot):
        base = tile * (tm * top_k)

        def body(r, carry):
            for k in range(top_k):
                row = dest_ref[base + r * top_k + k]
                pltpu.make_async_copy(ys_ref.at[pl.ds(row, 1)], buf_ref.at[slot, k, pl.ds(r, 1)],
                                      sem_ref.at[slot]).start()
            return carry

        lax.fori_loop(0, tm, body, 0, unroll=8)

    @pl.when(i == 0)
    def _():
        issue(0, 0)

    @pl.when(i + 1 < n_tiles)
    def _():
        issue(i + 1, (i + 1) % 2)

    slot = i % 2
    for k in range(top_k):
        pltpu.make_async_copy(ys_ref.at[pl.ds(0, tm)], buf_ref.at[slot, k], sem_ref.at[slot]).wait()
    h = buf_ref[slot, 0] * gate_ref[:, 0:1]
    for k in range(1, top_k):
        h = h + buf_ref[slot, k] * gate_ref[:, k:k + 1]
    y = _layer_norm_rows(alpha * x_ref[...] + h, g_ref, b_ref)
    _store_rows(y, oa_ref, ob_ref, head_tiles)


def moe_combine_layer_norm(x, ys, dest, gates, g, b, alpha, split_rows=None):
    m, d = x.shape
    top_k = dest.shape[1]
    tm = LN_ROWS
    assert m % tm == 0 and ys.dtype == F32
    head_tiles, out_specs, out_shape = _ln_out_specs(m, d, tm, split_rows)
    row = pl.BlockSpec((tm, d), lambda i, *_: (i, 0))
    vec = pl.BlockSpec((1, d), lambda i, *_: (0, 0))
    return pl.pallas_call(
        functools.partial(_combine_ln_body, alpha=alpha, head_tiles=head_tiles, tm=tm, top_k=top_k),
        grid_spec=pltpu.PrefetchScalarGridSpec(
            num_scalar_prefetch=1,
            grid=(m // tm,),
            in_specs=[row, pl.BlockSpec(memory_space=pl.ANY),
                      pl.BlockSpec((tm, top_k), lambda i, *_: (i, 0)), vec, vec],
            out_specs=out_specs,
            scratch_shapes=[pltpu.VMEM((2, top_k, tm, d), F32), pltpu.SemaphoreType.DMA((2,))]),
        out_shape=out_shape,
        compiler_params=_cparams(("arbitrary",)),
        name="moe_combine_layer_norm",
    )(dest.reshape(-1).astype(jnp.int32), x, ys, gates, g.reshape(1, d), b.reshape(1, d))


SUBLANES = 8


def _shift_mix_body(x_ref, halo_ref, shift_ref, mu_ref, o_ref, *, tm, tp, ts, prompt_tiles):
    i = pl.program_id(0)
    x = x_ref[...]
    row = lax.broadcasted_iota(jnp.int32, (tm, 1), 0)
    prev = jnp.where(row == 0, halo_ref[SUBLANES - 1:SUBLANES, :], pltpu.roll(x, 1, 0))
    prompt_start = jnp.logical_and(row == 0, (i * tm) % tp == 0)
    prev_prompt = jnp.where(prompt_start, 0.0, prev)
    nseq = tm // ts
    carried = jnp.broadcast_to(shift_ref[...][:, None, :], (nseq, ts, x.shape[1])).reshape(x.shape)
    prev_sample = jnp.where(row % ts == 0, carried, prev)
    xx = jnp.where(i < prompt_tiles, prev_prompt, prev_sample) - x
    for j in range(o_ref.shape[0]):
        o_ref[j] = (x + xx * mu_ref[j:j + 1, :]).astype(BF16)


def shift_mix(x, shift0, mu, *, n_prompt_rows, tp, ts):
    m, d = x.shape
    nmix = mu.shape[0]
    tm = 512
    assert tp % tm == 0 and tm % ts == 0 and n_prompt_rows % tm == 0 and m % tm == 0
    prompt_tiles = n_prompt_rows // tm
    nseq = tm // ts
    assert shift0.shape[0] % nseq == 0
    halo_blocks = tm // SUBLANES
    return pl.pallas_call(
        functools.partial(_shift_mix_body, tm=tm, tp=tp, ts=ts, prompt_tiles=prompt_tiles),
        grid=(m // tm,),
        in_specs=[pl.BlockSpec((tm, d), lambda i: (i, 0)),
                  pl.BlockSpec((SUBLANES, d), lambda i: (jnp.maximum(i * halo_blocks - 1, 0), 0)),
                  pl.BlockSpec((nseq, d), lambda i: (jnp.maximum(i - prompt_tiles, 0), 0)),
                  pl.BlockSpec((nmix, d), lambda i: (0, 0))],
        out_specs=pl.BlockSpec((nmix, tm, d), lambda i: (0, i, 0)),
        out_shape=jax.ShapeDtypeStruct((nmix, m, d), BF16),
        compiler_params=_cparams(("parallel",)),
        name="rwkv_shift_mix",
    )(x, x, shift0, mu)


def _chunk_state_init(c, ncp, cps, state_ref, s0_ref):
    @pl.when(jnp.logical_and(c < ncp, c % cps == 0))
    def _():
        state_ref[...] = jnp.zeros(state_ref.shape, state_ref.dtype)

    @pl.when(c >= ncp)
    def _():
        state_ref[...] = s0_ref[0]


def _dla_body(*refs, heads, dk, dv, mode, ncp, cps):
    o_ref, sp_ref, ss_ref, st_ref = refs[-4:]
    refs = refs[:-4]
    if mode == "gla":
        q_ref, k_ref, v_ref, g_ref, gate_ref, ng_ref, s0_ref = refs
    else:
        q_ref, k_ref, v_ref, cos_ref, sin_ref, gate_ref, ng_ref, s0_ref = refs
    c = pl.program_id(0)
    _chunk_state_init(c, ncp, cps, st_ref, s0_ref)

    t_row = lax.broadcasted_iota(jnp.int32, (CHUNK, CHUNK), 0)
    t_col = lax.broadcasted_iota(jnp.int32, (CHUNK, CHUNK), 1)
    causal = t_row >= t_col
    if mode == "gla":
        cum_all = _chunk_prefix_sum(g_ref[...])
    else:
        width = heads * dk
        even = (lax.broadcasted_iota(jnp.int32, (CHUNK, width), 1) % 2) == 0
        q_all = q_ref[...]
        k_all = k_ref[...]
        q_sw = jnp.where(even, pltpu.roll(q_all, width - 1, 1), pltpu.roll(q_all, 1, 1))
        k_sw = jnp.where(even, pltpu.roll(k_all, width - 1, 1), pltpu.roll(k_all, 1, 1))
        cos = cos_ref[...]
        sin = sin_ref[...]
        frame = (lax.broadcasted_iota(jnp.int32, (CHUNK, 1), 0) + 1).astype(F32)

    for h in range(heads):
        ks = slice(h * dk, (h + 1) * dk)
        vs = slice(h * dv, (h + 1) * dv)
        v = v_ref[:, vs]
        s_prev = st_ref[h]
        if mode == "gla":
            q = q_ref[:, ks]
            k = k_ref[:, ks]
            cum = cum_all[:, ks]
            total = cum[CHUNK - 1:CHUNK, :]
            total_col = jnp.transpose(jnp.broadcast_to(total, (LANES, dk)))[:, :1]
            state_decay = jnp.exp(total_col)
            q_dec = q * jnp.exp(cum)
            k_inv = k * jnp.exp(-cum)
            k_tail = k * jnp.exp(total - cum)
        else:
            q = q_all[:, ks] * cos + q_sw[:, ks] * sin
            k = (k_all[:, ks] * cos + k_sw[:, ks] * sin) * (dk ** -0.5)
            log_gamma = math.log1p(-(2.0 ** (-5.0 - h)))
            cum = frame * log_gamma
            total = CHUNK * log_gamma
            state_decay = math.exp(total)
            q_dec = q * jnp.exp(cum)
            k_inv = k * jnp.exp(-cum)
            k_tail = k * jnp.exp(total - cum)
        scores = jnp.where(causal, _bdot_nt(q_dec, k_inv), 0.0)
        o = _bdot(q_dec, s_prev) + _bdot(scores, v)
        k_tail_t = jnp.transpose(k_tail)
        st_ref[h] = state_decay * s_prev + _bdot(k_tail_t, v)
        if mode == "gla":
            o = o * lax.rsqrt(jnp.mean(o * o, -1, keepdims=True) + GLA_NORM_EPS) * ng_ref[...]
        else:
            mu = jnp.mean(o, -1, keepdims=True)
            oc = o - mu
            var = jnp.mean(oc * oc, -1, keepdims=True)
            o = oc * lax.rsqrt(var + RET_GN_EPS) * ng_ref[:, vs]
        if mode == "gla":
            o_ref[:, vs] = (o * gate_ref[:, vs]).astype(BF16)
        else:
            o_ref[:, vs] = (gate_ref[:, vs] * o).astype(BF16)

    @pl.when(jnp.logical_and(c < ncp, c % cps == cps - 1))
    def _():
        sp_ref[0] = st_ref[...]

    @pl.when(c >= ncp)
    def _():
        ss_ref[0] = st_ref[...]


def _seq_state_specs(state_shape, ncp, cps, n_prompt, slot):
    blk = (None, 1) + tuple(state_shape)
    zeros = (0,) * len(state_shape)
    s0_spec = pl.BlockSpec(blk, lambda c: (slot, jnp.maximum(c - ncp, 0)) + zeros)
    sp_spec = pl.BlockSpec(blk, lambda c: (0, jnp.minimum(c // cps, n_prompt - 1)) + zeros)
    ss_spec = pl.BlockSpec(blk, lambda c: (0, jnp.maximum(c - ncp, 0)) + zeros)
    return s0_spec, sp_spec, ss_spec


def _state_out_shapes(n_prompt, n_sample, state_shape):
    return [jax.ShapeDtypeStruct((1, n_prompt) + tuple(state_shape), F32),
            jax.ShapeDtypeStruct((1, n_sample) + tuple(state_shape), F32)]


def decay_attention(q, k, v, extra, gate, norm_g, s0, slot, *, mode, heads, n_prompt, cps):
    nt = q.shape[0]
    dk = q.shape[1] // heads
    dv = v.shape[1] // heads
    n_sample = s0.shape[1]
    ncp = n_prompt * cps
    nchunks = nt // CHUNK
    assert nchunks == ncp + n_sample
    rowk = pl.BlockSpec((CHUNK, heads * dk), lambda c: (c, 0))
    rowv = pl.BlockSpec((CHUNK, heads * dv), lambda c: (c, 0))
    s0_spec, sp_spec, ss_spec = _seq_state_specs((heads, dk, dv), ncp, cps, n_prompt, slot)
    if mode == "gla":
        extra_specs = [rowk]
        ng_spec = pl.BlockSpec((1, dv), lambda c: (0, 0))
        norm_g = norm_g.reshape(1, dv)
    else:
        pos_spec = pl.BlockSpec((CHUNK, dk), lambda c: (jnp.where(c < ncp, c % cps, cps), 0))
        extra_specs = [pos_spec, pos_spec]
        ng_spec = pl.BlockSpec((1, heads * dv), lambda c: (0, 0))
        norm_g = norm_g.reshape(1, heads * dv)
    return pl.pallas_call(
        functools.partial(_dla_body, heads=heads, dk=dk, dv=dv, mode=mode, ncp=ncp, cps=cps),
        grid=(nchunks,),
        in_specs=[rowk, rowk, rowv] + extra_specs + [rowv, ng_spec, s0_spec],
        out_specs=[rowv, sp_spec, ss_spec],
        out_shape=([jax.ShapeDtypeStruct((nt, heads * dv), BF16)]
                   + _state_out_shapes(n_prompt, n_sample, (heads, dk, dv))),
        scratch_shapes=[pltpu.VMEM((heads, dk, dv), F32)],
        compiler_params=_cparams(("arbitrary",)),
        name="decay_attention_" + mode,
    )(q, k, v, *extra, gate, norm_g, s0)


def _rwkv_body(r_ref, k_ref, v_ref, lw_ref, a_ref, g_ref, kk_ref, ka_ref, rk_ref, gng_ref, gnb_ref,
               s0_ref, o_ref, sp_ref, ss_ref, st_ref, *, pairs, ncp, cps):
    c = pl.program_id(0)
    n = RWKV_HEAD
    w2 = 2 * n

    @pl.when(jnp.logical_and(c < ncp, c % cps == 0))
    def _():
        st_ref[...] = jnp.zeros(st_ref.shape, F32)

    @pl.when(c >= ncp)
    def _():
        zero = jnp.zeros((n, n), F32)
        for p in range(pairs):
            top = jnp.concatenate([s0_ref[0, 2 * p], zero], axis=1)
            bot = jnp.concatenate([zero, s0_ref[0, 2 * p + 1]], axis=1)
            st_ref[p] = jnp.concatenate([top, bot], axis=0)

    def paired(ref):
        x = ref[...]
        return jnp.stack([x[:, p * w2:(p + 1) * w2] for p in range(pairs)])

    lane = lax.broadcasted_iota(jnp.int32, (1, 1, w2), 2)
    first = lane < n

    def head_sum(x):
        s_a = jnp.sum(jnp.where(first, x, 0.0), -1, keepdims=True)
        s_b = jnp.sum(jnp.where(first, 0.0, x), -1, keepdims=True)
        return jnp.where(first, s_a, s_b)

    def stacked(x):
        return jnp.concatenate([jnp.where(first, x, 0.0), jnp.where(first, 0.0, x)], axis=1)

    def bmm(a, b):
        return lax.dot_general(a.astype(BF16), b.astype(BF16), (((2,), (1,)), ((0,), (0,))),
                               preferred_element_type=F32)

    def bmm_nt(a, b):
        return lax.dot_general(a.astype(BF16), b.astype(BF16), (((2,), (2,)), ((0,), (0,))),
                               preferred_element_type=F32)

    r = paired(r_ref)
    k_raw = paired(k_ref)
    v = paired(v_ref)
    lw = paired(lw_ref)
    a = paired(a_ref)
    k_k = paired(kk_ref)
    k_a = paired(ka_ref)
    r_k = paired(rk_ref)

    kk = k_raw * k_k
    kk = kk / jnp.maximum(jnp.sqrt(head_sum(kk * kk)), 1e-12)
    k_h = k_raw * (1.0 + (a - 1.0) * k_a)
    a_vec = -kk
    b_vec = kk * a

    cum = paired_value(_chunk_prefix_sum(lw_ref[...]), pairs, w2)
    total = cum[:, CHUNK - 1:CHUNK, :]
    p_now = jnp.exp(cum)
    p_prev = jnp.exp(cum - lw)
    p_inv = jnp.exp(-cum)
    p_tail = jnp.exp(total - cum)

    lhs = jnp.concatenate([stacked(a_vec * p_prev), stacked(r * p_now)], axis=1)
    rhs = jnp.concatenate([stacked(b_vec * p_inv), stacked(k_h * p_inv)], axis=1)
    sc = bmm_nt(lhs, rhs)
    s_prev = st_ref[...]
    sr = bmm_nt(lhs, s_prev)

    i_row = lax.broadcasted_iota(jnp.int32, (1, w2, w2), 1)
    i_col = lax.broadcasted_iota(jnp.int32, (1, w2, w2), 2)
    same = (i_row // n) == (i_col // n)
    strict = jnp.logical_and(same, (i_col % n) < (i_row % n))
    incl = jnp.logical_and(same, (i_col % n) <= (i_row % n))
    a_ab = jnp.where(strict, sc[:, :w2, :w2], 0.0)
    a_ak = jnp.where(strict, sc[:, :w2, w2:], 0.0)
    a_rb = jnp.where(incl, sc[:, w2:, :w2], 0.0)
    a_rk = jnp.where(incl, sc[:, w2:, w2:], 0.0)
    u0 = sr[:, :w2]
    y0 = sr[:, w2:]
    v_st = stacked(v)

    eye = (i_row == i_col).astype(F32)
    t_inv = eye + a_ab
    power = a_ab
    span = 1
    while 2 * span < CHUNK:
        power = bmm(power, power)
        t_inv = t_inv + bmm(t_inv, power)
        span *= 2

    u_st = bmm(t_inv, u0 + bmm(a_ak, v_st))
    uv = jnp.concatenate([u_st, v_st], axis=1)
    y_st = y0 + bmm(jnp.concatenate([a_rb, a_rk], axis=2), uv)
    y = y_st[:, :n] + y_st[:, n:]

    tails = jnp.concatenate([stacked(b_vec * p_tail), stacked(k_h * p_tail)], axis=1)
    uv_t = jnp.swapaxes(uv, 1, 2)
    s_new = s_prev * jnp.exp(total) + bmm(uv_t, tails)
    st_ref[...] = s_new

    mu = head_sum(y) * (1.0 / n)
    yc = y - mu
    var = head_sum(yc * yc) * (1.0 / n)
    yn = yc * lax.rsqrt(var + RWKV_GN_EPS) * paired(gng_ref) + paired(gnb_ref)
    out = yn + head_sum(r * k_h * r_k) * v
    gate = paired(g_ref)
    for p in range(pairs):
        o_ref[:, p * w2:(p + 1) * w2] = (out[p] * gate[p]).astype(BF16)

    def store_state(dst_ref):
        for p in range(pairs):
            dst_ref[0, 2 * p] = s_new[p, :n, :n]
            dst_ref[0, 2 * p + 1] = s_new[p, n:, n:]

    @pl.when(jnp.logical_and(c < ncp, c % cps == cps - 1))
    def _():
        store_state(sp_ref)

    @pl.when(c >= ncp)
    def _():
        store_state(ss_ref)


def paired_value(x, pairs, w2):
    return jnp.stack([x[:, p * w2:(p + 1) * w2] for p in range(pairs)])


def rwkv7_attention(r, k, v, lw, a, g, k_k, k_a, r_k, gn_g, gn_b, s0, slot, *, n_prompt, cps):
    nt, d = r.shape
    heads = d // RWKV_HEAD
    pairs = heads // 2
    n_sample = s0.shape[1]
    ncp = n_prompt * cps
    assert nt // CHUNK == ncp + n_sample
    row = pl.BlockSpec((CHUNK, d), lambda c: (c, 0))
    vec = pl.BlockSpec((1, d), lambda c: (0, 0))
    s0_spec, sp_spec, ss_spec = _seq_state_specs((heads, RWKV_HEAD, RWKV_HEAD), ncp, cps, n_prompt,
                                                 slot)
    vecs = [u.reshape(1, d) for u in (k_k, k_a, r_k, gn_g, gn_b)]
    return pl.pallas_call(
        functools.partial(_rwkv_body, pairs=pairs, ncp=ncp, cps=cps),
        grid=(nt // CHUNK,),
        in_specs=[row] * 6 + [vec] * 5 + [s0_spec],
        out_specs=[row, sp_spec, ss_spec],
        out_shape=([jax.ShapeDtypeStruct((nt, d), BF16)]
                   + _state_out_shapes(n_prompt, n_sample, (heads, RWKV_HEAD, RWKV_HEAD))),
        scratch_shapes=[pltpu.VMEM((pairs, 2 * RWKV_HEAD, 2 * RWKV_HEAD), F32)],
        compiler_params=_cparams(("arbitrary",)),
        name="rwkv7_attention",
    )(r, k, v, lw, a, g, *vecs, s0)


def _expert_weight_pipeline(be_ref, chg_ref, nxt_ref, hbm_refs, stage_refs, cache_refs, sem_ref,
                            tile):
    j = pl.program_id(0)
    b = pl.program_id(1)
    n_pass = pl.num_programs(0)

    def copies(expert, col_pass):
        col = pl.multiple_of(col_pass * tile, tile)
        return [pltpu.make_async_copy(hbm.at[expert, :, pl.ds(col, tile)], stage, sem_ref.at[i])
                for i, (hbm, stage) in enumerate(zip(hbm_refs, stage_refs))]

    @pl.when(jnp.logical_and(j == 0, b == 0))
    def _():
        for cp in copies(be_ref[0], 0):
            cp.start()

    @pl.when(chg_ref[b] == 1)
    def _():
        for cp in copies(be_ref[b], j):
            cp.wait()
        for stage, cache in zip(stage_refs, cache_refs):
            cache[...] = stage[...].astype(BF16)
        nxt = nxt_ref[b]

        @pl.when(nxt >= 0)
        def _():
            for cp in copies(be_ref[jnp.maximum(nxt, 0)], j):
                cp.start()

        @pl.when(jnp.logical_and(nxt < 0, j + 1 < n_pass))
        def _():
            for cp in copies(be_ref[0], j + 1):
                cp.start()


def _row_block_compute(valid, out_ref, rows_fn):
    block = out_ref.shape[0]
    half = block // 2

    @pl.when(valid > half)
    def _():
        out_ref[...] = rows_fn(0, block)

    @pl.when(jnp.logical_and(valid > 0, valid <= half))
    def _():
        out_ref[:half, :] = rows_fn(0, half)
        out_ref[half:, :] = jnp.zeros((block - half, out_ref.shape[1]), out_ref.dtype)

    @pl.when(valid <= 0)
    def _():
        out_ref[...] = jnp.zeros(out_ref.shape, out_ref.dtype)


def _gate_up_body(be_ref, chg_ref, nxt_ref, valid_ref, x_ref, w1_ref, w3_ref, h_ref,
                  w1s_ref, w3s_ref, w1c_ref, w3c_ref, sem_ref, *, tile):
    b = pl.program_id(1)
    _expert_weight_pipeline(be_ref, chg_ref, nxt_ref, (w1_ref, w3_ref), (w1s_ref, w3s_ref),
                            (w1c_ref, w3c_ref), sem_ref, tile)

    def rows(lo, n):
        x = x_ref[lo:lo + n, :]
        gate = jnp.dot(x, w1c_ref[...], preferred_element_type=F32)
        up = jnp.dot(x, w3c_ref[...], preferred_element_type=F32)
        return (gate * jax.nn.sigmoid(gate) * up).astype(BF16)

    _row_block_compute(valid_ref[b], h_ref, rows)


def _down_body(be_ref, chg_ref, nxt_ref, valid_ref, h_ref, w2_ref, y_ref, w2s_ref, w2c_ref, sem_ref,
               *, tile):
    b = pl.program_id(1)
    _expert_weight_pipeline(be_ref, chg_ref, nxt_ref, (w2_ref,), (w2s_ref,), (w2c_ref,), sem_ref,
                            tile)

    def rows(lo, n):
        return jnp.dot(h_ref[lo:lo + n, :], w2c_ref[...],
                       preferred_element_type=F32).astype(y_ref.dtype)

    _row_block_compute(valid_ref[b], y_ref, rows)


def _block_meta(block_expert, rows_valid):
    be = block_expert.astype(jnp.int32)
    nb = be.shape[0]
    changed = jnp.concatenate([jnp.ones((1,), jnp.int32),
                               (be[1:] != be[:-1]).astype(jnp.int32)])
    idx = jnp.arange(nb, dtype=jnp.int32)
    opener = jnp.where(changed == 1, idx, nb)
    after = lax.cummin(jnp.concatenate([opener[1:], jnp.full((1,), nb, jnp.int32)]), reverse=True)
    nxt = jnp.where(after >= nb, -1, after).astype(jnp.int32)
    return be, changed, nxt, rows_valid.astype(jnp.int32)


def swiglu_gate_up(xs, block_expert, rows_valid, w1, w3, block, tile_f=(512, 256, 128)):
    rows, d = xs.shape
    f = w1.shape[-1]
    tf = _pick(f, tile_f)
    hbm = pl.BlockSpec(memory_space=pl.ANY)
    return pl.pallas_call(
        functools.partial(_gate_up_body, tile=tf),
        grid_spec=pltpu.PrefetchScalarGridSpec(
            num_scalar_prefetch=4,
            grid=(f // tf, rows // block),
            in_specs=[pl.BlockSpec((block, d), lambda j, b, *_: (b, 0)), hbm, hbm],
            out_specs=pl.BlockSpec((block, tf), lambda j, b, *_: (b, j)),
            scratch_shapes=[pltpu.VMEM((d, tf), F32), pltpu.VMEM((d, tf), F32),
                            pltpu.VMEM((d, tf), BF16), pltpu.VMEM((d, tf), BF16),
                            pltpu.SemaphoreType.DMA((2,))]),
        out_shape=jax.ShapeDtypeStruct((rows, f), BF16),
        compiler_params=_cparams(("arbitrary", "arbitrary")),
        name="swiglu_gate_up",
    )(*_block_meta(block_expert, rows_valid), xs, w1, w3)


def swiglu_down(h, block_expert, rows_valid, w2, block, out_dtype=F32):
    rows, f = h.shape
    d = w2.shape[-1]
    tn = _pick(d, (512, 256, 128))
    return pl.pallas_call(
        functools.partial(_down_body, tile=tn),
        grid_spec=pltpu.PrefetchScalarGridSpec(
            num_scalar_prefetch=4,
            grid=(d // tn, rows // block),
            in_specs=[pl.BlockSpec((block, f), lambda j, b, *_: (b, 0)),
                      pl.BlockSpec(memory_space=pl.ANY)],
            out_specs=pl.BlockSpec((block, tn), lambda j, b, *_: (b, j)),
            scratch_shapes=[pltpu.VMEM((f, tn), F32), pltpu.VMEM((f, tn), BF16),
                            pltpu.SemaphoreType.DMA((1,))]),
        out_shape=jax.ShapeDtypeStruct((rows, d), out_dtype),
        compiler_params=_cparams(("arbitrary", "arbitrary")),
        name="swiglu_down",
    )(*_block_meta(block_expert, rows_valid), h, w2)


def _dispatch_body(src_ref, nu_ref, x_ref, o_ref, buf_ref, sem_ref, *, block):
    b = pl.program_id(0)
    n_used = nu_ref[0]

    def issue(blk, slot):
        base = blk * block

        def body(r, carry):
            pltpu.make_async_copy(x_ref.at[pl.ds(src_ref[base + r], 1)],
                                  buf_ref.at[slot, pl.ds(r, 1)], sem_ref.at[slot]).start()
            return carry

        lax.fori_loop(0, block, body, 0, unroll=8)

    @pl.when(jnp.logical_and(b == 0, n_used > 0))
    def _():
        issue(0, 0)

    @pl.when(b + 1 < n_used)
    def _():
        issue(b + 1, (b + 1) % 2)

    @pl.when(b < n_used)
    def _():
        slot = b % 2
        pltpu.make_async_copy(x_ref.at[pl.ds(0, block)], buf_ref.at[slot], sem_ref.at[slot]).wait()
        o_ref[...] = buf_ref[slot].astype(BF16)

    @pl.when(b >= n_used)
    def _():
        o_ref[...] = jnp.zeros(o_ref.shape, BF16)


def moe_dispatch(x, src_tok, n_used, block):
    n, d = x.shape
    rows = src_tok.shape[0]
    return pl.pallas_call(
        functools.partial(_dispatch_body, block=block),
        grid_spec=pltpu.PrefetchScalarGridSpec(
            num_scalar_prefetch=2,
            grid=(rows // block,),
            in_specs=[pl.BlockSpec(memory_space=pl.ANY)],
            out_specs=pl.BlockSpec((block, d), lambda b, *_: (b, 0)),
            scratch_shapes=[pltpu.VMEM((2, block, d), F32), pltpu.SemaphoreType.DMA((2,))]),
        out_shape=jax.ShapeDtypeStruct((rows, d), BF16),
        compiler_params=_cparams(("arbitrary",)),
        name="moe_dispatch",
    )(src_tok.astype(jnp.int32), jnp.reshape(n_used, (1,)).astype(jnp.int32), x)


def _pad_cols(w, mult=LANES):
    pad = (-w.shape[-1]) % mult
    return jnp.pad(w, [(0, 0)] * (w.ndim - 1) + [(0, pad)]) if pad else w


def _pad_rows(w, mult=LANES):
    pad = (-w.shape[-2]) % mult
    return jnp.pad(w, [(0, 0)] * (w.ndim - 2) + [(0, pad), (0, 0)]) if pad else w


DENSE_GATE_UP_BLOCK = 1024
DENSE_DOWN_BLOCK = 512
MOE_GATE_UP_TILE_F = (1024, 512, 256, 128)


def dense_swiglu(xb, w1, w3, w2, slot):
    rows = xb.shape[0]
    assert rows % DENSE_GATE_UP_BLOCK == 0 and rows % DENSE_DOWN_BLOCK == 0
    nb_a = rows // DENSE_GATE_UP_BLOCK
    nb_b = rows // DENSE_DOWN_BLOCK
    h = swiglu_gate_up(xb, jnp.full((nb_a,), slot, jnp.int32),
                       jnp.full((nb_a,), DENSE_GATE_UP_BLOCK, jnp.int32), w1, w3, DENSE_GATE_UP_BLOCK)
    return swiglu_down(h, jnp.full((nb_b,), slot, jnp.int32),
                       jnp.full((nb_b,), DENSE_DOWN_BLOCK, jnp.int32), w2, DENSE_DOWN_BLOCK)


def moe_swiglu(logits, xf, w1, w3, w2, slot):
    n, d = xf.shape
    n_exp = logits.shape[-1]
    top_val, top_idx = lax.top_k(logits, MOE_TOP_K)
    gates = jax.nn.softmax(top_val, axis=-1)
    slots = n * MOE_TOP_K
    flat_e = top_idx.reshape(-1)
    onehot = (flat_e[:, None] == jnp.arange(n_exp, dtype=flat_e.dtype)[None, :]).astype(jnp.int32)
    rank = jnp.sum((jnp.cumsum(onehot, axis=0) - onehot) * onehot, axis=1)
    counts = jnp.sum(onehot, axis=0)
    start = jnp.cumsum(counts) - counts
    padded = (counts + MOE_BLOCK - 1) // MOE_BLOCK * MOE_BLOCK
    pend = jnp.cumsum(padded)
    pstart = pend - padded
    nb = (slots + n_exp * (MOE_BLOCK - 1) + MOE_BLOCK - 1) // MOE_BLOCK
    dest = (pstart[flat_e] + rank).astype(jnp.int32)
    block_start = jnp.arange(nb, dtype=jnp.int32) * MOE_BLOCK
    expert_of_block = jnp.clip(jnp.searchsorted(pend, block_start, side="right"), 0, n_exp - 1)
    order = jnp.argsort(flat_e, stable=True)
    e_row = jnp.repeat(expert_of_block, MOE_BLOCK)
    rank_row = jnp.arange(nb * MOE_BLOCK, dtype=jnp.int32) - pstart[e_row]
    valid = rank_row < counts[e_row]
    slot_row = order[jnp.clip(start[e_row] + rank_row, 0, slots - 1)]
    src_tok = jnp.where(valid, slot_row // MOE_TOP_K, 0).astype(jnp.int32)
    block_expert = expert_of_block.astype(jnp.int32) + slot * n_exp
    n_used = pend[-1] // MOE_BLOCK
    xs = moe_dispatch(xf, src_tok, n_used, MOE_BLOCK)
    e1 = w1.reshape((-1,) + w1.shape[2:])
    e3 = w3.reshape((-1,) + w3.shape[2:])
    e2 = w2.reshape((-1,) + w2.shape[2:])
    rows_valid = jnp.clip((pstart + counts)[expert_of_block] - block_start, 0, MOE_BLOCK)
    hs = swiglu_gate_up(xs, block_expert, rows_valid, e1, e3, MOE_BLOCK, tile_f=MOE_GATE_UP_TILE_F)
    ys = swiglu_down(hs, block_expert, rows_valid, e2, MOE_BLOCK)
    return ys, dest.reshape(n, MOE_TOP_K), gates


def _rope_tables(dk, cps, past_len):
    half = dk // 2
    inv_freq = 1.0 / (RET_ROPE_BASE ** jnp.linspace(0.0, 1.0, half, dtype=F32))
    pos = jnp.concatenate([jnp.arange(cps * CHUNK, dtype=jnp.int32),
                           past_len + jnp.arange(CHUNK, dtype=jnp.int32)])
    ang = pos.astype(F32)[:, None] * inv_freq[None, :]
    cos = jnp.repeat(jnp.cos(ang), 2, axis=1)
    sin = jnp.stack([-jnp.sin(ang), jnp.sin(ang)], axis=-1).reshape(pos.shape[0], dk)
    return cos, sin


def kernel(x_prompt, x_sample, state_gla, state_rwkv, state_shift, state_ret, ln_g, ln_b,
           gla_wq, gla_wk, gla_wv, gla_wr, gla_wa1, gla_wa2, gla_ba, gla_norm_g, gla_wo,
           rwkv_mu, rwkv_wr, rwkv_wk, rwkv_wv, rwkv_wo, rwkv_w0, rwkv_w1, rwkv_w2, rwkv_a0, rwkv_a1,
           rwkv_a2, rwkv_g1, rwkv_g2, rwkv_k_k, rwkv_k_a, rwkv_r_k, rwkv_gn_g, rwkv_gn_b,
           ret_wq, ret_wk, ret_wv, ret_wg, ret_gn_g, ret_wo,
           ffn_w1, ffn_w3, ffn_w2, moe_router, moe_w1, moe_w3, moe_w2):
    bp, tp, d = x_prompt.shape
    bs, ts, _ = x_sample.shape
    assert ts == CHUNK and tp % CHUNK == 0
    depth = ln_g.shape[0]
    alpha = (2.0 * depth) ** 0.25
    cps = tp // CHUNK
    past_len = tp
    n_prompt_rows = bp * tp
    seq = dict(n_prompt=bp, cps=cps)

    gla_heads = state_gla.shape[2]
    gla_dk = state_gla.shape[3]
    ret_heads = state_ret.shape[2]
    ret_dk = state_ret.shape[3]

    xf = jnp.concatenate([x_prompt.reshape(-1, d), x_sample.reshape(-1, d)], axis=0)
    xb = xf.astype(BF16)

    gla_states, rwkv_states, ret_states = [], [], []
    new_shift_p, new_shift_s = [], []

    def stacked(per_layer, which):
        parts = [states[which] for states in per_layer]
        return parts[0] if len(parts) == 1 else jnp.concatenate(parts, axis=0)

    last_rows = jnp.concatenate([jnp.arange(bp, dtype=jnp.int32) * tp + (tp - 1),
                                 n_prompt_rows + jnp.arange(bs, dtype=jnp.int32) * ts + (ts - 1)])
    for i in range(depth):
        kind, slot = i % 3, i // 3
        if kind == 0:
            q = matmul(xb, gla_wq, slot, scale=gla_dk ** -0.5)
            k = matmul(xb, gla_wk, slot)
            v = matmul(xb, gla_wv, slot)
            gate = matmul(xb, gla_wr, slot, act="silu")
            low = matmul(xb, _pad_cols(gla_wa1), slot, out_dtype=BF16)
            log_alpha = matmul(low, _pad_rows(gla_wa2), slot, act="gla_gate", bias=gla_ba[slot])
            o, *states = decay_attention(q, k, v, (log_alpha,), gate, gla_norm_g[slot], state_gla,
                                         slot, mode="gla", heads=gla_heads, **seq)
            gla_states.append(states)
            h = matmul(o, gla_wo, slot)
        elif kind == 1:
            mixes = shift_mix(xf, state_shift[slot], rwkv_mu[slot], n_prompt_rows=n_prompt_rows,
                              tp=tp, ts=ts)
            i_r, i_w, i_k, i_v, i_a, i_g = range(6)
            r = matmul(mixes, rwkv_wr, slot, x_slot=i_r)
            k = matmul(mixes, rwkv_wk, slot, x_slot=i_k)
            v = matmul(mixes, rwkv_wv, slot, x_slot=i_v)
            w_mid = matmul(mixes, _pad_cols(rwkv_w1), slot, x_slot=i_w, act="tanh", out_dtype=BF16)
            log_decay = matmul(w_mid, _pad_rows(rwkv_w2), slot, act="rwkv_decay", bias=rwkv_w0[slot])
            a_mid = matmul(mixes, _pad_cols(rwkv_a1), slot, x_slot=i_a, out_dtype=BF16)
            a = matmul(a_mid, _pad_rows(rwkv_a2), slot, act="sigmoid", bias=rwkv_a0[slot])
            g_mid = matmul(mixes, _pad_cols(rwkv_g1), slot, x_slot=i_g, act="sigmoid", out_dtype=BF16)
            g = matmul(g_mid, _pad_rows(rwkv_g2), slot)
            o, *states = rwkv7_attention(r, k, v, log_decay, a, g, rwkv_k_k[slot], rwkv_k_a[slot],
                                         rwkv_r_k[slot].reshape(-1), rwkv_gn_g[slot],
                                         rwkv_gn_b[slot], state_rwkv, slot, **seq)
            rwkv_states.append(states)
            h = matmul(o, rwkv_wo, slot)
            ends = jnp.take(xf, last_rows, axis=0, mode="clip")
            new_shift_p.append(ends[:bp])
            new_shift_s.append(ends[bp:])
        else:
            q = matmul(xb, ret_wq, slot)
            k = matmul(xb, ret_wk, slot)
            v = matmul(xb, ret_wv, slot)
            gate = matmul(xb, ret_wg, slot, act="silu")
            cos, sin = _rope_tables(ret_dk, cps, past_len)
            o, *states = decay_attention(q, k, v, (cos, sin), gate, ret_gn_g[slot], state_ret, slot,
                                         mode="ret", heads=ret_heads, **seq)
            ret_states.append(states)
            h = matmul(o, ret_wo, slot)
        fslot = i // 2
        is_moe = i % 2 == 1
        router_w = _pad_cols(moe_router[fslot]) if is_moe else None
        xf, xb, *logits = residual_layer_norm(xf, h, ln_g[i, 0], ln_b[i, 0], alpha, router_w=router_w)
        last = i == depth - 1
        if is_moe:
            ys, dest, gates = moe_swiglu(logits[0][:, :moe_router.shape[-1]], xf, moe_w1, moe_w3,
                                         moe_w2, fslot)
            out_a, out_b = moe_combine_layer_norm(xf, ys, dest, gates, ln_g[i, 1], ln_b[i, 1], alpha,
                                                  split_rows=n_prompt_rows if last else None)
        else:
            h = dense_swiglu(xb, ffn_w1, ffn_w3, ffn_w2, fslot)
            out_a, out_b = residual_layer_norm(xf, h, ln_g[i, 1], ln_b[i, 1], alpha)
            if last:
                out_a, out_b = out_a[:n_prompt_rows], out_a[n_prompt_rows:]
        if last:
            y_prompt, y_sample = out_a.reshape(bp, tp, d), out_b.reshape(bs, ts, d)
        else:
            xf, xb = out_a, out_b

    return (y_prompt, y_sample,
            stacked(gla_states, 0), stacked(rwkv_states, 0), jnp.stack(new_shift_p),
            stacked(ret_states, 0),
            stacked(gla_states, 1), stacked(rwkv_states, 1), jnp.stack(new_shift_s),
            stacked(ret_states, 1))
```

```python
import functools
import math

import jax
import jax.numpy as jnp
from jax import lax
from jax.experimental import pallas as pl
from jax.experimental.pallas import tpu as pltpu

F32 = jnp.float32
BF16 = jnp.bfloat16

CHUNK = 64
LANES = 128
DMA_PRIORITY_THREADS = 2
VMEM_LIMIT_BYTES = 56 * 1024 * 1024

LN_EPS = 1e-5
GLA_TAU = 16.0
GLA_NORM_EPS = 1e-5
RWKV_HEAD = 64
RWKV_GN_EPS = 64e-5
RET_ROPE_BASE = 10000.0
RET_GN_EPS = 1e-5
MOE_TOP_K = 2
MOE_BLOCK = 512

_HI = lax.Precision.HIGHEST


def _cparams(sem):
    return pltpu.CompilerParams(dimension_semantics=sem, vmem_limit_bytes=VMEM_LIMIT_BYTES)


def _bdot(a, b):
    return jnp.dot(a.astype(BF16), b.astype(BF16), preferred_element_type=F32)


def _bdot_nt(a, b):
    return lax.dot_general(a.astype(BF16), b.astype(BF16), (((1,), (1,)), ((), ())),
                           preferred_element_type=F32)


def _chunk_prefix_sum(x):
    t_row = lax.broadcasted_iota(jnp.int32, (CHUNK, 3 * CHUNK), 0)
    t_col = lax.broadcasted_iota(jnp.int32, (CHUNK, 3 * CHUNK), 1) % CHUNK
    lower_ones = (t_row >= t_col).astype(BF16)
    hi = x.astype(BF16)
    rest = x - hi.astype(F32)
    mid = rest.astype(BF16)
    lo = (rest - mid.astype(F32)).astype(BF16)
    return jnp.dot(lower_ones, jnp.concatenate([hi, mid, lo], axis=0), preferred_element_type=F32)


def _log_sigmoid(z):
    return -(jnp.maximum(-z, 0.0) + jnp.log1p(jnp.exp(-jnp.abs(z))))


def _act(name, z):
    if name == "none":
        return z
    if name == "silu":
        return z * jax.nn.sigmoid(z)
    if name == "sigmoid":
        return jax.nn.sigmoid(z)
    if name == "tanh":
        return jnp.tanh(z)
    if name == "gla_gate":
        return _log_sigmoid(z) / GLA_TAU
    if name == "rwkv_decay":
        return -jnp.exp(_log_sigmoid(z) - 0.5)
    raise ValueError(name)


def _mm_body(*refs, act, has_bias, scale):
    if has_bias:
        x_ref, w_ref, b_ref, o_ref, wc_ref = refs
    else:
        x_ref, w_ref, o_ref, wc_ref = refs
        b_ref = None

    @pl.when(pl.program_id(1) == 0)
    def _():
        wc_ref[...] = w_ref[...].astype(BF16)

    acc = jnp.dot(x_ref[...], wc_ref[...], preferred_element_type=F32)
    if scale != 1.0:
        acc = acc * scale
    if has_bias:
        acc = acc + b_ref[...]
    o_ref[...] = _act(act, acc).astype(o_ref.dtype)


def _pick(n, pref):
    for t in pref:
        if n % t == 0:
            return t
    return n


MM_WEIGHT_TILE_ELEMS = 2 * 1024 * 1024


def matmul(x, w, slot=0, *, x_slot=None, act="none", bias=None, scale=1.0, out_dtype=F32):
    m, kdim = x.shape[-2:]
    _, kw, n = w.shape
    assert kw == kdim, (w.shape, x.shape)
    tm = _pick(m, (1024, 512, 256, 128, 64, 32, 16, 8))
    tn = _pick(n, tuple(t for t in (1024, 512, 256, 128) if t * kdim <= MM_WEIGHT_TILE_ELEMS))
    if x_slot is None:
        x_spec = pl.BlockSpec((tm, kdim), lambda j, i: (i, 0))
    else:
        x_spec = pl.BlockSpec((None, tm, kdim), lambda j, i: (x_slot, i, 0))
    in_specs = [x_spec, pl.BlockSpec((None, kdim, tn), lambda j, i: (slot, 0, j))]
    args = [x, w]
    if bias is not None:
        in_specs.append(pl.BlockSpec((1, tn), lambda j, i: (0, j)))
        args.append(bias.reshape(1, n).astype(F32))
    return pl.pallas_call(
        functools.partial(_mm_body, act=act, has_bias=bias is not None, scale=scale),
        grid=(n // tn, m // tm),
        in_specs=in_specs,
        out_specs=pl.BlockSpec((tm, tn), lambda j, i: (i, j)),
        out_shape=jax.ShapeDtypeStruct((m, n), out_dtype),
        scratch_shapes=[pltpu.VMEM((kdim, tn), BF16)],
        compiler_params=_cparams(("arbitrary", "arbitrary")),
        name="matmul_" + act,
    )(*args)


def _layer_norm_rows(z, g_ref, b_ref):
    mu = jnp.mean(z, -1, keepdims=True)
    zc = z - mu
    var = jnp.mean(zc * zc, -1, keepdims=True)
    return zc * lax.rsqrt(var + LN_EPS) * g_ref[...] + b_ref[...]


def _store_rows(y, oa_ref, ob_ref, head_tiles):
    if head_tiles is None:
        oa_ref[...] = y
        ob_ref[...] = y.astype(BF16)
    else:
        i = pl.program_id(0)

        @pl.when(i < head_tiles)
        def _():
            oa_ref[...] = y

        @pl.when(i >= head_tiles)
        def _():
            ob_ref[...] = y


def _ln_body(*refs, alpha, routed):
    if routed:
        x_ref, h_ref, g_ref, b_ref, router_ref, oa_ref, ob_ref, logit_ref = refs
    else:
        x_ref, h_ref, g_ref, b_ref, oa_ref, ob_ref = refs
    y = _layer_norm_rows(alpha * x_ref[...] + h_ref[...], g_ref, b_ref)
    if routed:
        logit_ref[...] = jnp.dot(y, router_ref[...], precision=_HI, preferred_element_type=F32)
    _store_rows(y, oa_ref, ob_ref, None)


LN_ROWS = 512


def _ln_out_specs(m, d, tm, split_rows):
    row = pl.BlockSpec((tm, d), lambda i, *_: (i, 0))
    if split_rows is None:
        return None, [row, row], [jax.ShapeDtypeStruct((m, d), F32),
                                  jax.ShapeDtypeStruct((m, d), BF16)]
    assert split_rows % tm == 0 and 0 < split_rows < m
    head_tiles = split_rows // tm
    specs = [pl.BlockSpec((tm, d), lambda i, *_: (jnp.minimum(i, head_tiles - 1), 0)),
             pl.BlockSpec((tm, d), lambda i, *_: (jnp.maximum(i - head_tiles, 0), 0))]
    shapes = [jax.ShapeDtypeStruct((split_rows, d), F32),
              jax.ShapeDtypeStruct((m - split_rows, d), F32)]
    return head_tiles, specs, shapes


def residual_layer_norm(x, h, g, b, alpha, router_w=None):
    m, d = x.shape
    tm = _pick(m, (LN_ROWS, 256, 128, 64, 32, 16, 8))
    row = pl.BlockSpec((tm, d), lambda i: (i, 0))
    vec = pl.BlockSpec((1, d), lambda i: (0, 0))
    _, out_specs, out_shape = _ln_out_specs(m, d, tm, None)
    r_specs, r_args = [], []
    if router_w is not None:
        n_logit = router_w.shape[1]
        r_specs, r_args = [pl.BlockSpec((d, n_logit), lambda i: (0, 0))], [router_w]
        out_specs = out_specs + [pl.BlockSpec((tm, n_logit), lambda i: (i, 0))]
        out_shape = out_shape + [jax.ShapeDtypeStruct((m, n_logit), F32)]
    return pl.pallas_call(
        functools.partial(_ln_body, alpha=alpha, routed=router_w is not None),
        grid=(m // tm,),
        in_specs=[row, row, vec, vec] + r_specs,
        out_specs=out_specs,
        out_shape=out_shape,
        compiler_params=_cparams(("parallel",)),
        name="residual_layer_norm",
    )(x, h, g.reshape(1, d), b.reshape(1, d), *r_args)


def _combine_ln_body(dest_ref, x_ref, ys_ref, gate_ref, g_ref, b_ref, oa_ref, ob_ref, buf_ref, sem_ref,
                     *, alpha, head_tiles, tm, top_k):
    i = pl.program_id(0)
    n_tiles = pl.num_programs(0)

    def issue(tile, slot):
        base = tile * (tm * top_k)

        def body(r, carry):
            for k in range(top_k):
                row = dest_ref[base + r * top_k + k]
                pltpu.make_async_copy(ys_ref.at[pl.ds(row, 1)], buf_ref.at[slot, k, pl.ds(r, 1)],
                                      sem_ref.at[slot]).start(priority=k % DMA_PRIORITY_THREADS)
            return carry

        lax.fori_loop(0, tm, body, 0, unroll=8)

    @pl.when(i == 0)
    def _():
        issue(0, 0)

    @pl.when(i + 1 < n_tiles)
    def _():
        issue(i + 1, (i + 1) % 2)

    slot = i % 2
    for k in range(top_k):
        pltpu.make_async_copy(ys_ref.at[pl.ds(0, tm)], buf_ref.at[slot, k], sem_ref.at[slot]).wait()
    h = buf_ref[slot, 0] * gate_ref[:, 0:1]
    for k in range(1, top_k):
        h = h + buf_ref[slot, k] * gate_ref[:, k:k + 1]
    y = _layer_norm_rows(alpha * x_ref[...] + h, g_ref, b_ref)
    _store_rows(y, oa_ref, ob_ref, head_tiles)


def moe_combine_layer_norm(x, ys, dest, gates, g, b, alpha, split_rows=None):
    m, d = x.shape
    top_k = dest.shape[1]
    tm = LN_ROWS
    assert m % tm == 0 and ys.dtype == F32
    head_tiles, out_specs, out_shape = _ln_out_specs(m, d, tm, split_rows)
    row = pl.BlockSpec((tm, d), lambda i, *_: (i, 0))
    vec = pl.BlockSpec((1, d), lambda i, *_: (0, 0))
    return pl.pallas_call(
        functools.partial(_combine_ln_body, alpha=alpha, head_tiles=head_tiles, tm=tm, top_k=top_k),
        grid_spec=pltpu.PrefetchScalarGridSpec(
            num_scalar_prefetch=1,
            grid=(m // tm,),
            in_specs=[row, pl.BlockSpec(memory_space=pl.ANY),
                      pl.BlockSpec((tm, top_k), lambda i, *_: (i, 0)), vec, vec],
            out_specs=out_specs,
            scratch_shapes=[pltpu.VMEM((2, top_k, tm, d), F32), pltpu.SemaphoreType.DMA((2,))]),
        out_shape=out_shape,
        compiler_params=_cparams(("arbitrary",)),
        name="moe_combine_layer_norm",
    )(dest.reshape(-1).astype(jnp.int32), x, ys, gates, g.reshape(1, d), b.reshape(1, d))


SUBLANES = 8


def _shift_mix_body(x_ref, halo_ref, shift_ref, mu_ref, o_ref, *, tm, tp, ts, prompt_tiles):
    i = pl.program_id(0)
    x = x_ref[...]
    row = lax.broadcasted_iota(jnp.int32, (tm, 1), 0)
    prev = jnp.where(row == 0, halo_ref[SUBLANES - 1:SUBLANES, :], pltpu.roll(x, 1, 0))
    prompt_start = jnp.logical_and(row == 0, (i * tm) % tp == 0)
    prev_prompt = jnp.where(prompt_start, 0.0, prev)
    nseq = tm // ts
    carried = jnp.broadcast_to(shift_ref[...][:, None, :], (nseq, ts, x.shape[1])).reshape(x.shape)
    prev_sample = jnp.where(row % ts == 0, carried, prev)
    xx = jnp.where(i < prompt_tiles, prev_prompt, prev_sample) - x
    for j in range(o_ref.shape[0]):
        o_ref[j] = (x + xx * mu_ref[j:j + 1, :]).astype(BF16)


def shift_mix(x, shift0, mu, *, n_prompt_rows, tp, ts):
    m, d = x.shape
    nmix = mu.shape[0]
    tm = 512
    assert tp % tm == 0 and tm % ts == 0 and n_prompt_rows % tm == 0 and m % tm == 0
    prompt_tiles = n_prompt_rows // tm
    nseq = tm // ts
    assert shift0.shape[0] % nseq == 0
    halo_blocks = tm // SUBLANES
    return pl.pallas_call(
        functools.partial(_shift_mix_body, tm=tm, tp=tp, ts=ts, prompt_tiles=prompt_tiles),
        grid=(m // tm,),
        in_specs=[pl.BlockSpec((tm, d), lambda i: (i, 0)),
                  pl.BlockSpec((SUBLANES, d), lambda i: (jnp.maximum(i * halo_blocks - 1, 0), 0)),
                  pl.BlockSpec((nseq, d), lambda i: (jnp.maximum(i - prompt_tiles, 0), 0)),
                  pl.BlockSpec((nmix, d), lambda i: (0, 0))],
        out_specs=pl.BlockSpec((nmix, tm, d), lambda i: (0, i, 0)),
        out_shape=jax.ShapeDtypeStruct((nmix, m, d), BF16),
        compiler_params=_cparams(("parallel",)),
        name="rwkv_shift_mix",
    )(x, x, shift0, mu)


def _chunk_state_init(c, ncp, cps, state_ref, s0_ref):
    @pl.when(jnp.logical_and(c < ncp, c % cps == 0))
    def _():
        state_ref[...] = jnp.zeros(state_ref.shape, state_ref.dtype)

    @pl.when(c >= ncp)
    def _():
        state_ref[...] = s0_ref[0]


def _dla_body(*refs, heads, dk, dv, mode, ncp, cps):
    o_ref, sp_ref, ss_ref, st_ref = refs[-4:]
    refs = refs[:-4]
    if mode == "gla":
        q_ref, k_ref, v_ref, g_ref, gate_ref, ng_ref, s0_ref = refs
    else:
        q_ref, k_ref, v_ref, cos_ref, sin_ref, gate_ref, ng_ref, s0_ref = refs
    c = pl.program_id(0)
    _chunk_state_init(c, ncp, cps, st_ref, s0_ref)

    t_row = lax.broadcasted_iota(jnp.int32, (CHUNK, CHUNK), 0)
    t_col = lax.broadcasted_iota(jnp.int32, (CHUNK, CHUNK), 1)
    causal = t_row >= t_col
    if mode == "gla":
        cum_all = _chunk_prefix_sum(g_ref[...])
    else:
        width = heads * dk
        even = (lax.broadcasted_iota(jnp.int32, (CHUNK, width), 1) % 2) == 0
        q_all = q_ref[...]
        k_all = k_ref[...]
        q_sw = jnp.where(even, pltpu.roll(q_all, width - 1, 1), pltpu.roll(q_all, 1, 1))
        k_sw = jnp.where(even, pltpu.roll(k_all, width - 1, 1), pltpu.roll(k_all, 1, 1))
        cos = cos_ref[...]
        sin = sin_ref[...]
        frame = (lax.broadcasted_iota(jnp.int32, (CHUNK, 1), 0) + 1).astype(F32)

    for h in range(heads):
        ks = slice(h * dk, (h + 1) * dk)
        vs = slice(h * dv, (h + 1) * dv)
        v = v_ref[:, vs]
        s_prev = st_ref[h]
        if mode == "gla":
            q = q_ref[:, ks]
            k = k_ref[:, ks]
            cum = cum_all[:, ks]
            total = cum[CHUNK - 1:CHUNK, :]
            total_col = jnp.transpose(jnp.broadcast_to(total, (LANES, dk)))[:, :1]
            state_decay = jnp.exp(total_col)
            q_dec = q * jnp.exp(cum)
            k_inv = k * jnp.exp(-cum)
            k_tail = k * jnp.exp(total - cum)
        else:
            q = q_all[:, ks] * cos + q_sw[:, ks] * sin
            k = (k_all[:, ks] * cos + k_sw[:, ks] * sin) * (dk ** -0.5)
            log_gamma = math.log1p(-(2.0 ** (-5.0 - h)))
            cum = frame * log_gamma
            total = CHUNK * log_gamma
            state_decay = math.exp(total)
            q_dec = q * jnp.exp(cum)
            k_inv = k * jnp.exp(-cum)
            k_tail = k * jnp.exp(total - cum)
        scores = jnp.where(causal, _bdot_nt(q_dec, k_inv), 0.0)
        o = _bdot(q_dec, s_prev) + _bdot(scores, v)
        k_tail_t = jnp.transpose(k_tail)
        st_ref[h] = state_decay * s_prev + _bdot(k_tail_t, v)
        if mode == "gla":
            o = o * lax.rsqrt(jnp.mean(o * o, -1, keepdims=True) + GLA_NORM_EPS) * ng_ref[...]
        else:
            mu = jnp.mean(o, -1, keepdims=True)
            oc = o - mu
            var = jnp.mean(oc * oc, -1, keepdims=True)
            o = oc * lax.rsqrt(var + RET_GN_EPS) * ng_ref[:, vs]
        if mode == "gla":
            o_ref[:, vs] = (o * gate_ref[:, vs]).astype(BF16)
        else:
            o_ref[:, vs] = (gate_ref[:, vs] * o).astype(BF16)

    @pl.when(jnp.logical_and(c < ncp, c % cps == cps - 1))
    def _():
        sp_ref[0] = st_ref[...]

    @pl.when(c >= ncp)
    def _():
        ss_ref[0] = st_ref[...]


def _seq_state_specs(state_shape, ncp, cps, n_prompt, slot):
    blk = (None, 1) + tuple(state_shape)
    zeros = (0,) * len(state_shape)
    s0_spec = pl.BlockSpec(blk, lambda c: (slot, jnp.maximum(c - ncp, 0)) + zeros)
    sp_spec = pl.BlockSpec(blk, lambda c: (0, jnp.minimum(c // cps, n_prompt - 1)) + zeros)
    ss_spec = pl.BlockSpec(blk, lambda c: (0, jnp.maximum(c - ncp, 0)) + zeros)
    return s0_spec, sp_spec, ss_spec


def _state_out_shapes(n_prompt, n_sample, state_shape):
    return [jax.ShapeDtypeStruct((1, n_prompt) + tuple(state_shape), F32),
            jax.ShapeDtypeStruct((1, n_sample) + tuple(state_shape), F32)]


def decay_attention(q, k, v, extra, gate, norm_g, s0, slot, *, mode, heads, n_prompt, cps):
    nt = q.shape[0]
    dk = q.shape[1] // heads
    dv = v.shape[1] // heads
    n_sample = s0.shape[1]
    ncp = n_prompt * cps
    nchunks = nt // CHUNK
    assert nchunks == ncp + n_sample
    rowk = pl.BlockSpec((CHUNK, heads * dk), lambda c: (c, 0))
    rowv = pl.BlockSpec((CHUNK, heads * dv), lambda c: (c, 0))
    s0_spec, sp_spec, ss_spec = _seq_state_specs((heads, dk, dv), ncp, cps, n_prompt, slot)
    if mode == "gla":
        extra_specs = [rowk]
        ng_spec = pl.BlockSpec((1, dv), lambda c: (0, 0))
        norm_g = norm_g.reshape(1, dv)
    else:
        pos_spec = pl.BlockSpec((CHUNK, dk), lambda c: (jnp.where(c < ncp, c % cps, cps), 0))
        extra_specs = [pos_spec, pos_spec]
        ng_spec = pl.BlockSpec((1, heads * dv), lambda c: (0, 0))
        norm_g = norm_g.reshape(1, heads * dv)
    return pl.pallas_call(
        functools.partial(_dla_body, heads=heads, dk=dk, dv=dv, mode=mode, ncp=ncp, cps=cps),
        grid=(nchunks,),
        in_specs=[rowk, rowk, rowv] + extra_specs + [rowv, ng_spec, s0_spec],
        out_specs=[rowv, sp_spec, ss_spec],
        out_shape=([jax.ShapeDtypeStruct((nt, heads * dv), BF16)]
                   + _state_out_shapes(n_prompt, n_sample, (heads, dk, dv))),
        scratch_shapes=[pltpu.VMEM((heads, dk, dv), F32)],
        compiler_params=_cparams(("arbitrary",)),
        name="decay_attention_" + mode,
    )(q, k, v, *extra, gate, norm_g, s0)


def _rwkv_body(r_ref, k_ref, v_ref, lw_ref, a_ref, g_ref, kk_ref, ka_ref, rk_ref, gng_ref, gnb_ref,
               s0_ref, o_ref, sp_ref, ss_ref, st_ref, *, pairs, ncp, cps):
    c = pl.program_id(0)
    n = RWKV_HEAD
    w2 = 2 * n

    @pl.when(jnp.logical_and(c < ncp, c % cps == 0))
    def _():
        st_ref[...] = jnp.zeros(st_ref.shape, F32)

    @pl.when(c >= ncp)
    def _():
        zero = jnp.zeros((n, n), F32)
        for p in range(pairs):
            top = jnp.concatenate([s0_ref[0, 2 * p], zero], axis=1)
            bot = jnp.concatenate([zero, s0_ref[0, 2 * p + 1]], axis=1)
            st_ref[p] = jnp.concatenate([top, bot], axis=0)

    def paired(ref):
        x = ref[...]
        return jnp.stack([x[:, p * w2:(p + 1) * w2] for p in range(pairs)])

    lane = lax.broadcasted_iota(jnp.int32, (1, 1, w2), 2)
    first = lane < n

    def head_sum(x):
        s_a = jnp.sum(jnp.where(first, x, 0.0), -1, keepdims=True)
        s_b = jnp.sum(jnp.where(first, 0.0, x), -1, keepdims=True)
        return jnp.where(first, s_a, s_b)

    def stacked(x):
        return jnp.concatenate([jnp.where(first, x, 0.0), jnp.where(first, 0.0, x)], axis=1)

    def bmm(a, b):
        return lax.dot_general(a.astype(BF16), b.astype(BF16), (((2,), (1,)), ((0,), (0,))),
                               preferred_element_type=F32)

    def bmm_nt(a, b):
        return lax.dot_general(a.astype(BF16), b.astype(BF16), (((2,), (2,)), ((0,), (0,))),
                               preferred_element_type=F32)

    r = paired(r_ref)
    k_raw = paired(k_ref)
    v = paired(v_ref)
    lw = paired(lw_ref)
    a = paired(a_ref)
    k_k = paired(kk_ref)
    k_a = paired(ka_ref)
    r_k = paired(rk_ref)

    kk = k_raw * k_k
    kk = kk / jnp.maximum(jnp.sqrt(head_sum(kk * kk)), 1e-12)
    k_h = k_raw * (1.0 + (a - 1.0) * k_a)
    a_vec = -kk
    b_vec = kk * a

    cum = paired_value(_chunk_prefix_sum(lw_ref[...]), pairs, w2)
    total = cum[:, CHUNK - 1:CHUNK, :]
    p_now = jnp.exp(cum)
    p_prev = jnp.exp(cum - lw)
    p_inv = jnp.exp(-cum)
    p_tail = jnp.exp(total - cum)

    lhs = jnp.concatenate([stacked(a_vec * p_prev), stacked(r * p_now)], axis=1)
    rhs = jnp.concatenate([stacked(b_vec * p_inv), stacked(k_h * p_inv)], axis=1)
    sc = bmm_nt(lhs, rhs)
    s_prev = st_ref[...]
    sr = bmm_nt(lhs, s_prev)

    i_row = lax.broadcasted_iota(jnp.int32, (1, w2, w2), 1)
    i_col = lax.broadcasted_iota(jnp.int32, (1, w2, w2), 2)
    same = (i_row // n) == (i_col // n)
    strict = jnp.logical_and(same, (i_col % n) < (i_row % n))
    incl = jnp.logical_and(same, (i_col % n) <= (i_row % n))
    a_ab = jnp.where(strict, sc[:, :w2, :w2], 0.0)
    a_ak = jnp.where(strict, sc[:, :w2, w2:], 0.0)
    a_rb = jnp.where(incl, sc[:, w2:, :w2], 0.0)
    a_rk = jnp.where(incl, sc[:, w2:, w2:], 0.0)
    u0 = sr[:, :w2]
    y0 = sr[:, w2:]
    v_st = stacked(v)

    eye = (i_row == i_col).astype(F32)
    t_inv = eye + a_ab
    power = a_ab
    span = 1
    while 2 * span < CHUNK:
        power = bmm(power, power)
        t_inv = t_inv + bmm(t_inv, power)
        span *= 2

    u_st = bmm(t_inv, u0 + bmm(a_ak, v_st))
    uv = jnp.concatenate([u_st, v_st], axis=1)
    y_st = y0 + bmm(jnp.concatenate([a_rb, a_rk], axis=2), uv)
    y = y_st[:, :n] + y_st[:, n:]

    tails = jnp.concatenate([stacked(b_vec * p_tail), stacked(k_h * p_tail)], axis=1)
    uv_t = jnp.swapaxes(uv, 1, 2)
    s_new = s_prev * jnp.exp(total) + bmm(uv_t, tails)
    st_ref[...] = s_new

    mu = head_sum(y) * (1.0 / n)
    yc = y - mu
    var = head_sum(yc * yc) * (1.0 / n)
    yn = yc * lax.rsqrt(var + RWKV_GN_EPS) * paired(gng_ref) + paired(gnb_ref)
    out = yn + head_sum(r * k_h * r_k) * v
    gate = paired(g_ref)
    for p in range(pairs):
        o_ref[:, p * w2:(p + 1) * w2] = (out[p] * gate[p]).astype(BF16)

    def store_state(dst_ref):
        for p in range(pairs):
            dst_ref[0, 2 * p] = s_new[p, :n, :n]
            dst_ref[0, 2 * p + 1] = s_new[p, n:, n:]

    @pl.when(jnp.logical_and(c < ncp, c % cps == cps - 1))
    def _():
        store_state(sp_ref)

    @pl.when(c >= ncp)
    def _():
        store_state(ss_ref)


def paired_value(x, pairs, w2):
    return jnp.stack([x[:, p * w2:(p + 1) * w2] for p in range(pairs)])


def rwkv7_attention(r, k, v, lw, a, g, k_k, k_a, r_k, gn_g, gn_b, s0, slot, *, n_prompt, cps):
    nt, d = r.shape
    heads = d // RWKV_HEAD
    pairs = heads // 2
    n_sample = s0.shape[1]
    ncp = n_prompt * cps
    assert nt // CHUNK == ncp + n_sample
    row = pl.BlockSpec((CHUNK, d), lambda c: (c, 0))
    vec = pl.BlockSpec((1, d), lambda c: (0, 0))
    s0_spec, sp_spec, ss_spec = _seq_state_specs((heads, RWKV_HEAD, RWKV_HEAD), ncp, cps, n_prompt,
                                                 slot)
    vecs = [u.reshape(1, d) for u in (k_k, k_a, r_k, gn_g, gn_b)]
    return pl.pallas_call(
        functools.partial(_rwkv_body, pairs=pairs, ncp=ncp, cps=cps),
        grid=(nt // CHUNK,),
        in_specs=[row] * 6 + [vec] * 5 + [s0_spec],
        out_specs=[row, sp_spec, ss_spec],
        out_shape=([jax.ShapeDtypeStruct((nt, d), BF16)]
                   + _state_out_shapes(n_prompt, n_sample, (heads, RWKV_HEAD, RWKV_HEAD))),
        scratch_shapes=[pltpu.VMEM((pairs, 2 * RWKV_HEAD, 2 * RWKV_HEAD), F32)],
        compiler_params=_cparams(("arbitrary",)),
        name="rwkv7_attention",
    )(r, k, v, lw, a, g, *vecs, s0)


def _expert_weight_pipeline(be_ref, chg_ref, nxt_ref, hbm_refs, stage_refs, cache_refs, sem_ref,
                            tile):
    j = pl.program_id(0)
    b = pl.program_id(1)
    n_pass = pl.num_programs(0)

    def copies(expert, col_pass):
        col = pl.multiple_of(col_pass * tile, tile)
        return [pltpu.make_async_copy(hbm.at[expert, :, pl.ds(col, tile)], stage, sem_ref.at[i])
                for i, (hbm, stage) in enumerate(zip(hbm_refs, stage_refs))]

    @pl.when(jnp.logical_and(j == 0, b == 0))
    def _():
        for cp in copies(be_ref[0], 0):
            cp.start()

    @pl.when(chg_ref[b] == 1)
    def _():
        for cp in copies(be_ref[b], j):
            cp.wait()
        for stage, cache in zip(stage_refs, cache_refs):
            cache[...] = stage[...].astype(BF16)
        nxt = nxt_ref[b]

        @pl.when(nxt >= 0)
        def _():
            for cp in copies(be_ref[jnp.maximum(nxt, 0)], j):
                cp.start()

        @pl.when(jnp.logical_and(nxt < 0, j + 1 < n_pass))
        def _():
            for cp in copies(be_ref[0], j + 1):
                cp.start()


def _row_block_compute(valid, out_ref, rows_fn):
    block = out_ref.shape[0]
    half = block // 2

    @pl.when(valid > half)
    def _():
        out_ref[...] = rows_fn(0, block)

    @pl.when(jnp.logical_and(valid > 0, valid <= half))
    def _():
        out_ref[:half, :] = rows_fn(0, half)
        out_ref[half:, :] = jnp.zeros((block - half, out_ref.shape[1]), out_ref.dtype)

    @pl.when(valid <= 0)
    def _():
        out_ref[...] = jnp.zeros(out_ref.shape, out_ref.dtype)


def _gate_up_body(be_ref, chg_ref, nxt_ref, valid_ref, x_ref, w1_ref, w3_ref, h_ref,
                  w1s_ref, w3s_ref, w1c_ref, w3c_ref, sem_ref, *, tile):
    b = pl.program_id(1)
    _expert_weight_pipeline(be_ref, chg_ref, nxt_ref, (w1_ref, w3_ref), (w1s_ref, w3s_ref),
                            (w1c_ref, w3c_ref), sem_ref, tile)

    def rows(lo, n):
        x = x_ref[lo:lo + n, :]
        gate = jnp.dot(x, w1c_ref[...], preferred_element_type=F32)
        up = jnp.dot(x, w3c_ref[...], preferred_element_type=F32)
        return (gate * jax.nn.sigmoid(gate) * up).astype(BF16)

    _row_block_compute(valid_ref[b], h_ref, rows)


def _down_body(be_ref, chg_ref, nxt_ref, valid_ref, h_ref, w2_ref, y_ref, w2s_ref, w2c_ref, sem_ref,
               *, tile):
    b = pl.program_id(1)
    _expert_weight_pipeline(be_ref, chg_ref, nxt_ref, (w2_ref,), (w2s_ref,), (w2c_ref,), sem_ref,
                            tile)

    def rows(lo, n):
        return jnp.dot(h_ref[lo:lo + n, :], w2c_ref[...],
                       preferred_element_type=F32).astype(y_ref.dtype)

    _row_block_compute(valid_ref[b], y_ref, rows)


def _block_meta(block_expert, rows_valid):
    be = block_expert.astype(jnp.int32)
    nb = be.shape[0]
    changed = jnp.concatenate([jnp.ones((1,), jnp.int32),
                               (be[1:] != be[:-1]).astype(jnp.int32)])
    idx = jnp.arange(nb, dtype=jnp.int32)
    opener = jnp.where(changed == 1, idx, nb)
    after = lax.cummin(jnp.concatenate([opener[1:], jnp.full((1,), nb, jnp.int32)]), reverse=True)
    nxt = jnp.where(after >= nb, -1, after).astype(jnp.int32)
    return be, changed, nxt, rows_valid.astype(jnp.int32)


def swiglu_gate_up(xs, block_expert, rows_valid, w1, w3, block, tile_f=(512, 256, 128)):
    rows, d = xs.shape
    f = w1.shape[-1]
    tf = _pick(f, tile_f)
    hbm = pl.BlockSpec(memory_space=pl.ANY)
    return pl.pallas_call(
        functools.partial(_gate_up_body, tile=tf),
        grid_spec=pltpu.PrefetchScalarGridSpec(
            num_scalar_prefetch=4,
            grid=(f // tf, rows // block),
            in_specs=[pl.BlockSpec((block, d), lambda j, b, *_: (b, 0)), hbm, hbm],
            out_specs=pl.BlockSpec((block, tf), lambda j, b, *_: (b, j)),
            scratch_shapes=[pltpu.VMEM((d, tf), F32), pltpu.VMEM((d, tf), F32),
                            pltpu.VMEM((d, tf), BF16), pltpu.VMEM((d, tf), BF16),
                            pltpu.SemaphoreType.DMA((2,))]),
        out_shape=jax.ShapeDtypeStruct((rows, f), BF16),
        compiler_params=_cparams(("arbitrary", "arbitrary")),
        name="swiglu_gate_up",
    )(*_block_meta(block_expert, rows_valid), xs, w1, w3)


def swiglu_down(h, block_expert, rows_valid, w2, block, out_dtype=F32):
    rows, f = h.shape
    d = w2.shape[-1]
    tn = _pick(d, (512, 256, 128))
    return pl.pallas_call(
        functools.partial(_down_body, tile=tn),
        grid_spec=pltpu.PrefetchScalarGridSpec(
            num_scalar_prefetch=4,
            grid=(d // tn, rows // block),
            in_specs=[pl.BlockSpec((block, f), lambda j, b, *_: (b, 0)),
                      pl.BlockSpec(memory_space=pl.ANY)],
            out_specs=pl.BlockSpec((block, tn), lambda j, b, *_: (b, j)),
            scratch_shapes=[pltpu.VMEM((f, tn), F32), pltpu.VMEM((f, tn), BF16),
                            pltpu.SemaphoreType.DMA((1,))]),
        out_shape=jax.ShapeDtypeStruct((rows, d), out_dtype),
        compiler_params=_cparams(("arbitrary", "arbitrary")),
        name="swiglu_down",
    )(*_block_meta(block_expert, rows_valid), h, w2)


def _dispatch_body(src_ref, nu_ref, x_ref, o_ref, buf_ref, sem_ref, *, block):
    b = pl.program_id(0)
    n_used = nu_ref[0]

    def issue(blk, slot):
        base = blk * block

        def body(pair, carry):
            for t in range(DMA_PRIORITY_THREADS):
                r = pair * DMA_PRIORITY_THREADS + t
                pltpu.make_async_copy(x_ref.at[pl.ds(src_ref[base + r], 1)],
                                      buf_ref.at[slot, pl.ds(r, 1)], sem_ref.at[slot]).start(priority=t)
            return carry

        lax.fori_loop(0, block // DMA_PRIORITY_THREADS, body, 0, unroll=4)

    @pl.when(jnp.logical_and(b == 0, n_used > 0))
    def _():
        issue(0, 0)

    @pl.when(b + 1 < n_used)
    def _():
        issue(b + 1, (b + 1) % 2)

    @pl.when(b < n_used)
    def _():
        slot = b % 2
        pltpu.make_async_copy(x_ref.at[pl.ds(0, block)], buf_ref.at[slot], sem_ref.at[slot]).wait()
        o_ref[...] = buf_ref[slot].astype(BF16)

    @pl.when(b >= n_used)
    def _():
        o_ref[...] = jnp.zeros(o_ref.shape, BF16)


def moe_dispatch(x, src_tok, n_used, block):
    n, d = x.shape
    rows = src_tok.shape[0]
    return pl.pallas_call(
        functools.partial(_dispatch_body, block=block),
        grid_spec=pltpu.PrefetchScalarGridSpec(
            num_scalar_prefetch=2,
            grid=(rows // block,),
            in_specs=[pl.BlockSpec(memory_space=pl.ANY)],
            out_specs=pl.BlockSpec((block, d), lambda b, *_: (b, 0)),
            scratch_shapes=[pltpu.VMEM((2, block, d), F32), pltpu.SemaphoreType.DMA((2,))]),
        out_shape=jax.ShapeDtypeStruct((rows, d), BF16),
        compiler_params=_cparams(("arbitrary",)),
        name="moe_dispatch",
    )(src_tok.astype(jnp.int32), jnp.reshape(n_used, (1,)).astype(jnp.int32), x)


def _pad_cols(w, mult=LANES):
    pad = (-w.shape[-1]) % mult
    return jnp.pad(w, [(0, 0)] * (w.ndim - 1) + [(0, pad)]) if pad else w


def _pad_rows(w, mult=LANES):
    pad = (-w.shape[-2]) % mult
    return jnp.pad(w, [(0, 0)] * (w.ndim - 2) + [(0, pad), (0, 0)]) if pad else w


DENSE_GATE_UP_BLOCK = 1024
DENSE_DOWN_BLOCK = 512
MOE_GATE_UP_TILE_F = (1024, 512, 256, 128)


def dense_swiglu(xb, w1, w3, w2, slot):
    rows = xb.shape[0]
    assert rows % DENSE_GATE_UP_BLOCK == 0 and rows % DENSE_DOWN_BLOCK == 0
    nb_a = rows // DENSE_GATE_UP_BLOCK
    nb_b = rows // DENSE_DOWN_BLOCK
    h = swiglu_gate_up(xb, jnp.full((nb_a,), slot, jnp.int32),
                       jnp.full((nb_a,), DENSE_GATE_UP_BLOCK, jnp.int32), w1, w3, DENSE_GATE_UP_BLOCK)
    return swiglu_down(h, jnp.full((nb_b,), slot, jnp.int32),
                       jnp.full((nb_b,), DENSE_DOWN_BLOCK, jnp.int32), w2, DENSE_DOWN_BLOCK)


def moe_swiglu(logits, xf, w1, w3, w2, slot):
    n, d = xf.shape
    n_exp = logits.shape[-1]
    top_val, top_idx = lax.top_k(logits, MOE_TOP_K)
    gates = jax.nn.softmax(top_val, axis=-1)
    slots = n * MOE_TOP_K
    flat_e = top_idx.reshape(-1)
    onehot = (flat_e[:, None] == jnp.arange(n_exp, dtype=flat_e.dtype)[None, :]).astype(jnp.int32)
    rank = jnp.sum((jnp.cumsum(onehot, axis=0) - onehot) * onehot, axis=1)
    counts = jnp.sum(onehot, axis=0)
    start = jnp.cumsum(counts) - counts
    padded = (counts + MOE_BLOCK - 1) // MOE_BLOCK * MOE_BLOCK
    pend = jnp.cumsum(padded)
    pstart = pend - padded
    nb = (slots + n_exp * (MOE_BLOCK - 1) + MOE_BLOCK - 1) // MOE_BLOCK
    dest = (pstart[flat_e] + rank).astype(jnp.int32)
    block_start = jnp.arange(nb, dtype=jnp.int32) * MOE_BLOCK
    expert_of_block = jnp.clip(jnp.searchsorted(pend, block_start, side="right"), 0, n_exp - 1)
    order = jnp.argsort(flat_e, stable=True)
    e_row = jnp.repeat(expert_of_block, MOE_BLOCK)
    rank_row = jnp.arange(nb * MOE_BLOCK, dtype=jnp.int32) - pstart[e_row]
    valid = rank_row < counts[e_row]
    slot_row = order[jnp.clip(start[e_row] + rank_row, 0, slots - 1)]
    src_tok = jnp.where(valid, slot_row // MOE_TOP_K, 0).astype(jnp.int32)
    block_expert = expert_of_block.astype(jnp.int32) + slot * n_exp
    n_used = pend[-1] // MOE_BLOCK
    xs = moe_dispatch(xf, src_tok, n_used, MOE_BLOCK)
    e1 = w1.reshape((-1,) + w1.shape[2:])
    e3 = w3.reshape((-1,) + w3.shape[2:])
    e2 = w2.reshape((-1,) + w2.shape[2:])
    rows_valid = jnp.clip((pstart + counts)[expert_of_block] - block_start, 0, MOE_BLOCK)
    hs = swiglu_gate_up(xs, block_expert, rows_valid, e1, e3, MOE_BLOCK, tile_f=MOE_GATE_UP_TILE_F)
    ys = swiglu_down(hs, block_expert, rows_valid, e2, MOE_BLOCK)
    return ys, dest.reshape(n, MOE_TOP_K), gates


def _rope_tables(dk, cps, past_len):
    half = dk // 2
    inv_freq = 1.0 / (RET_ROPE_BASE ** jnp.linspace(0.0, 1.0, half, dtype=F32))
    pos = jnp.concatenate([jnp.arange(cps * CHUNK, dtype=jnp.int32),
                           past_len + jnp.arange(CHUNK, dtype=jnp.int32)])
    ang = pos.astype(F32)[:, None] * inv_freq[None, :]
    cos = jnp.repeat(jnp.cos(ang), 2, axis=1)
    sin = jnp.stack([-jnp.sin(ang), jnp.sin(ang)], axis=-1).reshape(pos.shape[0], dk)
    return cos, sin


def kernel(x_prompt, x_sample, state_gla, state_rwkv, state_shift, state_ret, ln_g, ln_b,
           gla_wq, gla_wk, gla_wv, gla_wr, gla_wa1, gla_wa2, gla_ba, gla_norm_g, gla_wo,
           rwkv_mu, rwkv_wr, rwkv_wk, rwkv_wv, rwkv_wo, rwkv_w0, rwkv_w1, rwkv_w2, rwkv_a0, rwkv_a1,
           rwkv_a2, rwkv_g1, rwkv_g2, rwkv_k_k, rwkv_k_a, rwkv_r_k, rwkv_gn_g, rwkv_gn_b,
           ret_wq, ret_wk, ret_wv, ret_wg, ret_gn_g, ret_wo,
           ffn_w1, ffn_w3, ffn_w2, moe_router, moe_w1, moe_w3, moe_w2):
    bp, tp, d = x_prompt.shape
    bs, ts, _ = x_sample.shape
    assert ts == CHUNK and tp % CHUNK == 0
    depth = ln_g.shape[0]
    alpha = (2.0 * depth) ** 0.25
    cps = tp // CHUNK
    past_len = tp
    n_prompt_rows = bp * tp
    seq = dict(n_prompt=bp, cps=cps)

    gla_heads = state_gla.shape[2]
    gla_dk = state_gla.shape[3]
    ret_heads = state_ret.shape[2]
    ret_dk = state_ret.shape[3]

    xf = jnp.concatenate([x_prompt.reshape(-1, d), x_sample.reshape(-1, d)], axis=0)
    xb = xf.astype(BF16)

    gla_states, rwkv_states, ret_states = [], [], []
    new_shift_p, new_shift_s = [], []

    def stacked(per_layer, which):
        parts = [states[which] for states in per_layer]
        return parts[0] if len(parts) == 1 else jnp.concatenate(parts, axis=0)

    last_rows = jnp.concatenate([jnp.arange(bp, dtype=jnp.int32) * tp + (tp - 1),
                                 n_prompt_rows + jnp.arange(bs, dtype=jnp.int32) * ts + (ts - 1)])
    for i in range(depth):
        kind, slot = i % 3, i // 3
        if kind == 0:
            q = matmul(xb, gla_wq, slot, scale=gla_dk ** -0.5)
            k = matmul(xb, gla_wk, slot)
            v = matmul(xb, gla_wv, slot)
            gate = matmul(xb, gla_wr, slot, act="silu")
            low = matmul(xb, _pad_cols(gla_wa1), slot, out_dtype=BF16)
            log_alpha = matmul(low, _pad_rows(gla_wa2), slot, act="gla_gate", bias=gla_ba[slot])
            o, *states = decay_attention(q, k, v, (log_alpha,), gate, gla_norm_g[slot], state_gla,
                                         slot, mode="gla", heads=gla_heads, **seq)
            gla_states.append(states)
            h = matmul(o, gla_wo, slot)
        elif kind == 1:
            mixes = shift_mix(xf, state_shift[slot], rwkv_mu[slot], n_prompt_rows=n_prompt_rows,
                              tp=tp, ts=ts)
            i_r, i_w, i_k, i_v, i_a, i_g = range(6)
            r = matmul(mixes, rwkv_wr, slot, x_slot=i_r)
            k = matmul(mixes, rwkv_wk, slot, x_slot=i_k)
            v = matmul(mixes, rwkv_wv, slot, x_slot=i_v)
            w_mid = matmul(mixes, _pad_cols(rwkv_w1), slot, x_slot=i_w, act="tanh", out_dtype=BF16)
            log_decay = matmul(w_mid, _pad_rows(rwkv_w2), slot, act="rwkv_decay", bias=rwkv_w0[slot])
            a_mid = matmul(mixes, _pad_cols(rwkv_a1), slot, x_slot=i_a, out_dtype=BF16)
            a = matmul(a_mid, _pad_rows(rwkv_a2), slot, act="sigmoid", bias=rwkv_a0[slot])
            g_mid = matmul(mixes, _pad_cols(rwkv_g1), slot, x_slot=i_g, act="sigmoid", out_dtype=BF16)
            g = matmul(g_mid, _pad_rows(rwkv_g2), slot)
            o, *states = rwkv7_attention(r, k, v, log_decay, a, g, rwkv_k_k[slot], rwkv_k_a[slot],
                                         rwkv_r_k[slot].reshape(-1), rwkv_gn_g[slot],
                                         rwkv_gn_b[slot], state_rwkv, slot, **seq)
            rwkv_states.append(states)
            h = matmul(o, rwkv_wo, slot)
            ends = jnp.take(xf, last_rows, axis=0, mode="clip")
            new_shift_p.append(ends[:bp])
            new_shift_s.append(ends[bp:])
        else:
            q = matmul(xb, ret_wq, slot)
            k = matmul(xb, ret_wk, slot)
            v = matmul(xb, ret_wv, slot)
            gate = matmul(xb, ret_wg, slot, act="silu")
            cos, sin = _rope_tables(ret_dk, cps, past_len)
            o, *states = decay_attention(q, k, v, (cos, sin), gate, ret_gn_g[slot], state_ret, slot,
                                         mode="ret", heads=ret_heads, **seq)
            ret_states.append(states)
            h = matmul(o, ret_wo, slot)
        fslot = i // 2
        is_moe = i % 2 == 1
        router_w = _pad_cols(moe_router[fslot]) if is_moe else None
        xf, xb, *logits = residual_layer_norm(xf, h, ln_g[i, 0], ln_b[i, 0], alpha, router_w=router_w)
        last = i == depth - 1
        if is_moe:
            ys, dest, gates = moe_swiglu(logits[0][:, :moe_router.shape[-1]], xf, moe_w1, moe_w3,
                                         moe_w2, fslot)
            out_a, out_b = moe_combine_layer_norm(xf, ys, dest, gates, ln_g[i, 1], ln_b[i, 1], alpha,
                                                  split_rows=n_prompt_rows if last else None)
        else:
            h = dense_swiglu(xb, ffn_w1, ffn_w3, ffn_w2, fslot)
            out_a, out_b = residual_layer_norm(xf, h, ln_g[i, 1], ln_b[i, 1], alpha)
            if last:
                out_a, out_b = out_a[:n_prompt_rows], out_a[n_prompt_rows:]
        if last:
            y_prompt, y_sample = out_a.reshape(bp, tp, d), out_b.reshape(bs, ts, d)
        else:
            xf, xb = out_a, out_b

    return (y_prompt, y_sample,
            stacked(gla_states, 0), stacked(rwkv_states, 0), jnp.stack(new_shift_p),
            stacked(ret_states, 0),
            stacked(gla_states, 1), stacked(rwkv_states, 1), jnp.stack(new_shift_s),
            stacked(ret_states, 1))
```
